```python
import jax, jax.numpy as jnp
from jax import lax
import numpy as np

D_MODEL = 1024
BATCH = 8
SEQ = 2048
DEPTH = 1
DEC_BATCH = 32
DEC_SEQ = 4
PAST_LEN = 16384
PAGE_SIZE = 128

N_HEADS = 8
HEAD_DIM = 64
N_KV = 2
HPG = N_HEADS // N_KV
NSA_W = N_HEADS * HEAD_DIM
KV_W = N_KV * HEAD_DIM
CMP_LEN = 32
CMP_STRIDE = 16
CMP_R = CMP_LEN // CMP_STRIDE
CMP_HID = HEAD_DIM
SLC_LEN = 64
TOP_N = 16
WINDOW = 512
CONV_C = D_MODEL // 2
CONV_W = 31
Q_BLOCK = 64
SPLITS = (NSA_W, 6 * KV_W, 3 * N_HEADS, NSA_W, 2 * CONV_C, CONV_C, 2 * D_MODEL)
D_IN = NSA_W + 6 * KV_W + 3 * N_HEADS + NSA_W + 2 * CONV_C + CONV_C + 2 * D_MODEL
EPS = 1e-6
FORCED_SCORE = 1e4
NEG_INF = -1e30

kernel_name = 'nsa_conformer_gated_hybrid_step'


def split_cols(z, widths):
    outs, start = [], 0
    for w in widths:
        outs.append(z[..., start:start + w])
        start += w
    return outs


def rmsnorm(x, g):
    xf = x.astype(jnp.float32)
    y = xf * lax.rsqrt(jnp.mean(xf * xf, axis=-1, keepdims=True) + EPS)
    return (y * g.astype(jnp.float32)).astype(x.dtype)


def layernorm(x, g, b):
    xf = x.astype(jnp.float32)
    mu = jnp.mean(xf, axis=-1, keepdims=True)
    var = jnp.mean(jnp.square(xf - mu), axis=-1, keepdims=True)
    y = (xf - mu) * lax.rsqrt(var + EPS) * g.astype(jnp.float32) + b.astype(jnp.float32)
    return y.astype(x.dtype)


def masked_softmax(s, mask):
    s = jnp.where(mask, s.astype(jnp.float32), NEG_INF)
    e = jnp.where(mask, jnp.exp(s - jnp.max(s, axis=-1, keepdims=True)), 0.0)
    return e / jnp.maximum(jnp.sum(e, axis=-1, keepdims=True), 1e-30)


def compress(k, pe, w1, w2):
    B, T = k.shape[:2]
    n_chunk = T // CMP_STRIDE
    n_cmp = n_chunk - CMP_R + 1
    kc = k[:, :n_chunk * CMP_STRIDE].reshape(B, n_chunk, CMP_STRIDE, N_KV, HEAD_DIM)
    w1r = w1.reshape(CMP_R, CMP_STRIDE, HEAD_DIM, CMP_HID)
    per = jnp.einsum('bcjgd,rjdh->rbcgh', kc, w1r)
    h = jnp.einsum('rjd,rjdh->h', pe.reshape(CMP_R, CMP_STRIDE, HEAD_DIM), w1r)
    for r in range(CMP_R):
        h = h + per[r, :, r:r + n_cmp]
    return jax.nn.silu(h) @ w2


def to_blocks(k):
    B, T = k.shape[:2]
    n_slc = -(-T // SLC_LEN)
    k = jnp.pad(k, ((0, 0), (0, n_slc * SLC_LEN - T), (0, 0), (0, 0)))
    return k.reshape(B, n_slc, SLC_LEN, N_KV, HEAD_DIM).transpose(0, 3, 1, 2, 4)


def cmp_to_slc(n_cmp, n_slc):
    i = jnp.arange(n_cmp)[:, None] * CMP_STRIDE
    j = jnp.arange(n_slc)[None, :] * SLC_LEN
    return ((i < j + SLC_LEN) & (i + CMP_LEN > j)).astype(jnp.float32)


def nsa_context(k_c, v_c, k_s, v_s, pe_k, w1_k, w2_k, pe_v, w1_v, w2_v):
    kc = compress(k_c, pe_k, w1_k, w2_k)
    vc = compress(v_c, pe_v, w1_v, w2_v)
    n_cmp = kc.shape[1]
    c_end = jnp.arange(n_cmp) * CMP_STRIDE + (CMP_LEN - 1)
    ks_bg, vs_bg = to_blocks(k_s), to_blocks(v_s)
    agg = cmp_to_slc(n_cmp, ks_bg.shape[2])
    return kc, vc, c_end, agg, ks_bg, vs_bg


def nsa_block(q, gates, q_pos, ctx, kw, vw, w_pos):
    kc, vc, c_end, agg, ks_bg, vs_bg = ctx
    B, Q = q.shape[:2]
    qf = q * (HEAD_DIM ** -0.5)
    s_c = jnp.einsum('bqghd,bcgd->bqghc', qf, kc)
    m_c = (c_end[None, :] <= q_pos[:, None])[None, :, None, None, :]
    p_c = masked_softmax(s_c, m_c)
    o_c = jnp.einsum('bqghc,bcgd->bqghd', p_c.astype(vc.dtype), vc)
    n_slc = ks_bg.shape[2]
    imp = jnp.einsum('bqgc,cj->bqgj', jnp.sum(p_c, axis=3), agg)
    j = jnp.arange(n_slc)[None, :]
    cur = (q_pos // SLC_LEN)[:, None]
    valid = j * SLC_LEN <= q_pos[:, None]
    forced = (j == 0) | (j == cur) | (j == cur - 1)
    score = jnp.where(forced[None, :, None, :], FORCED_SCORE, imp)
    score = jnp.where(valid[None, :, None, :], score, -1.0)
    n_top = min(TOP_N, n_slc)
    _, idx = lax.top_k(score, n_top)
    idx = idx.transpose(0, 2, 1, 3)
    b_i = jnp.arange(B)[:, None, None, None]
    g_i = jnp.arange(N_KV)[None, :, None, None]
    n_sel = n_top * SLC_LEN
    k_sel = ks_bg[b_i, g_i, idx].reshape(B, N_KV, Q, n_sel, HEAD_DIM)
    v_sel = vs_bg[b_i, g_i, idx].reshape(B, N_KV, Q, n_sel, HEAD_DIM)
    key_pos = (idx[..., None] * SLC_LEN + jnp.arange(SLC_LEN)).reshape(B, N_KV, Q, n_sel)
    m_s = (key_pos <= q_pos[None, None, :, None]).transpose(0, 2, 1, 3)[:, :, :, None, :]
    s_s = jnp.einsum('bqghd,bgqnd->bqghn', qf, k_sel)
    p_s = masked_softmax(s_s, m_s)
    o_s = jnp.einsum('bqghn,bgqnd->bqghd', p_s.astype(v_sel.dtype), v_sel)
    wp, qp = w_pos[None, :], q_pos[:, None]
    m_w = ((wp <= qp) & (wp >= qp - WINDOW) & (wp >= 0))[None, :, None, None, :]
    s_w = jnp.einsum('bqghd,bwgd->bqghw', qf, kw)
    p_w = masked_softmax(s_w, m_w)
    o_w = jnp.einsum('bqghw,bwgd->bqghd', p_w.astype(vw.dtype), vw)
    g = jax.nn.sigmoid(gates.astype(jnp.float32)).astype(q.dtype)
    return g[..., 0:1] * o_c + g[..., 1:2] * o_s + g[..., 2:3] * o_w


def attend_prompt(q, gates, k_c, v_c, k_s, v_s, k_w, v_w, cmp_params):
    B, T = q.shape[:2]
    ctx = nsa_context(k_c, v_c, k_s, v_s, *cmp_params)
    pad = ((0, 0), (WINDOW, 0), (0, 0), (0, 0))
    kw_pad, vw_pad = jnp.pad(k_w, pad), jnp.pad(v_w, pad)
    n_qb = T // Q_BLOCK

    def blocks(t):
        return t.reshape((B, n_qb, Q_BLOCK) + t.shape[2:]).swapaxes(0, 1)

    def step(args):
        qb, gb, s = args
        q_pos = s + jnp.arange(Q_BLOCK)
        kw = lax.dynamic_slice_in_dim(kw_pad, s, WINDOW + Q_BLOCK, axis=1)
        vw = lax.dynamic_slice_in_dim(vw_pad, s, WINDOW + Q_BLOCK, axis=1)
        w_pos = s - WINDOW + jnp.arange(WINDOW + Q_BLOCK)
        return nsa_block(qb, gb, q_pos, ctx, kw, vw, w_pos)

    out = lax.map(step, (blocks(q), blocks(gates), jnp.arange(n_qb) * Q_BLOCK))
    return out.swapaxes(0, 1).reshape(B, T, NSA_W)


def gather_pages(pool, page_table):
    g = pool[page_table]
    return g.reshape(page_table.shape[0], -1, N_KV, HEAD_DIM)


def make_attend_sample(pool_k_c, pool_v_c, pool_k_s, pool_v_s, buf_k, buf_v, page_table):
    def attend(q, gates, k_c, v_c, k_s, v_s, k_w, v_w, cmp_params):
        def full(pool, new):
            return jnp.concatenate([gather_pages(pool, page_table), new], axis=1)
        ctx = nsa_context(full(pool_k_c, k_c), full(pool_v_c, v_c),
                          full(pool_k_s, k_s), full(pool_v_s, v_s), *cmp_params)
        n_new = q.shape[1]
        n_buf = buf_k.shape[1]
        q_pos = PAST_LEN + jnp.arange(n_new)
        kw = jnp.concatenate([buf_k, k_w], axis=1)
        vw = jnp.concatenate([buf_v, v_w], axis=1)
        w_pos = PAST_LEN - n_buf + jnp.arange(n_buf + n_new)
        out = nsa_block(q, gates, q_pos, ctx, kw, vw, w_pos)
        return out.reshape(q.shape[0], n_new, NSA_W)
    return attend


def mixer_layer(x, conv_buf, attend, ln_g, w_in, pe_k, w1_k, w2_k, pe_v, w1_v, w2_v,
                dw_k, dw_b, cln_g, cln_b, pw_w, pw_b, w_pa, w_pb, w_o):
    B, T, _ = x.shape
    xn = rmsnorm(x, ln_g)
    q, kv, g_nsa, g_a, glu, g_b, g_mrg = split_cols(xn @ w_in, SPLITS)
    k_c, v_c, k_s, v_s, k_w, v_w = [t.reshape(B, T, N_KV, HEAD_DIM) for t in split_cols(kv, (KV_W,) * 6)]
    q = q.reshape(B, T, N_KV, HPG, HEAD_DIM)
    g_nsa = g_nsa.reshape(B, T, N_KV, HPG, 3)
    o_a = attend(q, g_nsa, k_c, v_c, k_s, v_s, k_w, v_w, (pe_k, w1_k, w2_k, pe_v, w1_v, w2_v))
    u_val, u_gate = split_cols(glu, (CONV_C, CONV_C))
    u = u_val * jax.nn.sigmoid(u_gate)
    up = jnp.concatenate([conv_buf, u], axis=1)
    c = lax.conv_general_dilated(up, dw_k[:, None, :], (1,), 'VALID',
                                 dimension_numbers=('NWC', 'WIO', 'NWC'),
                                 feature_group_count=CONV_C) + dw_b
    c = jax.nn.silu(layernorm(c, cln_g, cln_b)) @ pw_w + pw_b
    br_a = (o_a * jax.nn.silu(g_a)) @ w_pa
    br_b = (c * jax.nn.silu(g_b)) @ w_pb
    m_a, m_b = split_cols(g_mrg, (D_MODEL, D_MODEL))
    h = jax.nn.sigmoid(m_a) * br_a + jax.nn.sigmoid(m_b) * br_b
    return x + h @ w_o, (k_c, v_c, k_s, v_s, k_w, v_w, up[:, -(CONV_W - 1):])


def setup_inputs(seed: int = 0) -> dict:
    key = jax.random.key(seed)
    ks = jax.random.split(key, 40)
    n_pages = PAST_LEN // PAGE_SIZE
    n_pool = (DEC_BATCH * n_pages * 5) // 4
    win_buf = min(WINDOW, PAST_LEN)

    def nrm(k, shape, scale):
        return jax.random.normal(k, shape, jnp.float32) * scale

    pool_shape = (DEPTH, n_pool, PAGE_SIZE, N_KV, HEAD_DIM)
    win_shape = (DEPTH, DEC_BATCH, win_buf, N_KV, HEAD_DIM)
    page_table = jax.random.permutation(ks[10], n_pool)[: DEC_BATCH * n_pages]
    page_table = page_table.reshape(DEC_BATCH, n_pages).astype(jnp.int32)
    return {
        'x_prompt': nrm(ks[0], (BATCH, SEQ, D_MODEL), 1.0),
        'x_sample': nrm(ks[1], (DEC_BATCH, DEC_SEQ, D_MODEL), 1.0),
        'cache_k_cmp': nrm(ks[2], pool_shape, 1.0),
        'cache_v_cmp': nrm(ks[3], pool_shape, 1.0),
        'cache_k_slc': nrm(ks[4], pool_shape, 1.0),
        'cache_v_slc': nrm(ks[5], pool_shape, 1.0),
        'cache_k_win': nrm(ks[6], win_shape, 1.0),
        'cache_v_win': nrm(ks[7], win_shape, 1.0),
        'state_conv': nrm(ks[8], (DEPTH, DEC_BATCH, CONV_W - 1, CONV_C), 0.5),
        'page_table': page_table,
        'ln_g': 1.0 + nrm(ks[11], (DEPTH, D_MODEL), 0.02),
        'w_in': nrm(ks[12], (DEPTH, D_MODEL, D_IN), D_MODEL ** -0.5),
        'pe_k': nrm(ks[13], (DEPTH, CMP_LEN, HEAD_DIM), 0.1),
        'w1_k': nrm(ks[14], (DEPTH, CMP_LEN * HEAD_DIM, CMP_HID), (CMP_LEN * HEAD_DIM) ** -0.5),
        'w2_k': nrm(ks[15], (DEPTH, CMP_HID, HEAD_DIM), CMP_HID ** -0.5),
        'pe_v': nrm(ks[16], (DEPTH, CMP_LEN, HEAD_DIM), 0.1),
        'w1_v': nrm(ks[17], (DEPTH, CMP_LEN * HEAD_DIM, CMP_HID), (CMP_LEN * HEAD_DIM) ** -0.5),
        'w2_v': nrm(ks[18], (DEPTH, CMP_HID, HEAD_DIM), CMP_HID ** -0.5),
        'dw_k': nrm(ks[19], (DEPTH, CONV_W, CONV_C), CONV_W ** -0.5),
        'dw_b': nrm(ks[20], (DEPTH, CONV_C), 0.02),
        'cln_g': 1.0 + nrm(ks[21], (DEPTH, CONV_C), 0.02),
        'cln_b': nrm(ks[22], (DEPTH, CONV_C), 0.02),
        'pw_w': nrm(ks[23], (DEPTH, CONV_C, CONV_C), CONV_C ** -0.5),
        'pw_b': nrm(ks[24], (DEPTH, CONV_C), 0.02),
        'w_pa': nrm(ks[25], (DEPTH, NSA_W, D_MODEL), NSA_W ** -0.5),
        'w_pb': nrm(ks[26], (DEPTH, CONV_C, D_MODEL), CONV_C ** -0.5),
        'w_o': nrm(ks[27], (DEPTH, D_MODEL, D_MODEL), D_MODEL ** -0.5),
        'final_g': 1.0 + nrm(ks[28], (D_MODEL,), 0.02),
    }


def reference(x_prompt, x_sample, cache_k_cmp, cache_v_cmp, cache_k_slc, cache_v_slc,
              cache_k_win, cache_v_win, state_conv, page_table,
              ln_g, w_in, pe_k, w1_k, w2_k, pe_v, w1_v, w2_v,
              dw_k, dw_b, cln_g, cln_b, pw_w, pw_b, w_pa, w_pb, w_o, final_g):
    xp, xs = x_prompt, x_sample
    win_p = min(WINDOW, xp.shape[1])
    win_s = cache_k_win.shape[2]
    pst = [[] for _ in range(7)]
    sst = [[] for _ in range(7)]
    for l in range(DEPTH):
        params = (ln_g[l], w_in[l], pe_k[l], w1_k[l], w2_k[l], pe_v[l], w1_v[l], w2_v[l],
                  dw_k[l], dw_b[l], cln_g[l], cln_b[l], pw_w[l], pw_b[l], w_pa[l], w_pb[l], w_o[l])
        zero_buf = jnp.zeros((xp.shape[0], CONV_W - 1, CONV_C), xp.dtype)
        xp, sp = mixer_layer(xp, zero_buf, attend_prompt, *params)
        attend_s = make_attend_sample(cache_k_cmp[l], cache_v_cmp[l], cache_k_slc[l], cache_v_slc[l],
                                      cache_k_win[l], cache_v_win[l], page_table)
        xs, ss = mixer_layer(xs, state_conv[l], attend_s, *params)
        p_rows = (sp[0], sp[1], sp[2], sp[3], sp[4][:, -win_p:], sp[5][:, -win_p:], sp[6])
        s_rows = (ss[0], ss[1], ss[2], ss[3],
                  jnp.concatenate([cache_k_win[l], ss[4]], axis=1)[:, -win_s:],
                  jnp.concatenate([cache_v_win[l], ss[5]], axis=1)[:, -win_s:], ss[6])
        for i in range(7):
            pst[i].append(p_rows[i])
            sst[i].append(s_rows[i])
    y_prompt = rmsnorm(xp, final_g)
    y_sample = rmsnorm(xs, final_g)
    p_k_cmp, p_v_cmp, p_k_slc, p_v_slc, p_k_win, p_v_win, p_conv = [jnp.stack(t, 0) for t in pst]
    s_k_cmp, s_v_cmp, s_k_slc, s_v_slc, s_k_win, s_v_win, s_conv = [jnp.stack(t, 0) for t in sst]
    return (y_prompt, y_sample, p_k_cmp, p_v_cmp, p_k_slc, p_v_slc, p_k_win, p_v_win, p_conv,
            s_k_cmp, s_v_cmp, s_k_slc, s_v_slc, s_k_win, s_v_win, s_conv)
```

```python
import functools

import jax
import jax.numpy as jnp
from jax import lax
from jax.experimental import pallas as pl
from jax.experimental.pallas import tpu as pltpu

F32 = jnp.float32
BF16 = jnp.bfloat16

D_MODEL = 1024
N_HEADS = 8
HEAD_DIM = 64
N_KV = 2
HPG = N_HEADS // N_KV
NSA_W = N_HEADS * HEAD_DIM
KV_W = N_KV * HEAD_DIM
CMP_LEN = 32
CMP_STRIDE = 16
CMP_R = CMP_LEN // CMP_STRIDE
SLC_LEN = 64
TOP_N = 16
WINDOW = 512
CONV_C = 512
CONV_W = 31
PAGE_SIZE = 128
EPS = 1e-6
FORCED_SCORE = 1e4
NEG_INF = -1e30
MASK_BIAS = -1e4

LANES = 128
SUBLANES = 8
VMEM_LIMIT = 56 * 1024 * 1024

C_Q = 0
C_KV = C_Q + N_HEADS * LANES
C_GN = C_KV + 6 * KV_W
C_GA = C_GN + LANES
C_GLU = C_GA + NSA_W
C_GB = C_GLU + 2 * CONV_C
C_MRG = C_GB + CONV_C
C_END = C_MRG + 2 * D_MODEL

TM_PROJ = 256
TQ = 256
TM_TAIL = 256
PAGES_PER_STEP = 32


def _sigmoid(x):
    return jax.nn.sigmoid(x)


def _silu(x):
    return jax.nn.silu(x)


def _dot(a, b):
    return jnp.dot(a, b, preferred_element_type=F32)


def _dot_nt(a, b):
    return lax.dot_general(a, b, (((1,), (1,)), ((), ())), preferred_element_type=F32)


def _iota(shape, dim):
    return lax.broadcasted_iota(jnp.int32, shape, dim)


def _const_spec(shape):
    nd = len(shape)
    return pl.BlockSpec(shape, lambda *_: (0,) * nd)


def _proj_kernel(x_ref, lng_ref, w_ref, qp_ref, kc_ref, vc_ref, ks_ref, vs_ref, kw_ref, vw_ref,
                 gn_ref, sga_ref, u_ref, sgb_ref, sma_ref, smb_ref, *attn_refs, tiles_per_batch):
    x = x_ref[...]
    tm = x.shape[0]
    xn = x * lax.rsqrt(jnp.mean(x * x, axis=-1, keepdims=True) + EPS) * lng_ref[...]
    xn = xn.astype(BF16)

    zq = _dot(xn, w_ref[:, C_Q:C_KV]) * (HEAD_DIM ** -0.5)
    for h in range(N_HEADS):
        qp_ref[h] = zq[:, h * LANES:(h + 1) * LANES].astype(BF16)

    zkv = _dot(xn, w_ref[:, C_KV:C_GN])
    kv = [zkv[:, i * KV_W:(i + 1) * KV_W] for i in range(6)]
    for r, v in zip((kc_ref, vc_ref, ks_ref, vs_ref, kw_ref, vw_ref), kv):
        r[...] = v

    if attn_refs:
        ksa_ref, vsp_ref, kwp_ref, vwp_ref = attn_refs
        t0 = (pl.program_id(0) % tiles_per_batch) * tm
        lane = _iota((tm, LANES), 1)
        blk = (t0 + _iota((tm, LANES), 0)) >> 6
        onehot = jnp.where((lane - HEAD_DIM) == blk, 1.0, 0.0)
        for g in range(N_KV):
            def grp(v):
                return v if g == 0 else pltpu.roll(v, HEAD_DIM, axis=1)
            ksa_ref[g] = jnp.where(lane < HEAD_DIM, grp(kv[2]), onehot).astype(BF16)
            vsp_ref[g] = grp(kv[3]).astype(BF16)
            kwp_ref[g] = grp(kv[4]).astype(BF16)
            vwp_ref[g] = grp(kv[5]).astype(BF16)

    gn_ref[...] = _dot(xn, w_ref[:, C_GN:C_GA])
    sga_ref[...] = _silu(_dot(xn, w_ref[:, C_GA:C_GLU]))
    zglu = _dot(xn, w_ref[:, C_GLU:C_GB])
    u_ref[...] = zglu[:, :CONV_C] * _sigmoid(zglu[:, CONV_C:])
    sgb_ref[...] = _silu(_dot(xn, w_ref[:, C_GB:C_MRG]))
    zm = _dot(xn, w_ref[:, C_MRG:C_END])
    sma_ref[...] = _sigmoid(zm[:, :D_MODEL])
    smb_ref[...] = _sigmoid(zm[:, D_MODEL:])


def _proj(x2d, ln_g, w_all, n_batch, t_len, attn_ops):
    m = x2d.shape[0]
    tm = min(TM_PROJ, t_len)
    tpb = t_len // tm
    grid = (m // tm,)

    def row_spec(width):
        return pl.BlockSpec((tm, width), lambda i: (i, 0))

    def bt_spec(lead, width):
        return pl.BlockSpec((None, lead, tm, width), lambda i: (i // tpb, 0, i % tpb, 0))

    out_shape = [jax.ShapeDtypeStruct((n_batch, N_HEADS, t_len, LANES), BF16)]
    out_specs = [bt_spec(N_HEADS, LANES)]
    out_shape += [jax.ShapeDtypeStruct((m, KV_W), F32)] * 6
    out_specs += [row_spec(KV_W)] * 6
    for width in (LANES, NSA_W, CONV_C, CONV_C, D_MODEL, D_MODEL):
        out_shape.append(jax.ShapeDtypeStruct((m, width), F32))
        out_specs.append(row_spec(width))
    if attn_ops:
        out_shape += [jax.ShapeDtypeStruct((n_batch, N_KV, t_len, LANES), BF16)] * 4
        out_specs += [bt_spec(N_KV, LANES)] * 4

    return pl.pallas_call(
        functools.partial(_proj_kernel, tiles_per_batch=tpb),
        grid=grid,
        in_specs=[row_spec(D_MODEL), _const_spec((1, D_MODEL)), _const_spec((D_MODEL, C_END))],
        out_specs=out_specs,
        out_shape=out_shape,
        compiler_params=pltpu.CompilerParams(dimension_semantics=("arbitrary",),
                                             vmem_limit_bytes=VMEM_LIMIT),
        name="proj",
    )(x2d, ln_g, w_all)


def _compress_from_partials(per, pe_ref, w1_ref, w2_ref):
    n = per.shape[0]
    hp = _dot(pe_ref[...], w1_ref[...])
    h0 = hp[0:1, :KV_W] + hp[1:2, KV_W:]
    h = h0 + per[:, :KV_W] + pltpu.roll(per[:, KV_W:], n - 1, axis=0)
    return _dot(_silu(h).astype(BF16), w2_ref[...])


def _chunk_rows(ref, n_chunks):
    return jnp.concatenate([ref[pl.ds(j, n_chunks, stride=CMP_STRIDE), :] for j in range(CMP_STRIDE)],
                           axis=-1)


def _cmp_prompt_kernel(kc_ref, vc_ref, pek_ref, w1k_ref, w2k_ref, pev_ref, w1v_ref, w2v_ref,
                       ok_ref, ov_ref):
    n_chunks = kc_ref.shape[0] // CMP_STRIDE
    for x_ref, pe_ref, w1_ref, w2_ref, o_ref in ((kc_ref, pek_ref, w1k_ref, w2k_ref, ok_ref),
                                                  (vc_ref, pev_ref, w1v_ref, w2v_ref, ov_ref)):
        per = _dot(_chunk_rows(x_ref, n_chunks).astype(BF16), w1_ref[...])
        c = _compress_from_partials(per, pe_ref, w1_ref, w2_ref)
        o_ref[0] = c.astype(BF16)
        o_ref[1] = pltpu.roll(c, HEAD_DIM, axis=1).astype(BF16)


def _cmp_prompt(k_c, v_c, cmp_w):
    n_batch, t_len, _ = k_c.shape
    n_chunks = t_len // CMP_STRIDE
    seq = pl.BlockSpec((None, t_len, KV_W), lambda b: (b, 0, 0))
    out = pl.BlockSpec((None, N_KV, n_chunks, KV_W), lambda b: (b, 0, 0, 0))
    w_specs = [_const_spec(w.shape) for w in cmp_w]
    return pl.pallas_call(
        _cmp_prompt_kernel,
        grid=(n_batch,),
        in_specs=[seq, seq] + w_specs,
        out_specs=[out, out],
        out_shape=[jax.ShapeDtypeStruct((n_batch, N_KV, n_chunks, KV_W), BF16)] * 2,
        compiler_params=pltpu.CompilerParams(dimension_semantics=("arbitrary",),
                                             vmem_limit_bytes=VMEM_LIMIT),
        name="cmp_prompt",
    )(k_c, v_c, *cmp_w)


def _masked_softmax(s, mask):
    s = jnp.where(mask, s, NEG_INF)
    e = jnp.where(mask, jnp.exp(s - jnp.max(s, axis=-1, keepdims=True)), 0.0)
    return e / jnp.maximum(jnp.sum(e, axis=-1, keepdims=True), 1e-30)


def _block_scores(imp, j, qpos, n_blocks):
    cur = qpos >> 6
    forced = (j == 0) | (j == cur) | (j == cur - 1)
    score = jnp.where(forced, FORCED_SCORE, imp)
    score = jnp.where(j * SLC_LEN <= qpos, score, -1.0)
    return jnp.where((j >= 0) & (j < n_blocks), score, -2.0)


def _pair_heads(o_even, o_odd):
    lane = _iota(o_even.shape, 1)
    return jnp.where(lane < HEAD_DIM, o_even, pltpu.roll(o_odd, HEAD_DIM, axis=1))


def _attn_prompt_kernel(qp_ref, gn_ref, kcm_ref, vcm_ref, ksa_ref, vsp_ref, kwp_ref, vwp_ref, agg_ref,
                        o_ref, *, t_len):
    tq = gn_ref.shape[0]
    n_blocks = t_len // SLC_LEN
    i = pl.program_id(1)
    t0 = i * tq
    rows4 = HPG * tq

    lane = _iota((tq, LANES), 1)
    qpos = t0 + _iota((tq, LANES), 0)
    qpos4 = t0 + (_iota((rows4, LANES), 0) & (tq - 1))
    lane4 = _iota((rows4, LANES), 1)
    cvalid4 = (lane4 * CMP_STRIDE + (CMP_LEN - 1)) <= qpos4
    jblk = lane - HEAD_DIM
    in_range = (jblk >= 0) & (jblk < n_blocks)
    gates = _sigmoid(gn_ref[...])

    w_start = pl.multiple_of(jnp.maximum(t0 - WINDOW, 0), tq)
    w_len = WINDOW + tq
    wpos = w_start + _iota((rows4, w_len), 1)
    qpos_w = t0 + (_iota((rows4, w_len), 0) & (tq - 1))
    wmask = (wpos <= qpos_w) & (wpos >= qpos_w - WINDOW)

    kpos_d = t0 + _iota((rows4, tq), 1)
    qpos_d = t0 + (_iota((rows4, tq), 0) & (tq - 1))
    dmask = kpos_d <= qpos_d

    for g in range(N_KV):
        q4 = qp_ref[g * HPG:(g + 1) * HPG].reshape(rows4, LANES)

        p_c = _masked_softmax(_dot_nt(q4, kcm_ref[g]), cvalid4)
        o_c = _dot(p_c.astype(BF16), vcm_ref[g])
        psum = p_c[0:tq] + p_c[tq:2 * tq] + p_c[2 * tq:3 * tq] + p_c[3 * tq:4 * tq]
        imp = _dot(psum.astype(BF16), agg_ref[...])
        score = _block_scores(imp, jblk, qpos, n_blocks)
        rank = jnp.zeros((tq, LANES), jnp.int32)
        for jp in range(n_blocks):
            col = score[:, HEAD_DIM + jp:HEAD_DIM + jp + 1]
            ahead = (col > score) | ((col == score) & (jblk > jp))
            rank = rank + jnp.where(ahead, 1, 0)
        allowed = (rank < TOP_N) & in_range & (jblk * SLC_LEN <= qpos)
        bias = jnp.where(in_range & jnp.logical_not(allowed), MASK_BIAS, 0.0).astype(BF16)
        q_sel = q4 + jnp.concatenate([bias] * HPG, axis=0)

        def chunk(c, carry, causal):
            m, l, acc = carry
            off = pl.multiple_of(c * tq, tq)
            s = _dot_nt(q_sel, ksa_ref[g, pl.ds(off, tq), :])
            if causal:
                s = jnp.where(dmask, s, NEG_INF)
            m_new = jnp.maximum(m, jnp.max(s, axis=-1, keepdims=True))
            alpha = jnp.exp(m - m_new)
            p = jnp.exp(s - m_new)
            l = alpha * l + jnp.sum(p, axis=-1, keepdims=True)
            acc = alpha * acc + _dot(p.astype(BF16), vsp_ref[g, pl.ds(off, tq), :])
            return m_new, l, acc

        init = (jnp.full((rows4, 1), NEG_INF, F32), jnp.zeros((rows4, 1), F32),
                jnp.zeros((rows4, LANES), F32))
        carry = lax.fori_loop(0, i, lambda c, cr: chunk(c, cr, False), init)
        _, l_s, acc_s = chunk(i, carry, True)
        o_s = acc_s / l_s

        p_w = _masked_softmax(_dot_nt(q4, kwp_ref[g, pl.ds(w_start, w_len), :]), wmask)
        o_w = _dot(p_w.astype(BF16), vwp_ref[g, pl.ds(w_start, w_len), :])

        heads = []
        for h in range(HPG):
            col = (g * HPG + h) * 3
            rs = slice(h * tq, (h + 1) * tq)
            heads.append(gates[:, col:col + 1] * o_c[rs] + gates[:, col + 1:col + 2] * o_s[rs]
                         + gates[:, col + 2:col + 3] * o_w[rs])
        for pr in range(HPG // 2):
            c0 = (g * HPG + 2 * pr) * HEAD_DIM
            o_ref[:, c0:c0 + LANES] = _pair_heads(heads[2 * pr], heads[2 * pr + 1])


def _attn_prompt(qp, gn, kcm, vcm, ksa, vsp, kwp, vwp, agg):
    n_batch, _, t_len, _ = qp.shape
    tq = TQ
    n_cmp = kcm.shape[2]
    full = pl.BlockSpec((None, N_KV, t_len, LANES), lambda b, i: (b, 0, 0, 0))
    cmp_spec = pl.BlockSpec((None, N_KV, n_cmp, KV_W), lambda b, i: (b, 0, 0, 0))
    return pl.pallas_call(
        functools.partial(_attn_prompt_kernel, t_len=t_len),
        grid=(n_batch, t_len // tq),
        in_specs=[pl.BlockSpec((None, N_HEADS, tq, LANES), lambda b, i: (b, 0, i, 0)),
                  pl.BlockSpec((None, tq, LANES), lambda b, i: (b, i, 0)),
                  cmp_spec, cmp_spec, full, full, full, full, _const_spec(agg.shape)],
        out_specs=pl.BlockSpec((None, tq, NSA_W), lambda b, i: (b, i, 0)),
        out_shape=jax.ShapeDtypeStruct((n_batch, t_len, NSA_W), F32),
        compiler_params=pltpu.CompilerParams(dimension_semantics=("arbitrary", "arbitrary"),
                                             vmem_limit_bytes=VMEM_LIMIT),
        name="attn_prompt",
    )(qp, gn, kcm, vcm, ksa, vsp, kwp, vwp, agg)


def _tail_math(x, o_a, sga, c_pre, sgb, sma, smb, w):
    (dwb_ref, clg_ref, clb_ref, pww_ref, pwb_ref, wpa_ref, wpb_ref, wo_ref, fg_ref) = w
    c = c_pre + dwb_ref[...]
    mu = jnp.mean(c, axis=-1, keepdims=True)
    var = jnp.mean(jnp.square(c - mu), axis=-1, keepdims=True)
    cn = (c - mu) * lax.rsqrt(var + EPS) * clg_ref[...] + clb_ref[...]
    cp = _dot(_silu(cn).astype(BF16), pww_ref[...]) + pwb_ref[...]
    br_a = _dot((o_a * sga).astype(BF16), wpa_ref[...])
    br_b = _dot((cp * sgb).astype(BF16), wpb_ref[...])
    h = sma * br_a + smb * br_b
    xo = x + _dot(h.astype(BF16), wo_ref[...])
    return xo * lax.rsqrt(jnp.mean(xo * xo, axis=-1, keepdims=True) + EPS) * fg_ref[...]


HIST = 32


def _tail_prompt_kernel(x_ref, oa_ref, sga_ref, u_ref, sgb_ref, sma_ref, smb_ref, dw_ref, *rest):
    w, (y_ref, up_ref) = rest[:9], rest[9:]
    tm = x_ref.shape[0]

    @pl.when(pl.program_id(1) == 0)
    def _():
        up_ref[0:HIST, :] = jnp.zeros((HIST, CONV_C), F32)

    up_ref[HIST:HIST + tm, :] = u_ref[...]
    acc = jnp.zeros((tm, CONV_C), F32)
    for k in range(CONV_W):
        acc = acc + up_ref[pl.ds(HIST - (CONV_W - 1) + k, tm), :] * dw_ref[k:k + 1, :]
    up_ref[0:HIST, :] = up_ref[tm:tm + HIST, :]
    y_ref[...] = _tail_math(x_ref[...], oa_ref[...], sga_ref[...], acc, sgb_ref[...], sma_ref[...],
                            smb_ref[...], w)


def _tail_prompt(x, o_a, sga, u, sgb, sma, smb, dw, tail_w):
    n_batch, t_len, _ = x.shape
    tm = TM_TAIL

    def bt(width):
        return pl.BlockSpec((None, tm, width), lambda b, t: (b, t, 0))

    ins = [x, o_a, sga, u, sgb, sma, smb]
    return pl.pallas_call(
        _tail_prompt_kernel,
        grid=(n_batch, t_len // tm),
        in_specs=[bt(a.shape[-1]) for a in ins] + [_const_spec(dw.shape)] + [_const_spec(a.shape) for a in tail_w],
        out_specs=bt(D_MODEL),
        out_shape=jax.ShapeDtypeStruct((n_batch, t_len, D_MODEL), F32),
        scratch_shapes=[pltpu.VMEM((HIST + tm, CONV_C), F32)],
        compiler_params=pltpu.CompilerParams(dimension_semantics=("arbitrary", "arbitrary"),
                                             vmem_limit_bytes=VMEM_LIMIT),
        name="tail_prompt",
    )(*ins, dw, *tail_w)


S_ROWS = 8
UP_ROWS = 40


def _tail_sample_kernel(x_ref, oa_ref, sga_ref, ups_ref, sgb_ref, sma_ref, smb_ref, dws_ref, *rest,
                        n_new):
    w, (y_ref, c_ref) = rest[:9], rest[9:]
    n_batch = ups_ref.shape[0]
    c_ref[...] = jnp.zeros(c_ref.shape, F32)

    def body(b, carry):
        up = ups_ref[b]
        for t in range(n_new):
            c_ref[pl.ds(b * S_ROWS + t, 1), :] = jnp.sum(up * dws_ref[t], axis=0, keepdims=True)
        return carry

    lax.fori_loop(0, n_batch, body, 0)
    y_ref[...] = _tail_math(x_ref[...], oa_ref[...], sga_ref[...], c_ref[...], sgb_ref[...], sma_ref[...],
                            smb_ref[...], w)


def _tail_sample(x, o_a, sga, up_s, sgb, sma, smb, dw_shift, tail_w, n_new):
    ins = [x, o_a, sga, up_s, sgb, sma, smb, dw_shift] + list(tail_w)
    m = x.shape[0]
    return pl.pallas_call(
        functools.partial(_tail_sample_kernel, n_new=n_new),
        grid=(1,),
        in_specs=[_const_spec(a.shape) for a in ins],
        out_specs=_const_spec((m, D_MODEL)),
        out_shape=jax.ShapeDtypeStruct((m, D_MODEL), F32),
        scratch_shapes=[pltpu.VMEM((m, CONV_C), F32)],
        compiler_params=pltpu.CompilerParams(dimension_semantics=("arbitrary",),
                                             vmem_limit_bytes=VMEM_LIMIT),
        name="tail_sample",
    )(*ins)


def _group_q(q_ref, g):
    q = q_ref[g * HPG:(g + 1) * HPG].reshape(HPG * S_ROWS, LANES).astype(F32)
    return q if g == 0 else pltpu.roll(q, HEAD_DIM, axis=1)


def _group_out(o, g):
    return o if g == 0 else pltpu.roll(o, HEAD_DIM, axis=1)


def _scmp_kernel(pt_ref, *refs, n_pages_step, n_steps, past_len):
    p = n_pages_step
    kpages, vpages = refs[:p], refs[p:2 * p]
    (q_ref, pek_ref, w1k_ref, w2k_ref, pev_ref, w1v_ref, w2v_ref, agg_ref,
     oc_ref, idx_ref, ak_ref, av_ref) = refs[2 * p:]
    pg = pl.program_id(1)
    chunks_page = PAGE_SIZE // CMP_STRIDE
    rows = chunks_page * p

    for pages, w1_ref, a_ref in ((kpages, w1k_ref, ak_ref), (vpages, w1v_ref, av_ref)):
        cols = [jnp.concatenate([pr[pl.ds(j, chunks_page, stride=CMP_STRIDE), :] for pr in pages], axis=0)
                for j in range(CMP_STRIDE)]
        xc = jnp.concatenate(cols, axis=-1).astype(BF16)
        a_ref[pl.ds(pl.multiple_of(pg * rows, rows), rows), :] = _dot(xc, w1_ref[...])

    @pl.when(pg == n_steps - 1)
    def _():
        kc = _compress_from_partials(ak_ref[...], pek_ref, w1k_ref, w2k_ref).astype(BF16)
        vc = _compress_from_partials(av_ref[...], pev_ref, w1v_ref, w2v_ref).astype(BF16)
        n_chunks = kc.shape[0]
        n_blocks = past_len // SLC_LEN + 1
        r32 = HPG * S_ROWS
        qpos32 = past_len + (_iota((r32, n_chunks), 0) & (S_ROWS - 1))
        cvalid = (_iota((r32, n_chunks), 1) * CMP_STRIDE + (CMP_LEN - 1)) <= qpos32
        nb_pad = agg_ref.shape[1]
        jl = _iota((S_ROWS, nb_pad), 1)
        jlf = jl.astype(F32)
        qpos8 = past_len + _iota((S_ROWS, nb_pad), 0)
        lane8 = _iota((S_ROWS, LANES), 1)
        for g in range(N_KV):
            qg = _group_q(q_ref, g).astype(BF16)
            p_c = _masked_softmax(_dot_nt(qg, kc), cvalid)
            oc_ref[g] = _group_out(_dot(p_c.astype(BF16), vc), g)
            psum = (p_c[0:S_ROWS] + p_c[S_ROWS:2 * S_ROWS] + p_c[2 * S_ROWS:3 * S_ROWS]
                    + p_c[3 * S_ROWS:4 * S_ROWS])
            imp = _dot(psum.astype(BF16), agg_ref[...])
            score = _block_scores(imp, jl, qpos8, n_blocks)
            idx = jnp.zeros((S_ROWS, LANES), F32)
            for k in range(TOP_N):
                mx = jnp.max(score, axis=-1, keepdims=True)
                am = jnp.min(jnp.where(score == mx, jlf, float(nb_pad)), axis=-1, keepdims=True)
                idx = jnp.where(lane8 == k, am, idx)
                score = jnp.where(jlf == am, -3.0, score)
            idx_ref[g] = idx.astype(jnp.int32)


def _scmp(page_table, kpool, vpool, qp_s, cmp_w, agg_s, past_len):
    n_batch, n_pages = page_table.shape
    p = PAGES_PER_STEP
    n_steps = n_pages // p
    n_chunks = n_pages * (PAGE_SIZE // CMP_STRIDE)

    def page_spec(i):
        return pl.BlockSpec((None, PAGE_SIZE, KV_W), lambda b, s, pt, i=i: (pt[b, s * p + i], 0, 0))

    def cspec(shape):
        nd = len(shape)
        return pl.BlockSpec(shape, lambda b, s, pt: (0,) * nd)

    page_specs = [page_spec(i) for i in range(p)]
    in_specs = (page_specs + page_specs
                + [pl.BlockSpec((None, N_HEADS, S_ROWS, LANES), lambda b, s, pt: (0, 0, b, 0))]
                + [cspec(w.shape) for w in cmp_w] + [cspec(agg_s.shape)])
    out_specs = [pl.BlockSpec((None, N_KV, HPG * S_ROWS, LANES), lambda b, s, pt: (b, 0, 0, 0)),
                 pl.BlockSpec((None, N_KV, S_ROWS, LANES), lambda b, s, pt: (b, 0, 0, 0))]
    out_shape = [jax.ShapeDtypeStruct((n_batch, N_KV, HPG * S_ROWS, LANES), F32),
                 jax.ShapeDtypeStruct((n_batch, N_KV, S_ROWS, LANES), jnp.int32)]
    return pl.pallas_call(
        functools.partial(_scmp_kernel, n_pages_step=p, n_steps=n_steps, past_len=past_len),
        grid_spec=pltpu.PrefetchScalarGridSpec(
            num_scalar_prefetch=1, grid=(n_batch, n_steps), in_specs=in_specs, out_specs=out_specs,
            scratch_shapes=[pltpu.VMEM((n_chunks, 2 * KV_W), F32)] * 2),
        out_shape=out_shape,
        compiler_params=pltpu.CompilerParams(dimension_semantics=("arbitrary", "arbitrary"),
                                             vmem_limit_bytes=VMEM_LIMIT),
        name="sample_cmp",
    )(page_table, *([kpool] * p), *([vpool] * p), qp_s, *cmp_w, agg_s)


def _ssel_kernel(pt_ref, idx_ref, kpool_ref, vpool_ref, q_ref, ksn_ref, vsn_ref, kwn_ref, vwn_ref,
                 kwb_ref, vwb_ref, gn_ref, oc_ref, o_ref, kbuf_ref, vbuf_ref, sem_ref,
                 *, n_new, past_len):
    b = pl.program_id(0)
    n_pool_blocks = past_len // SLC_LEN
    blocks_page = PAGE_SIZE // SLC_LEN

    def block_copy(pool_ref, buf_ref, sem, g, qi, k):
        j = jnp.minimum(idx_ref[b, (g * n_new + qi) * TOP_N + k], n_pool_blocks - 1)
        page = pt_ref[b, j // blocks_page]
        row0 = pl.multiple_of((j % blocks_page) * SLC_LEN, SLC_LEN)
        return pltpu.make_async_copy(pool_ref.at[page, pl.ds(row0, SLC_LEN), :],
                                     buf_ref.at[g * n_new + qi, pl.ds(k * SLC_LEN, SLC_LEN), :], sem)

    slots = [(g, qi, k) for g in range(N_KV) for qi in range(n_new) for k in range(TOP_N)]
    for g, qi, k in slots:
        block_copy(kpool_ref, kbuf_ref, sem_ref.at[0], g, qi, k).start()
        block_copy(vpool_ref, vbuf_ref, sem_ref.at[1], g, qi, k).start()

    r32 = HPG * S_ROWS
    qi32 = _iota((r32, LANES), 0) & (S_ROWS - 1)
    n_sel = TOP_N * SLC_LEN
    slot_id = _iota((r32, n_sel), 1) >> 6
    new_rows = LANES
    newcol = _iota((r32, new_rows), 1)
    qrow_new = _iota((r32, new_rows), 0) & (S_ROWS - 1)

    def pad_new(ref):
        return jnp.concatenate([ref[...], jnp.zeros((new_rows - S_ROWS, KV_W), F32)], axis=0).astype(BF16)

    ks_new, vs_new, kw_new, vw_new = pad_new(ksn_ref), pad_new(vsn_ref), pad_new(kwn_ref), pad_new(vwn_ref)

    w_buf = kwb_ref.shape[0]
    wpos = jnp.concatenate([past_len - w_buf + _iota((r32, w_buf), 1), past_len + newcol], axis=1)
    qpos_w = past_len + (_iota((r32, w_buf + new_rows), 0) & (S_ROWS - 1))
    wmask = (wpos <= qpos_w) & (wpos >= qpos_w - WINDOW)
    wmask = wmask & jnp.concatenate([jnp.full((r32, w_buf), True), newcol < n_new], axis=1)
    kw_all = jnp.concatenate([kwb_ref[...].astype(BF16), kw_new], axis=0)
    vw_all = jnp.concatenate([vwb_ref[...].astype(BF16), vw_new], axis=0)
    gates = _sigmoid(gn_ref[...])

    o_w = []
    q_g = []
    for g in range(N_KV):
        qg = _group_q(q_ref, g).astype(BF16)
        q_g.append(qg)
        p_w = _masked_softmax(_dot_nt(qg, kw_all), wmask)
        o_w.append(_group_out(_dot(p_w.astype(BF16), vw_all), g))

    for g, qi, k in slots:
        block_copy(kpool_ref, kbuf_ref, sem_ref.at[0], g, qi, k).wait()
        block_copy(vpool_ref, vbuf_ref, sem_ref.at[1], g, qi, k).wait()

    for g in range(N_KV):
        o_s = jnp.zeros((r32, LANES), F32)
        for qi in range(n_new):
            smask = jnp.full((r32, n_sel), False)
            n_hit = jnp.int32(0)
            for k in range(TOP_N):
                j = idx_ref[b, (g * n_new + qi) * TOP_N + k]
                smask = smask | (slot_id == jnp.where(j < n_pool_blocks, k, -1))
                n_hit = n_hit + jnp.where(j == n_pool_blocks, 1, 0)
            nmask = (newcol <= qrow_new) & (newcol < jnp.where(n_hit > 0, n_new, 0))
            kk = jnp.concatenate([kbuf_ref[g * n_new + qi].astype(BF16), ks_new], axis=0)
            vv = jnp.concatenate([vbuf_ref[g * n_new + qi].astype(BF16), vs_new], axis=0)
            p_s = _masked_softmax(_dot_nt(q_g[g], kk), jnp.concatenate([smask, nmask], axis=1))
            o_qi = _group_out(_dot(p_s.astype(BF16), vv), g)
            o_s = jnp.where(qi32 == qi, o_qi, o_s)
        o_c = oc_ref[g]
        heads = []
        for h in range(HPG):
            col = (g * HPG + h) * 3
            rs = slice(h * S_ROWS, (h + 1) * S_ROWS)
            heads.append(gates[:, col:col + 1] * o_c[rs] + gates[:, col + 1:col + 2] * o_s[rs]
                         + gates[:, col + 2:col + 3] * o_w[g][rs])
        for pr in range(HPG // 2):
            c0 = (g * HPG + 2 * pr) * HEAD_DIM
            o_ref[:, c0:c0 + LANES] = _pair_heads(heads[2 * pr], heads[2 * pr + 1])


def _ssel(page_table, idx, kpool, vpool, qp_s, ks_n, vs_n, kw_n, vw_n, kw_buf, vw_buf, gn_s, oc,
          n_new, past_len):
    n_batch = page_table.shape[0]
    w_buf = kw_buf.shape[1]
    rows8 = pl.BlockSpec((S_ROWS, KV_W), lambda b, pt, ix: (b, 0))
    wbuf_spec = pl.BlockSpec((None, w_buf, KV_W), lambda b, pt, ix: (b, 0, 0))
    in_specs = [pl.BlockSpec(memory_space=pl.ANY), pl.BlockSpec(memory_space=pl.ANY),
                pl.BlockSpec((None, N_HEADS, S_ROWS, LANES), lambda b, pt, ix: (0, 0, b, 0)),
                rows8, rows8, rows8, rows8, wbuf_spec, wbuf_spec,
                pl.BlockSpec((S_ROWS, LANES), lambda b, pt, ix: (b, 0)),
                pl.BlockSpec((None, N_KV, HPG * S_ROWS, LANES), lambda b, pt, ix: (b, 0, 0, 0))]
    return pl.pallas_call(
        functools.partial(_ssel_kernel, n_new=n_new, past_len=past_len),
        grid_spec=pltpu.PrefetchScalarGridSpec(
            num_scalar_prefetch=2, grid=(n_batch,), in_specs=in_specs,
            out_specs=pl.BlockSpec((S_ROWS, NSA_W), lambda b, pt, ix: (b, 0)),
            scratch_shapes=[pltpu.VMEM((N_KV * n_new, TOP_N * SLC_LEN, KV_W), F32)] * 2
            + [pltpu.SemaphoreType.DMA((2,))]),
        out_shape=jax.ShapeDtypeStruct((n_batch * S_ROWS, NSA_W), F32),
        compiler_params=pltpu.CompilerParams(dimension_semantics=("arbitrary",),
                                             vmem_limit_bytes=VMEM_LIMIT),
        name="sample_sel",
    )(page_table, idx, kpool, vpool, qp_s, ks_n, vs_n, kw_n, vw_n, kw_buf, vw_buf, gn_s, oc)


def _prep_w_in(w_in):
    q = w_in[:, :NSA_W].reshape(D_MODEL, N_HEADS, HEAD_DIM)
    q = jnp.pad(q, ((0, 0), (0, 0), (0, LANES - HEAD_DIM))).reshape(D_MODEL, N_HEADS * LANES)
    c = NSA_W
    kv = w_in[:, c:c + 6 * KV_W]
    c += 6 * KV_W
    gn = jnp.pad(w_in[:, c:c + 3 * N_HEADS], ((0, 0), (0, LANES - 3 * N_HEADS)))
    c += 3 * N_HEADS
    rest = w_in[:, c:]
    return jnp.concatenate([q, kv, gn, rest], axis=1).astype(BF16)


def _prep_cmp(pe, w1, w2):
    w1r = w1.reshape(CMP_R, CMP_STRIDE, HEAD_DIM, HEAD_DIM)
    eye = jnp.eye(N_KV, dtype=w1.dtype)
    w1big = jnp.einsum('rjdh,ge->jgdreh', w1r, eye).reshape(CMP_STRIDE * KV_W, CMP_R * KV_W)
    pe_rows = jnp.broadcast_to(pe.reshape(CMP_R, CMP_STRIDE, 1, HEAD_DIM),
                               (CMP_R, CMP_STRIDE, N_KV, HEAD_DIM)).reshape(CMP_R, CMP_STRIDE * KV_W)
    pe_rows = jnp.pad(pe_rows, ((0, SUBLANES - CMP_R), (0, 0)))
    w2big = jnp.einsum('hd,ge->ghed', w2, eye).reshape(KV_W, KV_W)
    return pe_rows.astype(BF16), w1big.astype(BF16), w2big.astype(BF16)


def _agg_matrix(n_rows, n_blocks, n_cols, col0):
    i = jnp.arange(n_rows)[:, None] * CMP_STRIDE
    jj = jnp.arange(n_cols)[None, :] - col0
    hit = (i < jj * SLC_LEN + SLC_LEN) & (i + CMP_LEN > jj * SLC_LEN) & (jj >= 0) & (jj < n_blocks)
    return hit.astype(BF16)


def kernel(x_prompt, x_sample, cache_k_cmp, cache_v_cmp, cache_k_slc, cache_v_slc, cache_k_win, cache_v_win, state_conv, page_table, ln_g, w_in, pe_k, w1_k, w2_k, pe_v, w1_v, w2_v, dw_k, dw_b, cln_g, cln_b, pw_w, pw_b, w_pa, w_pb, w_o, final_g):
    assert w_in.shape[0] == 1, "single-layer stack"
    n_b, t_len, _ = x_prompt.shape
    n_db, n_new, _ = x_sample.shape
    n_pages = page_table.shape[1]
    past_len = n_pages * PAGE_SIZE
    w_buf = cache_k_win.shape[2]
    assert n_new <= S_ROWS and t_len % TQ == 0 and t_len >= WINDOW + TQ
    assert t_len // SLC_LEN <= LANES - HEAD_DIM and past_len % PAGE_SIZE == 0

    w_all = _prep_w_in(w_in[0])
    lng = ln_g[0].reshape(1, D_MODEL)
    cmp_w = _prep_cmp(pe_k[0], w1_k[0], w2_k[0]) + _prep_cmp(pe_v[0], w1_v[0], w2_v[0])
    tail_w = (dw_b[0].reshape(1, CONV_C), cln_g[0].reshape(1, CONV_C), cln_b[0].reshape(1, CONV_C),
              pw_w[0].astype(BF16), pw_b[0].reshape(1, CONV_C), w_pa[0].astype(BF16), w_pb[0].astype(BF16),
              w_o[0].astype(BF16), final_g.reshape(1, D_MODEL))
    dw = jnp.pad(dw_k[0], ((0, 1), (0, 0)))

    (qp, k_c, v_c, k_s, v_s, k_w, v_w, gn, sga, u, sgb, sma, smb, ksa, vsp, kwp, vwp) = _proj(
        x_prompt.reshape(n_b * t_len, D_MODEL), lng, w_all, n_b, t_len, True)
    seq = lambda a: a.reshape(n_b, t_len, a.shape[-1])
    kcm, vcm = _cmp_prompt(seq(k_c), seq(v_c), cmp_w)
    n_cmp_rows = t_len // CMP_STRIDE
    agg_p = _agg_matrix(n_cmp_rows, t_len // SLC_LEN, LANES, HEAD_DIM)
    o_a = _attn_prompt(qp, seq(gn), kcm, vcm, ksa, vsp, kwp, vwp, agg_p)
    y_prompt = _tail_prompt(x_prompt, o_a, seq(sga), seq(u), seq(sgb), seq(sma), seq(smb), dw, tail_w)

    xs = jnp.pad(x_sample, ((0, 0), (0, S_ROWS - n_new), (0, 0))).reshape(n_db * S_ROWS, D_MODEL)
    (qp_s, k_c_s, v_c_s, k_s_s, v_s_s, k_w_s, v_w_s, gn_s, sga_s, u_s, sgb_s, sma_s, smb_s) = _proj(
        xs, lng, w_all, 1, n_db * S_ROWS, False)
    pool = lambda c: c[0].reshape(c.shape[1], PAGE_SIZE, KV_W)
    n_blocks_s = past_len // SLC_LEN + 1
    nb_pad = -(-n_blocks_s // LANES) * LANES
    agg_s = _agg_matrix(past_len // CMP_STRIDE, n_blocks_s, nb_pad, 0)
    agg_s = agg_s.at[past_len // CMP_STRIDE - 1].set(0)
    oc_s, idx_s = _scmp(page_table, pool(cache_k_cmp), pool(cache_v_cmp), qp_s, cmp_w, agg_s, past_len)
    idx_flat = idx_s[:, :, :n_new, :TOP_N].reshape(n_db, N_KV * n_new * TOP_N)
    win = lambda c: c[0].reshape(n_db, w_buf, KV_W)
    o_a_s = _ssel(page_table, idx_flat, pool(cache_k_slc), pool(cache_v_slc), qp_s, k_s_s, v_s_s, k_w_s,
                  v_w_s, win(cache_k_win), win(cache_v_win), gn_s, oc_s, n_new, past_len)
    u_new = u_s.reshape(n_db, S_ROWS, CONV_C)[:, :n_new]
    up_s = jnp.concatenate([state_conv[0], u_new], axis=1)
    up_pad = jnp.pad(up_s, ((0, 0), (0, UP_ROWS - up_s.shape[1]), (0, 0)))
    dw_shift = jnp.stack([jnp.pad(dw_k[0], ((t, UP_ROWS - CONV_W - t), (0, 0))) for t in range(n_new)])
    y_s = _tail_sample(xs, o_a_s, sga_s, up_pad, sgb_s, sma_s, smb_s, dw_shift, tail_w, n_new)
    y_sample = y_s.reshape(n_db, S_ROWS, D_MODEL)[:, :n_new]

    def p_rows(a):
        return a.reshape(1, n_b, t_len, N_KV, HEAD_DIM)

    def s_rows(a):
        return a.reshape(n_db, S_ROWS, N_KV, HEAD_DIM)[:, :n_new][None]

    win_p = min(WINDOW, t_len)
    p_conv = seq(u)[:, -(CONV_W - 1):][None]
    s_k_win = jnp.concatenate([cache_k_win[0], s_rows(k_w_s)[0]], axis=1)[:, -w_buf:][None]
    s_v_win = jnp.concatenate([cache_v_win[0], s_rows(v_w_s)[0]], axis=1)[:, -w_buf:][None]
    s_conv = up_s[:, -(CONV_W - 1):][None]
    return (y_prompt, y_sample, p_rows(k_c), p_rows(v_c), p_rows(k_s), p_rows(v_s),
            p_rows(k_w)[:, :, -win_p:], p_rows(v_w)[:, :, -win_p:], p_conv,
            s_rows(k_c_s), s_rows(v_c_s), s_rows(k_s_s), s_rows(v_s_s), s_k_win, s_v_win, s_conv)
```

```python
import functools

import jax
import jax.numpy as jnp
from jax import lax
from jax.experimental import pallas as pl
from jax.experimental.pallas import tpu as pltpu

F32 = jnp.float32
BF16 = jnp.bfloat16

D_MODEL = 1024
N_HEADS = 8
HEAD_DIM = 64
N_KV = 2
HPG = N_HEADS // N_KV
NSA_W = N_HEADS * HEAD_DIM
KV_W = N_KV * HEAD_DIM
CMP_LEN = 32
CMP_STRIDE = 16
CMP_R = CMP_LEN // CMP_STRIDE
SLC_LEN = 64
TOP_N = 16
WINDOW = 512
CONV_C = 512
CONV_W = 31
PAGE_SIZE = 128
EPS = 1e-6
FORCED_SCORE = 1e4
NEG_INF = -1e30
MASK_BIAS = -1e4

LANES = 128
SUBLANES = 8
VMEM_LIMIT = 56 * 1024 * 1024

C_Q = 0
C_KV = C_Q + N_HEADS * LANES
C_GN = C_KV + 6 * KV_W
C_GA = C_GN + LANES
C_GLU = C_GA + NSA_W
C_GB = C_GLU + 2 * CONV_C
C_MRG = C_GB + CONV_C
C_END = C_MRG + 2 * D_MODEL

TM_PROJ = 256
TQ = 256
TM_TAIL = 256
PAGES_PER_STEP = 32


def _sigmoid(x):
    return jax.nn.sigmoid(x)


def _silu(x):
    return jax.nn.silu(x)


def _dot(a, b):
    return jnp.dot(a, b, preferred_element_type=F32)


def _dot_nt(a, b):
    return lax.dot_general(a, b, (((1,), (1,)), ((), ())), preferred_element_type=F32)


def _iota(shape, dim):
    return lax.broadcasted_iota(jnp.int32, shape, dim)


def _const_spec(shape):
    nd = len(shape)
    return pl.BlockSpec(shape, lambda *_: (0,) * nd)


def _proj_kernel(x_ref, lng_ref, w_ref, qp_ref, kct_ref, vct_ref, kst_ref, vst_ref, kwt_ref, vwt_ref,
                 gn_ref, sga_ref, u_ref, sgb_ref, sma_ref, smb_ref, *more_refs, tiles_per_batch, row_ids,
                 attn_ops):
    x = x_ref[...]
    tm = x.shape[0]
    xn = x * lax.rsqrt(jnp.mean(x * x, axis=-1, keepdims=True) + EPS) * lng_ref[...]
    xn = xn.astype(BF16)

    zq = _dot(xn, w_ref[:, C_Q:C_KV]) * (HEAD_DIM ** -0.5)
    for h in range(N_HEADS):
        qp_ref[h] = zq[:, h * LANES:(h + 1) * LANES].astype(BF16)

    zkv = _dot(xn, w_ref[:, C_KV:C_GN])
    kv = [zkv[:, i * KV_W:(i + 1) * KV_W] for i in range(6)]
    for r, v in zip((kct_ref, vct_ref, kst_ref, vst_ref, kwt_ref, vwt_ref), kv):
        r[...] = v.T
    for r, i in zip(more_refs[:len(row_ids)], row_ids):
        r[...] = kv[i]

    if attn_ops:
        ksa_ref, vsp_ref, kwp_ref, vwp_ref = more_refs[len(row_ids):]
        t0 = (pl.program_id(0) % tiles_per_batch) * tm
        lane = _iota((tm, LANES), 1)
        blk = (t0 + _iota((tm, LANES), 0)) >> 6
        onehot = jnp.where((lane - HEAD_DIM) == blk, 1.0, 0.0)
        for g in range(N_KV):
            def grp(v):
                return v if g == 0 else pltpu.roll(v, HEAD_DIM, axis=1)
            ksa_ref[g] = jnp.where(lane < HEAD_DIM, grp(kv[2]), onehot).astype(BF16)
            vsp_ref[g] = grp(kv[3]).astype(BF16)
            kwp_ref[g] = grp(kv[4]).astype(BF16)
            vwp_ref[g] = grp(kv[5]).astype(BF16)

    gn_ref[...] = _dot(xn, w_ref[:, C_GN:C_GA])
    sga_ref[...] = _silu(_dot(xn, w_ref[:, C_GA:C_GLU]))
    zglu = _dot(xn, w_ref[:, C_GLU:C_GB])
    u_ref[...] = zglu[:, :CONV_C] * _sigmoid(zglu[:, CONV_C:])
    sgb_ref[...] = _silu(_dot(xn, w_ref[:, C_GB:C_MRG]))
    zm = _dot(xn, w_ref[:, C_MRG:C_END])
    sma_ref[...] = _sigmoid(zm[:, :D_MODEL])
    smb_ref[...] = _sigmoid(zm[:, D_MODEL:])


def _proj(x2d, ln_g, w_all, n_batch, t_len, row_ids, attn_ops):
    m = x2d.shape[0]
    tm = min(TM_PROJ, t_len)
    tpb = t_len // tm
    grid = (m // tm,)

    def row_spec(width):
        return pl.BlockSpec((tm, width), lambda i: (i, 0))

    def bt_spec(lead, width):
        return pl.BlockSpec((None, lead, tm, width), lambda i: (i // tpb, 0, i % tpb, 0))

    out_shape = [jax.ShapeDtypeStruct((n_batch, N_HEADS, t_len, LANES), BF16)]
    out_specs = [bt_spec(N_HEADS, LANES)]
    out_shape += [jax.ShapeDtypeStruct((n_batch, KV_W, t_len), F32)] * 6
    out_specs += [pl.BlockSpec((None, KV_W, tm), lambda i: (i // tpb, 0, i % tpb))] * 6
    for width in (LANES, NSA_W, CONV_C, CONV_C, D_MODEL, D_MODEL):
        out_shape.append(jax.ShapeDtypeStruct((m, width), F32))
        out_specs.append(row_spec(width))
    out_shape += [jax.ShapeDtypeStruct((m, KV_W), F32)] * len(row_ids)
    out_specs += [row_spec(KV_W)] * len(row_ids)
    if attn_ops:
        out_shape += [jax.ShapeDtypeStruct((n_batch, N_KV, t_len, LANES), BF16)] * 4
        out_specs += [bt_spec(N_KV, LANES)] * 4

    return pl.pallas_call(
        functools.partial(_proj_kernel, tiles_per_batch=tpb, row_ids=row_ids, attn_ops=attn_ops),
        grid=grid,
        in_specs=[row_spec(D_MODEL), _const_spec((1, D_MODEL)), _const_spec((D_MODEL, C_END))],
        out_specs=out_specs,
        out_shape=out_shape,
        compiler_params=pltpu.CompilerParams(dimension_semantics=("arbitrary",),
                                             vmem_limit_bytes=VMEM_LIMIT),
        name="proj",
    )(x2d, ln_g, w_all)


def _compress_from_partials(per, pe_ref, w1_ref, w2_ref):
    n = per.shape[0]
    hp = _dot(pe_ref[...], w1_ref[...])
    h0 = hp[0:1, :KV_W] + hp[1:2, KV_W:]
    h = h0 + per[:, :KV_W] + pltpu.roll(per[:, KV_W:], n - 1, axis=0)
    return _dot(_silu(h).astype(BF16), w2_ref[...])


def _chunk_rows(ref, n_chunks):
    return jnp.concatenate([ref[pl.ds(j, n_chunks, stride=CMP_STRIDE), :] for j in range(CMP_STRIDE)],
                           axis=-1)


def _cmp_prompt_kernel(kc_ref, vc_ref, pek_ref, w1k_ref, w2k_ref, pev_ref, w1v_ref, w2v_ref,
                       ok_ref, ov_ref):
    n_chunks = kc_ref.shape[0] // CMP_STRIDE
    for x_ref, pe_ref, w1_ref, w2_ref, o_ref in ((kc_ref, pek_ref, w1k_ref, w2k_ref, ok_ref),
                                                  (vc_ref, pev_ref, w1v_ref, w2v_ref, ov_ref)):
        per = _dot(_chunk_rows(x_ref, n_chunks).astype(BF16), w1_ref[...])
        c = _compress_from_partials(per, pe_ref, w1_ref, w2_ref)
        o_ref[0] = c.astype(BF16)
        o_ref[1] = pltpu.roll(c, HEAD_DIM, axis=1).astype(BF16)


def _cmp_prompt(k_c, v_c, cmp_w):
    n_batch, t_len, _ = k_c.shape
    n_chunks = t_len // CMP_STRIDE
    seq = pl.BlockSpec((None, t_len, KV_W), lambda b: (b, 0, 0))
    out = pl.BlockSpec((None, N_KV, n_chunks, KV_W), lambda b: (b, 0, 0, 0))
    w_specs = [_const_spec(w.shape) for w in cmp_w]
    return pl.pallas_call(
        _cmp_prompt_kernel,
        grid=(n_batch,),
        in_specs=[seq, seq] + w_specs,
        out_specs=[out, out],
        out_shape=[jax.ShapeDtypeStruct((n_batch, N_KV, n_chunks, KV_W), BF16)] * 2,
        compiler_params=pltpu.CompilerParams(dimension_semantics=("arbitrary",),
                                             vmem_limit_bytes=VMEM_LIMIT),
        name="cmp_prompt",
    )(k_c, v_c, *cmp_w)


def _masked_softmax(s, mask):
    s = jnp.where(mask, s, NEG_INF)
    e = jnp.where(mask, jnp.exp(s - jnp.max(s, axis=-1, keepdims=True)), 0.0)
    return e / jnp.maximum(jnp.sum(e, axis=-1, keepdims=True), 1e-30)


def _block_scores(imp, j, qpos, n_blocks):
    cur = qpos >> 6
    forced = (j == 0) | (j == cur) | (j == cur - 1)
    score = jnp.where(forced, FORCED_SCORE, imp)
    score = jnp.where(j * SLC_LEN <= qpos, score, -1.0)
    return jnp.where((j >= 0) & (j < n_blocks), score, -2.0)


def _pair_heads(o_even, o_odd):
    lane = _iota(o_even.shape, 1)
    return jnp.where(lane < HEAD_DIM, o_even, pltpu.roll(o_odd, HEAD_DIM, axis=1))


def _attn_prompt_kernel(qp_ref, gn_ref, kcm_ref, vcm_ref, ksa_ref, vsp_ref, kwp_ref, vwp_ref, agg_ref,
                        o_ref, *, t_len):
    tq = gn_ref.shape[0]
    n_blocks = t_len // SLC_LEN
    i = pl.program_id(1)
    t0 = i * tq
    rows4 = HPG * tq

    lane = _iota((tq, LANES), 1)
    qpos = t0 + _iota((tq, LANES), 0)
    qpos4 = t0 + (_iota((rows4, LANES), 0) & (tq - 1))
    lane4 = _iota((rows4, LANES), 1)
    cvalid4 = (lane4 * CMP_STRIDE + (CMP_LEN - 1)) <= qpos4
    jblk = lane - HEAD_DIM
    in_range = (jblk >= 0) & (jblk < n_blocks)
    gates = _sigmoid(gn_ref[...])

    w_start = pl.multiple_of(jnp.maximum(t0 - WINDOW, 0), tq)
    w_len = WINDOW + tq
    wpos = w_start + _iota((rows4, w_len), 1)
    qpos_w = t0 + (_iota((rows4, w_len), 0) & (tq - 1))
    wmask = (wpos <= qpos_w) & (wpos >= qpos_w - WINDOW)

    kpos_d = t0 + _iota((rows4, tq), 1)
    qpos_d = t0 + (_iota((rows4, tq), 0) & (tq - 1))
    dmask = kpos_d <= qpos_d

    for g in range(N_KV):
        q4 = qp_ref[g * HPG:(g + 1) * HPG].reshape(rows4, LANES)

        p_c = _masked_softmax(_dot_nt(q4, kcm_ref[g]), cvalid4)
        o_c = _dot(p_c.astype(BF16), vcm_ref[g])
        psum = p_c[0:tq] + p_c[tq:2 * tq] + p_c[2 * tq:3 * tq] + p_c[3 * tq:4 * tq]
        imp = _dot(psum.astype(BF16), agg_ref[...])
        score = _block_scores(imp, jblk, qpos, n_blocks)
        rank = jnp.zeros((tq, LANES), jnp.int32)
        for jp in range(n_blocks):
            col = score[:, HEAD_DIM + jp:HEAD_DIM + jp + 1]
            ahead = (col > score) | ((col == score) & (jblk > jp))
            rank = rank + jnp.where(ahead, 1, 0)
        allowed = (rank < TOP_N) & in_range & (jblk * SLC_LEN <= qpos)
        bias = jnp.where(in_range & jnp.logical_not(allowed), MASK_BIAS, 0.0).astype(BF16)
        q_sel = q4 + jnp.concatenate([bias] * HPG, axis=0)

        def chunk(c, carry, causal):
            m, l, acc = carry
            off = pl.multiple_of(c * tq, tq)
            s = _dot_nt(q_sel, ksa_ref[g, pl.ds(off, tq), :])
            if causal:
                s = jnp.where(dmask, s, NEG_INF)
            m_new = jnp.maximum(m, jnp.max(s, axis=-1, keepdims=True))
            alpha = jnp.exp(m - m_new)
            p = jnp.exp(s - m_new)
            l = alpha * l + jnp.sum(p, axis=-1, keepdims=True)
            acc = alpha * acc + _dot(p.astype(BF16), vsp_ref[g, pl.ds(off, tq), :])
            return m_new, l, acc

        init = (jnp.full((rows4, 1), NEG_INF, F32), jnp.zeros((rows4, 1), F32),
                jnp.zeros((rows4, LANES), F32))
        carry = lax.fori_loop(0, i, lambda c, cr: chunk(c, cr, False), init)
        _, l_s, acc_s = chunk(i, carry, True)
        o_s = acc_s / l_s

        p_w = _masked_softmax(_dot_nt(q4, kwp_ref[g, pl.ds(w_start, w_len), :]), wmask)
        o_w = _dot(p_w.astype(BF16), vwp_ref[g, pl.ds(w_start, w_len), :])

        heads = []
        for h in range(HPG):
            col = (g * HPG + h) * 3
            rs = slice(h * tq, (h + 1) * tq)
            heads.append(gates[:, col:col + 1] * o_c[rs] + gates[:, col + 1:col + 2] * o_s[rs]
                         + gates[:, col + 2:col + 3] * o_w[rs])
        for pr in range(HPG // 2):
            c0 = (g * HPG + 2 * pr) * HEAD_DIM
            o_ref[:, c0:c0 + LANES] = _pair_heads(heads[2 * pr], heads[2 * pr + 1])


def _attn_prompt(qp, gn, kcm, vcm, ksa, vsp, kwp, vwp, agg):
    n_batch, _, t_len, _ = qp.shape
    tq = TQ
    n_cmp = kcm.shape[2]
    full = pl.BlockSpec((None, N_KV, t_len, LANES), lambda b, i: (b, 0, 0, 0))
    cmp_spec = pl.BlockSpec((None, N_KV, n_cmp, KV_W), lambda b, i: (b, 0, 0, 0))
    return pl.pallas_call(
        functools.partial(_attn_prompt_kernel, t_len=t_len),
        grid=(n_batch, t_len // tq),
        in_specs=[pl.BlockSpec((None, N_HEADS, tq, LANES), lambda b, i: (b, 0, i, 0)),
                  pl.BlockSpec((None, tq, LANES), lambda b, i: (b, i, 0)),
                  cmp_spec, cmp_spec, full, full, full, full, _const_spec(agg.shape)],
        out_specs=pl.BlockSpec((None, tq, NSA_W), lambda b, i: (b, i, 0)),
        out_shape=jax.ShapeDtypeStruct((n_batch, t_len, NSA_W), F32),
        compiler_params=pltpu.CompilerParams(dimension_semantics=("arbitrary", "arbitrary"),
                                             vmem_limit_bytes=VMEM_LIMIT),
        name="attn_prompt",
    )(qp, gn, kcm, vcm, ksa, vsp, kwp, vwp, agg)


def _tail_math(x, o_a, sga, c_pre, sgb, sma, smb, w):
    (dwb_ref, clg_ref, clb_ref, pww_ref, pwb_ref, wpa_ref, wpb_ref, wo_ref, fg_ref) = w
    c = c_pre + dwb_ref[...]
    mu = jnp.mean(c, axis=-1, keepdims=True)
    var = jnp.mean(jnp.square(c - mu), axis=-1, keepdims=True)
    cn = (c - mu) * lax.rsqrt(var + EPS) * clg_ref[...] + clb_ref[...]
    cp = _dot(_silu(cn).astype(BF16), pww_ref[...]) + pwb_ref[...]
    br_a = _dot((o_a * sga).astype(BF16), wpa_ref[...])
    br_b = _dot((cp * sgb).astype(BF16), wpb_ref[...])
    h = sma * br_a + smb * br_b
    xo = x + _dot(h.astype(BF16), wo_ref[...])
    return xo * lax.rsqrt(jnp.mean(xo * xo, axis=-1, keepdims=True) + EPS) * fg_ref[...]


HIST = 32


def _tail_prompt_kernel(x_ref, oa_ref, sga_ref, u_ref, sgb_ref, sma_ref, smb_ref, dw_ref, *rest):
    w, (y_ref, up_ref) = rest[:9], rest[9:]
    tm = x_ref.shape[0]

    @pl.when(pl.program_id(1) == 0)
    def _():
        up_ref[0:HIST, :] = jnp.zeros((HIST, CONV_C), F32)

    up_ref[HIST:HIST + tm, :] = u_ref[...]
    acc = jnp.zeros((tm, CONV_C), F32)
    for k in range(CONV_W):
        acc = acc + up_ref[pl.ds(HIST - (CONV_W - 1) + k, tm), :] * dw_ref[k:k + 1, :]
    up_ref[0:HIST, :] = up_ref[tm:tm + HIST, :]
    y_ref[...] = _tail_math(x_ref[...], oa_ref[...], sga_ref[...], acc, sgb_ref[...], sma_ref[...],
                            smb_ref[...], w)


def _tail_prompt(x, o_a, sga, u, sgb, sma, smb, dw, tail_w):
    n_batch, t_len, _ = x.shape
    tm = TM_TAIL

    def bt(width):
        return pl.BlockSpec((None, tm, width), lambda b, t: (b, t, 0))

    ins = [x, o_a, sga, u, sgb, sma, smb]
    return pl.pallas_call(
        _tail_prompt_kernel,
        grid=(n_batch, t_len // tm),
        in_specs=[bt(a.shape[-1]) for a in ins] + [_const_spec(dw.shape)] + [_const_spec(a.shape) for a in tail_w],
        out_specs=bt(D_MODEL),
        out_shape=jax.ShapeDtypeStruct((n_batch, t_len, D_MODEL), F32),
        scratch_shapes=[pltpu.VMEM((HIST + tm, CONV_C), F32)],
        compiler_params=pltpu.CompilerParams(dimension_semantics=("arbitrary", "arbitrary"),
                                             vmem_limit_bytes=VMEM_LIMIT),
        name="tail_prompt",
    )(*ins, dw, *tail_w)


S_ROWS = 8
UP_ROWS = 40


def _tail_sample_kernel(x_ref, oa_ref, sga_ref, ups_ref, sgb_ref, sma_ref, smb_ref, dws_ref, *rest,
                        n_new):
    w, (y_ref, c_ref) = rest[:9], rest[9:]
    n_batch = ups_ref.shape[0]
    c_ref[...] = jnp.zeros(c_ref.shape, F32)

    def body(b, carry):
        up = ups_ref[b]
        for t in range(n_new):
            c_ref[pl.ds(b * S_ROWS + t, 1), :] = jnp.sum(up * dws_ref[t], axis=0, keepdims=True)
        return carry

    lax.fori_loop(0, n_batch, body, 0)
    y_ref[...] = _tail_math(x_ref[...], oa_ref[...], sga_ref[...], c_ref[...], sgb_ref[...], sma_ref[...],
                            smb_ref[...], w)


def _tail_sample(x, o_a, sga, up_s, sgb, sma, smb, dw_shift, tail_w, n_new):
    ins = [x, o_a, sga, up_s, sgb, sma, smb, dw_shift] + list(tail_w)
    m = x.shape[0]
    return pl.pallas_call(
        functools.partial(_tail_sample_kernel, n_new=n_new),
        grid=(1,),
        in_specs=[_const_spec(a.shape) for a in ins],
        out_specs=_const_spec((m, D_MODEL)),
        out_shape=jax.ShapeDtypeStruct((m, D_MODEL), F32),
        scratch_shapes=[pltpu.VMEM((m, CONV_C), F32)],
        compiler_params=pltpu.CompilerParams(dimension_semantics=("arbitrary",),
                                             vmem_limit_bytes=VMEM_LIMIT),
        name="tail_sample",
    )(*ins)


def _group_q(q_ref, g):
    q = q_ref[g * HPG:(g + 1) * HPG].reshape(HPG * S_ROWS, LANES).astype(F32)
    return q if g == 0 else pltpu.roll(q, HEAD_DIM, axis=1)


def _group_out(o, g):
    return o if g == 0 else pltpu.roll(o, HEAD_DIM, axis=1)


def _scmp_kernel(pt_ref, *refs, n_pages_step, n_steps, past_len):
    p = n_pages_step
    kpages, vpages = refs[:p], refs[p:2 * p]
    (q_ref, pek_ref, w1k_ref, w2k_ref, pev_ref, w1v_ref, w2v_ref, agg_ref,
     oc_ref, idx_ref, ak_ref, av_ref, slab_ref) = refs[2 * p:]
    pg = pl.program_id(1)
    rows = (PAGE_SIZE // CMP_STRIDE) * p

    for pages, w1_ref, a_ref in ((kpages, w1k_ref, ak_ref), (vpages, w1v_ref, av_ref)):
        for i, pr in enumerate(pages):
            slab_ref[i * PAGE_SIZE:(i + 1) * PAGE_SIZE, :] = pr[...].T
        a_ref[pl.ds(pl.multiple_of(pg * rows, rows), rows), :] = _dot(
            _chunk_rows(slab_ref, rows).astype(BF16), w1_ref[...])

    @pl.when(pg == n_steps - 1)
    def _():
        kc = _compress_from_partials(ak_ref[...], pek_ref, w1k_ref, w2k_ref).astype(BF16)
        vc = _compress_from_partials(av_ref[...], pev_ref, w1v_ref, w2v_ref).astype(BF16)
        n_chunks = kc.shape[0]
        n_blocks = past_len // SLC_LEN + 1
        r32 = HPG * S_ROWS
        qpos32 = past_len + (_iota((r32, n_chunks), 0) & (S_ROWS - 1))
        cvalid = (_iota((r32, n_chunks), 1) * CMP_STRIDE + (CMP_LEN - 1)) <= qpos32
        nb_pad = agg_ref.shape[1]
        jl = _iota((S_ROWS, nb_pad), 1)
        jlf = jl.astype(F32)
        qpos8 = past_len + _iota((S_ROWS, nb_pad), 0)
        lane8 = _iota((S_ROWS, LANES), 1)
        for g in range(N_KV):
            qg = _group_q(q_ref, g).astype(BF16)
            p_c = _masked_softmax(_dot_nt(qg, kc), cvalid)
            oc_ref[g] = _group_out(_dot(p_c.astype(BF16), vc), g)
            psum = (p_c[0:S_ROWS] + p_c[S_ROWS:2 * S_ROWS] + p_c[2 * S_ROWS:3 * S_ROWS]
                    + p_c[3 * S_ROWS:4 * S_ROWS])
            imp = _dot(psum.astype(BF16), agg_ref[...])
            score = _block_scores(imp, jl, qpos8, n_blocks)
            idx = jnp.zeros((S_ROWS, LANES), F32)
            for k in range(TOP_N):
                mx = jnp.max(score, axis=-1, keepdims=True)
                am = jnp.min(jnp.where(score == mx, jlf, float(nb_pad)), axis=-1, keepdims=True)
                idx = jnp.where(lane8 == k, am, idx)
                score = jnp.where(jlf == am, -3.0, score)
            idx_ref[g] = idx.astype(jnp.int32)


def _scmp(page_table, kpool, vpool, qp_s, cmp_w, agg_s, past_len):
    n_batch, n_pages = page_table.shape
    p = PAGES_PER_STEP
    n_steps = n_pages // p
    n_chunks = n_pages * (PAGE_SIZE // CMP_STRIDE)

    def page_spec(i):
        return pl.BlockSpec((None, PAGE_SIZE, KV_W), lambda b, s, pt, i=i: (pt[b, s * p + i], 0, 0))

    def cspec(shape):
        nd = len(shape)
        return pl.BlockSpec(shape, lambda b, s, pt: (0,) * nd)

    page_specs = [page_spec(i) for i in range(p)]
    in_specs = (page_specs + page_specs
                + [pl.BlockSpec((None, N_HEADS, S_ROWS, LANES), lambda b, s, pt: (0, 0, b, 0))]
                + [cspec(w.shape) for w in cmp_w] + [cspec(agg_s.shape)])
    out_specs = [pl.BlockSpec((None, N_KV, HPG * S_ROWS, LANES), lambda b, s, pt: (b, 0, 0, 0)),
                 pl.BlockSpec((None, N_KV, S_ROWS, LANES), lambda b, s, pt: (b, 0, 0, 0))]
    out_shape = [jax.ShapeDtypeStruct((n_batch, N_KV, HPG * S_ROWS, LANES), F32),
                 jax.ShapeDtypeStruct((n_batch, N_KV, S_ROWS, LANES), jnp.int32)]
    return pl.pallas_call(
        functools.partial(_scmp_kernel, n_pages_step=p, n_steps=n_steps, past_len=past_len),
        grid_spec=pltpu.PrefetchScalarGridSpec(
            num_scalar_prefetch=1, grid=(n_batch, n_steps), in_specs=in_specs, out_specs=out_specs,
            scratch_shapes=[pltpu.VMEM((n_chunks, 2 * KV_W), F32)] * 2
            + [pltpu.VMEM((p * PAGE_SIZE, KV_W), F32)]),
        out_shape=out_shape,
        compiler_params=pltpu.CompilerParams(dimension_semantics=("arbitrary", "arbitrary"),
                                             vmem_limit_bytes=VMEM_LIMIT),
        name="sample_cmp",
    )(page_table, *([kpool] * p), *([vpool] * p), qp_s, *cmp_w, agg_s)


def _ssel_kernel(pt_ref, idx_ref, kpool_ref, vpool_ref, q_ref, ksn_ref, vsn_ref, kwn_ref, vwn_ref,
                 kwb_ref, vwb_ref, gn_ref, oc_ref, o_ref, kbuf_ref, vbuf_ref, sem_ref,
                 *, n_new, past_len):
    b = pl.program_id(0)
    n_pool_blocks = past_len // SLC_LEN
    blocks_page = PAGE_SIZE // SLC_LEN

    def block_copy(pool_ref, buf_ref, sem, g, qi, k):
        j = jnp.minimum(idx_ref[b, (g * n_new + qi) * TOP_N + k], n_pool_blocks - 1)
        page = pt_ref[b, j // blocks_page]
        return pltpu.make_async_copy(pool_ref.at[page, pl.ds(g * HEAD_DIM, HEAD_DIM), :],
                                     buf_ref.at[g * n_new + qi, :, pl.ds(k * PAGE_SIZE, PAGE_SIZE)], sem)

    slots = [(g, qi, k) for g in range(N_KV) for qi in range(n_new) for k in range(TOP_N)]
    for g, qi, k in slots:
        block_copy(kpool_ref, kbuf_ref, sem_ref.at[0], g, qi, k).start()
        block_copy(vpool_ref, vbuf_ref, sem_ref.at[1], g, qi, k).start()

    r32 = HPG * S_ROWS
    qi32 = _iota((r32, LANES), 0) & (S_ROWS - 1)
    n_sel = TOP_N * PAGE_SIZE
    half_id = _iota((r32, n_sel), 1) >> 6
    new_rows = LANES
    newcol = _iota((r32, new_rows), 1)
    qrow_new = _iota((r32, new_rows), 0) & (S_ROWS - 1)

    def pad_new(ref):
        return jnp.concatenate([ref[...], jnp.zeros((new_rows - S_ROWS, KV_W), F32)], axis=0).astype(BF16)

    ks_new, vs_new, kw_new, vw_new = pad_new(ksn_ref), pad_new(vsn_ref), pad_new(kwn_ref), pad_new(vwn_ref)

    w_buf = kwb_ref.shape[1]
    wpos = jnp.concatenate([past_len - w_buf + _iota((r32, w_buf), 1), past_len + newcol], axis=1)
    qpos_w = past_len + (_iota((r32, w_buf + new_rows), 0) & (S_ROWS - 1))
    wmask = (wpos <= qpos_w) & (wpos >= qpos_w - WINDOW)
    wmask = wmask & jnp.concatenate([jnp.full((r32, w_buf), True), newcol < n_new], axis=1)
    kw_t = kwb_ref[...].astype(BF16)
    vw_t = vwb_ref[...].astype(BF16)
    gates = _sigmoid(gn_ref[...])

    o_w = []
    q_g = []
    for g in range(N_KV):
        qg = _group_q(q_ref, g).astype(BF16)
        q_g.append(qg)
        s_w = jnp.concatenate([_dot(qg, kw_t), _dot_nt(qg, kw_new)], axis=1)
        p_w = _masked_softmax(s_w, wmask).astype(BF16)
        o_w.append(_group_out(_dot_nt(p_w[:, :w_buf], vw_t) + _dot(p_w[:, w_buf:], vw_new), g))

    for g, qi, k in slots:
        block_copy(kpool_ref, kbuf_ref, sem_ref.at[0], g, qi, k).wait()
        block_copy(vpool_ref, vbuf_ref, sem_ref.at[1], g, qi, k).wait()

    pad64 = jnp.zeros((r32, LANES - HEAD_DIM), F32)
    for g in range(N_KV):
        q64 = q_ref[g * HPG:(g + 1) * HPG].reshape(r32, LANES)[:, :HEAD_DIM]
        o_s = jnp.zeros((r32, LANES), F32)
        for qi in range(n_new):
            smask = jnp.full((r32, n_sel), False)
            n_hit = jnp.int32(0)
            for k in range(TOP_N):
                j = idx_ref[b, (g * n_new + qi) * TOP_N + k]
                smask = smask | (half_id == jnp.where(j < n_pool_blocks, blocks_page * k + j % blocks_page, -1))
                n_hit = n_hit + jnp.where(j == n_pool_blocks, 1, 0)
            nmask = (newcol <= qrow_new) & (newcol < jnp.where(n_hit > 0, n_new, 0))
            s_s = jnp.concatenate([_dot(q64, kbuf_ref[g * n_new + qi].astype(BF16)),
                                   _dot_nt(q_g[g], ks_new)], axis=1)
            p_s = _masked_softmax(s_s, jnp.concatenate([smask, nmask], axis=1)).astype(BF16)
            o_pool = _dot_nt(p_s[:, :n_sel], vbuf_ref[g * n_new + qi].astype(BF16))
            o_qi = jnp.concatenate([o_pool, pad64], axis=1) + _group_out(_dot(p_s[:, n_sel:], vs_new), g)
            o_s = jnp.where(qi32 == qi, o_qi, o_s)
        o_c = oc_ref[g]
        heads = []
        for h in range(HPG):
            col = (g * HPG + h) * 3
            rs = slice(h * S_ROWS, (h + 1) * S_ROWS)
            heads.append(gates[:, col:col + 1] * o_c[rs] + gates[:, col + 1:col + 2] * o_s[rs]
                         + gates[:, col + 2:col + 3] * o_w[g][rs])
        for pr in range(HPG // 2):
            c0 = (g * HPG + 2 * pr) * HEAD_DIM
            o_ref[:, c0:c0 + LANES] = _pair_heads(heads[2 * pr], heads[2 * pr + 1])


def _ssel(page_table, idx, kpool, vpool, qp_s, ks_n, vs_n, kw_n, vw_n, kw_buf, vw_buf, gn_s, oc,
          n_new, past_len):
    n_batch = page_table.shape[0]
    w_buf = kw_buf.shape[2]
    rows8 = pl.BlockSpec((S_ROWS, KV_W), lambda b, pt, ix: (b, 0))
    wbuf_spec = pl.BlockSpec((None, KV_W, w_buf), lambda b, pt, ix: (b, 0, 0))
    in_specs = [pl.BlockSpec(memory_space=pl.ANY), pl.BlockSpec(memory_space=pl.ANY),
                pl.BlockSpec((None, N_HEADS, S_ROWS, LANES), lambda b, pt, ix: (0, 0, b, 0)),
                rows8, rows8, rows8, rows8, wbuf_spec, wbuf_spec,
                pl.BlockSpec((S_ROWS, LANES), lambda b, pt, ix: (b, 0)),
                pl.BlockSpec((None, N_KV, HPG * S_ROWS, LANES), lambda b, pt, ix: (b, 0, 0, 0))]
    return pl.pallas_call(
        functools.partial(_ssel_kernel, n_new=n_new, past_len=past_len),
        grid_spec=pltpu.PrefetchScalarGridSpec(
            num_scalar_prefetch=2, grid=(n_batch,), in_specs=in_specs,
            out_specs=pl.BlockSpec((S_ROWS, NSA_W), lambda b, pt, ix: (b, 0)),
            scratch_shapes=[pltpu.VMEM((N_KV * n_new, HEAD_DIM, TOP_N * PAGE_SIZE), F32)] * 2
            + [pltpu.SemaphoreType.DMA((2,))]),
        out_shape=jax.ShapeDtypeStruct((n_batch * S_ROWS, NSA_W), F32),
        compiler_params=pltpu.CompilerParams(dimension_semantics=("arbitrary",),
                                             vmem_limit_bytes=VMEM_LIMIT),
        name="sample_sel",
    )(page_table, idx, kpool, vpool, qp_s, ks_n, vs_n, kw_n, vw_n, kw_buf, vw_buf, gn_s, oc)


def _prep_w_in(w_in):
    q = w_in[:, :NSA_W].reshape(D_MODEL, N_HEADS, HEAD_DIM)
    q = jnp.pad(q, ((0, 0), (0, 0), (0, LANES - HEAD_DIM))).reshape(D_MODEL, N_HEADS * LANES)
    c = NSA_W
    kv = w_in[:, c:c + 6 * KV_W]
    c += 6 * KV_W
    gn = jnp.pad(w_in[:, c:c + 3 * N_HEADS], ((0, 0), (0, LANES - 3 * N_HEADS)))
    c += 3 * N_HEADS
    rest = w_in[:, c:]
    return jnp.concatenate([q, kv, gn, rest], axis=1).astype(BF16)


def _prep_cmp(pe, w1, w2):
    w1r = w1.reshape(CMP_R, CMP_STRIDE, HEAD_DIM, HEAD_DIM)
    eye = jnp.eye(N_KV, dtype=w1.dtype)
    w1big = jnp.einsum('rjdh,ge->jgdreh', w1r, eye).reshape(CMP_STRIDE * KV_W, CMP_R * KV_W)
    pe_rows = jnp.broadcast_to(pe.reshape(CMP_R, CMP_STRIDE, 1, HEAD_DIM),
                               (CMP_R, CMP_STRIDE, N_KV, HEAD_DIM)).reshape(CMP_R, CMP_STRIDE * KV_W)
    pe_rows = jnp.pad(pe_rows, ((0, SUBLANES - CMP_R), (0, 0)))
    w2big = jnp.einsum('hd,ge->ghed', w2, eye).reshape(KV_W, KV_W)
    return pe_rows.astype(BF16), w1big.astype(BF16), w2big.astype(BF16)


def _agg_matrix(n_rows, n_blocks, n_cols, col0):
    i = jnp.arange(n_rows)[:, None] * CMP_STRIDE
    jj = jnp.arange(n_cols)[None, :] - col0
    hit = (i < jj * SLC_LEN + SLC_LEN) & (i + CMP_LEN > jj * SLC_LEN) & (jj >= 0) & (jj < n_blocks)
    return hit.astype(BF16)


def kernel(x_prompt, x_sample, cache_k_cmp, cache_v_cmp, cache_k_slc, cache_v_slc, cache_k_win, cache_v_win, state_conv, page_table, ln_g, w_in, pe_k, w1_k, w2_k, pe_v, w1_v, w2_v, dw_k, dw_b, cln_g, cln_b, pw_w, pw_b, w_pa, w_pb, w_o, final_g):
    assert w_in.shape[0] == 1, "single-layer stack"
    n_b, t_len, _ = x_prompt.shape
    n_db, n_new, _ = x_sample.shape
    n_pages = page_table.shape[1]
    past_len = n_pages * PAGE_SIZE
    w_buf = cache_k_win.shape[2]
    assert n_new <= S_ROWS and t_len % TQ == 0 and t_len >= WINDOW + TQ
    assert n_new < CMP_STRIDE, "the new rows must not complete a compression chunk"
    assert t_len // SLC_LEN <= LANES - HEAD_DIM and past_len % PAGE_SIZE == 0

    w_all = _prep_w_in(w_in[0])
    lng = ln_g[0].reshape(1, D_MODEL)
    cmp_w = _prep_cmp(pe_k[0], w1_k[0], w2_k[0]) + _prep_cmp(pe_v[0], w1_v[0], w2_v[0])
    tail_w = (dw_b[0].reshape(1, CONV_C), cln_g[0].reshape(1, CONV_C), cln_b[0].reshape(1, CONV_C),
              pw_w[0].astype(BF16), pw_b[0].reshape(1, CONV_C), w_pa[0].astype(BF16), w_pb[0].astype(BF16),
              w_o[0].astype(BF16), final_g.reshape(1, D_MODEL))
    dw = jnp.pad(dw_k[0], ((0, 1), (0, 0)))

    (qp, kct, vct, kst, vst, kwt, vwt, gn, sga, u, sgb, sma, smb, k_c, v_c, ksa, vsp, kwp, vwp) = _proj(
        x_prompt.reshape(n_b * t_len, D_MODEL), lng, w_all, n_b, t_len, (0, 1), True)
    seq = lambda a: a.reshape(n_b, t_len, a.shape[-1])
    kcm, vcm = _cmp_prompt(seq(k_c), seq(v_c), cmp_w)
    n_cmp_rows = t_len // CMP_STRIDE
    agg_p = _agg_matrix(n_cmp_rows, t_len // SLC_LEN, LANES, HEAD_DIM)
    o_a = _attn_prompt(qp, seq(gn), kcm, vcm, ksa, vsp, kwp, vwp, agg_p)
    y_prompt = _tail_prompt(x_prompt, o_a, seq(sga), seq(u), seq(sgb), seq(sma), seq(smb), dw, tail_w)

    xs = jnp.pad(x_sample, ((0, 0), (0, S_ROWS - n_new), (0, 0))).reshape(n_db * S_ROWS, D_MODEL)
    (qp_s, kct_s, vct_s, kst_s, vst_s, kwt_s, vwt_s, gn_s, sga_s, u_s, sgb_s, sma_s, smb_s,
     k_s_s, v_s_s, k_w_s, v_w_s) = _proj(xs, lng, w_all, 1, n_db * S_ROWS, (2, 3, 4, 5), False)
    pool = lambda c: jnp.transpose(c[0], (0, 2, 3, 1)).reshape(c.shape[1], KV_W, PAGE_SIZE)
    n_blocks_s = past_len // SLC_LEN + 1
    nb_pad = -(-n_blocks_s // LANES) * LANES
    agg_s = _agg_matrix(past_len // CMP_STRIDE, n_blocks_s, nb_pad, 0)
    agg_s = agg_s.at[past_len // CMP_STRIDE - 1].set(0)
    oc_s, idx_s = _scmp(page_table, pool(cache_k_cmp), pool(cache_v_cmp), qp_s, cmp_w, agg_s, past_len)
    idx_flat = idx_s[:, :, :n_new, :TOP_N].reshape(n_db, N_KV * n_new * TOP_N)
    win = lambda c: jnp.transpose(c[0], (0, 2, 3, 1)).reshape(n_db, KV_W, w_buf)
    o_a_s = _ssel(page_table, idx_flat, pool(cache_k_slc), pool(cache_v_slc), qp_s, k_s_s, v_s_s, k_w_s,
                  v_w_s, win(cache_k_win), win(cache_v_win), gn_s, oc_s, n_new, past_len)
    u_new = u_s.reshape(n_db, S_ROWS, CONV_C)[:, :n_new]
    up_s = jnp.concatenate([state_conv[0], u_new], axis=1)
    up_pad = jnp.pad(up_s, ((0, 0), (0, UP_ROWS - up_s.shape[1]), (0, 0)))
    dw_shift = jnp.stack([jnp.pad(dw_k[0], ((t, UP_ROWS - CONV_W - t), (0, 0))) for t in range(n_new)])
    y_s = _tail_sample(xs, o_a_s, sga_s, up_pad, sgb_s, sma_s, smb_s, dw_shift, tail_w, n_new)
    y_sample = y_s.reshape(n_db, S_ROWS, D_MODEL)[:, :n_new]

    def p_rows(at):
        return jnp.transpose(at.reshape(n_b, N_KV, HEAD_DIM, at.shape[-1]), (0, 3, 1, 2))[None]

    def s_rows(at):
        a = at.reshape(N_KV, HEAD_DIM, n_db, S_ROWS)[:, :, :, :n_new]
        return jnp.transpose(a, (2, 3, 0, 1))[None]

    win_p = min(WINDOW, t_len)
    p_conv = seq(u)[:, -(CONV_W - 1):][None]
    s_k_win = jnp.concatenate([cache_k_win[0], s_rows(kwt_s)[0]], axis=1)[:, -w_buf:][None]
    s_v_win = jnp.concatenate([cache_v_win[0], s_rows(vwt_s)[0]], axis=1)[:, -w_buf:][None]
    s_conv = up_s[:, -(CONV_W - 1):][None]
    return (y_prompt, y_sample, p_rows(kct), p_rows(vct), p_rows(kst), p_rows(vst),
            p_rows(kwt[:, :, -win_p:]), p_rows(vwt[:, :, -win_p:]), p_conv,
            s_rows(kct_s), s_rows(vct_s), s_rows(kst_s), s_rows(vst_s), s_k_win, s_v_win, s_conv)
```

```python
import functools

import jax
import jax.numpy as jnp
from jax import lax
from jax.experimental import pallas as pl
from jax.experimental.pallas import tpu as pltpu

F32 = jnp.float32
BF16 = jnp.bfloat16

D_MODEL = 1024
N_HEADS = 8
HEAD_DIM = 64
N_KV = 2
HPG = N_HEADS // N_KV
NSA_W = N_HEADS * HEAD_DIM
KV_W = N_KV * HEAD_DIM
CMP_LEN = 32
CMP_STRIDE = 16
CMP_R = CMP_LEN // CMP_STRIDE
SLC_LEN = 64
TOP_N = 16
WINDOW = 512
CONV_C = 512
CONV_W = 31
PAGE_SIZE = 128
EPS = 1e-6
FORCED_SCORE = 1e4
NEG_INF = -1e30
MASK_BIAS = -1e4

LANES = 128
SUBLANES = 8
VMEM_LIMIT = 56 * 1024 * 1024

C_Q = 0
C_KV = C_Q + N_HEADS * LANES
C_GN = C_KV + 6 * KV_W
C_GA = C_GN + LANES
C_GLU = C_GA + NSA_W
C_GB = C_GLU + 2 * CONV_C
C_MRG = C_GB + CONV_C
C_END = C_MRG + 2 * D_MODEL

TM_PROJ = 256
TQ = 256
TM_TAIL = 256
PAGES_PER_STEP = 32


def _sigmoid(x):
    return jax.nn.sigmoid(x)


def _silu(x):
    return jax.nn.silu(x)


def _dot(a, b):
    return jnp.dot(a, b, preferred_element_type=F32)


def _dot_nt(a, b):
    return lax.dot_general(a, b, (((1,), (1,)), ((), ())), preferred_element_type=F32)


def _iota(shape, dim):
    return lax.broadcasted_iota(jnp.int32, shape, dim)


def _const_spec(shape):
    nd = len(shape)
    return pl.BlockSpec(shape, lambda *_: (0,) * nd)


def _proj_kernel(x_ref, lng_ref, w_ref, qp_ref, kct_ref, vct_ref, kst_ref, vst_ref, kwt_ref, vwt_ref,
                 gn_ref, sga_ref, u_ref, sgb_ref, sma_ref, smb_ref, *more_refs, tiles_per_batch, row_ids,
                 attn_ops):
    x = x_ref[...]
    tm = x.shape[0]
    xn = x * lax.rsqrt(jnp.mean(x * x, axis=-1, keepdims=True) + EPS) * lng_ref[...]
    xn = xn.astype(BF16)

    zq = _dot(xn, w_ref[:, C_Q:C_KV]) * (HEAD_DIM ** -0.5)
    zgn = _dot(xn, w_ref[:, C_GN:C_GA])
    for h in range(N_HEADS):
        qh = zq[:, h * LANES:(h + 1) * LANES]
        qp_ref[h] = (qh.T if attn_ops else qh).astype(BF16)
    gn_ref[...] = zgn.T if attn_ops else zgn

    zkv = _dot(xn, w_ref[:, C_KV:C_GN])
    kv = [zkv[:, i * KV_W:(i + 1) * KV_W] for i in range(6)]
    kvt = [v.T for v in kv]
    for r, v in zip((kct_ref, vct_ref, kst_ref, vst_ref, kwt_ref, vwt_ref), kvt):
        r[...] = v
    for r, i in zip(more_refs[:len(row_ids)], row_ids):
        r[...] = kv[i]

    if attn_ops:
        ksa_ref, vst_g_ref, kwp_ref, vwt_g_ref = more_refs[len(row_ids):]
        t0 = (pl.program_id(0) % tiles_per_batch) * tm
        lane = _iota((tm, LANES), 1)
        blk = (t0 + _iota((tm, LANES), 0)) >> 6
        onehot = jnp.where((lane - HEAD_DIM) == blk, 1.0, 0.0)
        ones_row = jnp.where(_iota((HEAD_DIM, tm), 0) == 0, 1.0, 0.0)
        for g in range(N_KV):
            rows_g = slice(g * HEAD_DIM, (g + 1) * HEAD_DIM)

            def grp(v):
                return v if g == 0 else pltpu.roll(v, HEAD_DIM, axis=1)

            ksa_ref[g] = jnp.where(lane < HEAD_DIM, grp(kv[2]), onehot).astype(BF16)
            kwp_ref[g] = grp(kv[4]).astype(BF16)
            for ref, vt in ((vst_g_ref, kvt[3]), (vwt_g_ref, kvt[5])):
                v_aug = jnp.concatenate([vt[rows_g], ones_row], axis=0).astype(BF16)
                for j in range(tm // KEY_CHUNK):
                    ref[g, j] = v_aug[:, j * KEY_CHUNK:(j + 1) * KEY_CHUNK]

    sga_ref[...] = _silu(_dot(xn, w_ref[:, C_GA:C_GLU]))
    zglu = _dot(xn, w_ref[:, C_GLU:C_GB])
    u_ref[...] = zglu[:, :CONV_C] * _sigmoid(zglu[:, CONV_C:])
    sgb_ref[...] = _silu(_dot(xn, w_ref[:, C_GB:C_MRG]))
    zm = _dot(xn, w_ref[:, C_MRG:C_END])
    sma_ref[...] = _sigmoid(zm[:, :D_MODEL])
    smb_ref[...] = _sigmoid(zm[:, D_MODEL:])


def _proj(x2d, ln_g, w_all, n_batch, t_len, row_ids, attn_ops):
    m = x2d.shape[0]
    tm = min(TM_PROJ, t_len)
    tpb = t_len // tm
    grid = (m // tm,)

    def row_spec(width):
        return pl.BlockSpec((tm, width), lambda i: (i, 0))

    def bt_spec(lead, width):
        return pl.BlockSpec((None, lead, tm, width), lambda i: (i // tpb, 0, i % tpb, 0))

    t_spec = pl.BlockSpec((None, KV_W, tm), lambda i: (i // tpb, 0, i % tpb))
    if attn_ops:
        out_shape = [jax.ShapeDtypeStruct((n_batch, N_HEADS, LANES, t_len), BF16)]
        out_specs = [pl.BlockSpec((None, N_HEADS, LANES, tm), lambda i: (i // tpb, 0, 0, i % tpb))]
    else:
        out_shape = [jax.ShapeDtypeStruct((n_batch, N_HEADS, t_len, LANES), BF16)]
        out_specs = [bt_spec(N_HEADS, LANES)]
    out_shape += [jax.ShapeDtypeStruct((n_batch, KV_W, t_len), F32)] * 6
    out_specs += [t_spec] * 6
    if attn_ops:
        out_shape.append(jax.ShapeDtypeStruct((n_batch, LANES, t_len), F32))
        out_specs.append(t_spec)
    else:
        out_shape.append(jax.ShapeDtypeStruct((m, LANES), F32))
        out_specs.append(row_spec(LANES))
    for width in (NSA_W, CONV_C, CONV_C, D_MODEL, D_MODEL):
        out_shape.append(jax.ShapeDtypeStruct((m, width), F32))
        out_specs.append(row_spec(width))
    out_shape += [jax.ShapeDtypeStruct((m, KV_W), F32)] * len(row_ids)
    out_specs += [row_spec(KV_W)] * len(row_ids)
    if attn_ops:
        cpt = tm // KEY_CHUNK
        k_shape = jax.ShapeDtypeStruct((n_batch, N_KV, t_len, LANES), BF16)
        vt_shape = jax.ShapeDtypeStruct((n_batch, N_KV, t_len // KEY_CHUNK, LANES, KEY_CHUNK), BF16)
        vt_spec = pl.BlockSpec((None, N_KV, cpt, LANES, KEY_CHUNK), lambda i: (i // tpb, 0, i % tpb, 0, 0))
        out_shape += [k_shape, vt_shape, k_shape, vt_shape]
        out_specs += [bt_spec(N_KV, LANES), vt_spec, bt_spec(N_KV, LANES), vt_spec]

    return pl.pallas_call(
        functools.partial(_proj_kernel, tiles_per_batch=tpb, row_ids=row_ids, attn_ops=attn_ops),
        grid=grid,
        in_specs=[row_spec(D_MODEL), _const_spec((1, D_MODEL)), _const_spec((D_MODEL, C_END))],
        out_specs=out_specs,
        out_shape=out_shape,
        compiler_params=pltpu.CompilerParams(dimension_semantics=("arbitrary",),
                                             vmem_limit_bytes=VMEM_LIMIT),
        name="proj",
    )(x2d, ln_g, w_all)


def _compress_from_partials(per, pe_ref, w1_ref, w2_ref):
    n = per.shape[0]
    hp = _dot(pe_ref[...], w1_ref[...])
    h0 = hp[0:1, :KV_W] + hp[1:2, KV_W:]
    h = h0 + per[:, :KV_W] + pltpu.roll(per[:, KV_W:], n - 1, axis=0)
    return _dot(_silu(h).astype(BF16), w2_ref[...])


def _chunk_rows(ref, n_chunks):
    return jnp.concatenate([ref[pl.ds(j, n_chunks, stride=CMP_STRIDE), :] for j in range(CMP_STRIDE)],
                           axis=-1)


def _cmp_prompt_kernel(kc_ref, vc_ref, pek_ref, w1k_ref, w2k_ref, pev_ref, w1v_ref, w2v_ref,
                       ok_ref, ov_ref):
    n_chunks = kc_ref.shape[0] // CMP_STRIDE
    for x_ref, pe_ref, w1_ref, w2_ref, o_ref, transposed in (
            (kc_ref, pek_ref, w1k_ref, w2k_ref, ok_ref, False), (vc_ref, pev_ref, w1v_ref, w2v_ref, ov_ref, True)):
        per = _dot(_chunk_rows(x_ref, n_chunks).astype(BF16), w1_ref[...])
        c = _compress_from_partials(per, pe_ref, w1_ref, w2_ref)
        for g, cg in enumerate((c, pltpu.roll(c, HEAD_DIM, axis=1))):
            o_ref[g] = (cg.T if transposed else cg).astype(BF16)


def _cmp_prompt(k_c, v_c, cmp_w):
    n_batch, t_len, _ = k_c.shape
    n_chunks = t_len // CMP_STRIDE
    seq = pl.BlockSpec((None, t_len, KV_W), lambda b: (b, 0, 0))
    out = pl.BlockSpec((None, N_KV, n_chunks, KV_W), lambda b: (b, 0, 0, 0))
    w_specs = [_const_spec(w.shape) for w in cmp_w]
    return pl.pallas_call(
        _cmp_prompt_kernel,
        grid=(n_batch,),
        in_specs=[seq, seq] + w_specs,
        out_specs=[out, out],
        out_shape=[jax.ShapeDtypeStruct((n_batch, N_KV, n_chunks, KV_W), BF16)] * 2,
        compiler_params=pltpu.CompilerParams(dimension_semantics=("arbitrary",),
                                             vmem_limit_bytes=VMEM_LIMIT),
        name="cmp_prompt",
    )(k_c, v_c, *cmp_w)


def _masked_softmax(s, mask):
    s = jnp.where(mask, s, NEG_INF)
    e = jnp.where(mask, jnp.exp(s - jnp.max(s, axis=-1, keepdims=True)), 0.0)
    return e / jnp.maximum(jnp.sum(e, axis=-1, keepdims=True), 1e-30)


def _softmax_rows(s, mask):
    s = jnp.where(mask, s, NEG_INF)
    e = jnp.where(mask, jnp.exp(s - jnp.max(s, axis=0, keepdims=True)), 0.0)
    return e / jnp.maximum(jnp.sum(e, axis=0, keepdims=True), 1e-30)


def _block_scores(imp, j, qpos, n_blocks):
    cur = qpos >> 6
    forced = (j == 0) | (j == cur) | (j == cur - 1)
    score = jnp.where(forced, FORCED_SCORE, imp)
    score = jnp.where(j * SLC_LEN <= qpos, score, -1.0)
    return jnp.where((j >= 0) & (j < n_blocks), score, -2.0)


def _pair_heads(o_even, o_odd):
    lane = _iota(o_even.shape, 1)
    return jnp.where(lane < HEAD_DIM, o_even, pltpu.roll(o_odd, HEAD_DIM, axis=1))


KEY_CHUNK = 128
SEL_ROWS = 32


def _attn_prompt_kernel(qt_ref, gnt_ref, kcm_ref, vcmt_ref, ksa_ref, vst_ref, kwp_ref, vwt_ref, aggt_ref,
                        o_ref, oacc_ref, qsel_ref, m_ref, acc_ref, *, t_len):
    tq = gnt_ref.shape[1]
    n_blocks = t_len // SLC_LEN
    i = pl.program_id(1)
    t0 = i * tq
    chunks_tile = tq // KEY_CHUNK

    gates = _sigmoid(gnt_ref[...])
    qpos_c = t0 + _iota((KV_W, tq), 1)
    cvalid = (_iota((KV_W, tq), 0) * CMP_STRIDE + (CMP_LEN - 1)) <= qpos_c
    jblk = _iota((SEL_ROWS, tq), 0)
    qpos_j = t0 + _iota((SEL_ROWS, tq), 1)
    row_minus_lane = _iota((KEY_CHUNK, tq), 0) - _iota((KEY_CHUNK, tq), 1)

    def gate(hh, branch):
        r = hh * 3 + branch
        return gates[r:r + 1, :]

    def flash(q_ref, k_ref, vt_ref, c_lo, banded, branch):
        def step(c, hh, diagonal):
            g = hh // HPG
            kc = k_ref[g, pl.ds(pl.multiple_of(c * KEY_CHUNK, KEY_CHUNK), KEY_CHUNK), :]
            s = _dot(kc, q_ref[hh])
            off = c * KEY_CHUNK - t0
            if diagonal:
                s = jnp.where(row_minus_lane + off <= 0, s, NEG_INF)
            elif banded:
                s = jnp.where(row_minus_lane + off >= -WINDOW, s, NEG_INF)
            m = m_ref[hh]
            m_new = jnp.maximum(m, jnp.max(s, axis=0, keepdims=True))
            p = jnp.exp(s - m_new).astype(BF16)
            acc_ref[hh] = jnp.exp(m - m_new) * acc_ref[hh] + _dot(vt_ref[g, c], p)
            m_ref[hh] = m_new

        m_ref[...] = jnp.full(m_ref.shape, NEG_INF, F32)
        acc_ref[...] = jnp.zeros(acc_ref.shape, F32)

        def body(c, carry):
            for hh in range(N_HEADS):
                step(c, hh, False)
            return carry

        lax.fori_loop(c_lo, i * chunks_tile, body, 0)
        for cd in range(chunks_tile):
            for hh in range(N_HEADS):
                step(i * chunks_tile + cd, hh, True)
        for hh in range(N_HEADS):
            acc = acc_ref[hh]
            oacc_ref[hh] = oacc_ref[hh] + gate(hh, branch) * (acc / acc[HEAD_DIM:HEAD_DIM + 1, :])

    for g in range(N_KV):
        psum = None
        for h in range(HPG):
            hh = g * HPG + h
            p_c = _softmax_rows(_dot(kcm_ref[g], qt_ref[hh]), cvalid)
            oacc_ref[hh] = gate(hh, 0) * _dot(vcmt_ref[g], p_c.astype(BF16))
            psum = p_c if psum is None else psum + p_c
        imp = _dot(aggt_ref[...], psum.astype(BF16))[:SEL_ROWS]
        score = _block_scores(imp, jblk, qpos_j, n_blocks)
        rank = jnp.zeros((SEL_ROWS, tq), jnp.int32)
        for jp in range(n_blocks):
            other = score[jp:jp + 1, :]
            ahead = (other > score) | ((other == score) & (jblk > jp))
            rank = rank + jnp.where(ahead, 1, 0)
        allowed = (rank < TOP_N) & (jblk < n_blocks) & (jblk * SLC_LEN <= qpos_j)
        bias = jnp.where(allowed, 0.0, MASK_BIAS).astype(BF16)
        pad = jnp.zeros((LANES - HEAD_DIM - SEL_ROWS, tq), BF16)
        for h in range(HPG):
            qsel_ref[g * HPG + h] = jnp.concatenate([qt_ref[g * HPG + h, 0:HEAD_DIM, :], bias, pad], axis=0)

    flash(qsel_ref, ksa_ref, vst_ref, 0, False, 1)
    flash(qt_ref, kwp_ref, vwt_ref, jnp.maximum(i * chunks_tile - WINDOW // KEY_CHUNK, 0), True, 2)
    for pr in range(N_HEADS // 2):
        two = jnp.concatenate([oacc_ref[2 * pr, 0:HEAD_DIM, :], oacc_ref[2 * pr + 1, 0:HEAD_DIM, :]], axis=0)
        o_ref[:, pr * LANES:(pr + 1) * LANES] = two.T


def _attn_prompt(qt, gnt, kcm, vcmt, ksa, vst, kwp, vwt, aggt):
    n_batch, _, _, t_len = qt.shape
    tq = TQ
    n_cmp = kcm.shape[2]
    assert tq % KEY_CHUNK == 0 and WINDOW % KEY_CHUNK == 0 and t_len // SLC_LEN <= SEL_ROWS
    assert n_cmp == KV_W, "one lane tile of compressed blocks"
    k_spec = pl.BlockSpec((None, N_KV, t_len, LANES), lambda b, i: (b, 0, 0, 0))
    vt_spec = pl.BlockSpec((None, N_KV, t_len // KEY_CHUNK, LANES, KEY_CHUNK), lambda b, i: (b, 0, 0, 0, 0))
    cmp_spec = pl.BlockSpec((None, N_KV, n_cmp, KV_W), lambda b, i: (b, 0, 0, 0))
    return pl.pallas_call(
        functools.partial(_attn_prompt_kernel, t_len=t_len),
        grid=(n_batch, t_len // tq),
        in_specs=[pl.BlockSpec((None, N_HEADS, LANES, tq), lambda b, i: (b, 0, 0, i)),
                  pl.BlockSpec((None, LANES, tq), lambda b, i: (b, 0, i)),
                  cmp_spec, cmp_spec, k_spec, vt_spec, k_spec, vt_spec, _const_spec(aggt.shape)],
        out_specs=pl.BlockSpec((None, tq, NSA_W), lambda b, i: (b, i, 0)),
        out_shape=jax.ShapeDtypeStruct((n_batch, t_len, NSA_W), F32),
        scratch_shapes=[pltpu.VMEM((N_HEADS, LANES, tq), F32), pltpu.VMEM((N_HEADS, LANES, tq), BF16),
                        pltpu.VMEM((N_HEADS, 1, tq), F32), pltpu.VMEM((N_HEADS, LANES, tq), F32)],
        compiler_params=pltpu.CompilerParams(dimension_semantics=("arbitrary", "arbitrary"),
                                             vmem_limit_bytes=VMEM_LIMIT),
        name="attn_prompt",
    )(qt, gnt, kcm, vcmt, ksa, vst, kwp, vwt, aggt)


def _tail_math(x, o_a, sga, c_pre, sgb, sma, smb, w):
    (dwb_ref, clg_ref, clb_ref, pww_ref, pwb_ref, wpa_ref, wpb_ref, wo_ref, fg_ref) = w
    c = c_pre + dwb_ref[...]
    mu = jnp.mean(c, axis=-1, keepdims=True)
    var = jnp.mean(jnp.square(c - mu), axis=-1, keepdims=True)
    cn = (c - mu) * lax.rsqrt(var + EPS) * clg_ref[...] + clb_ref[...]
    cp = _dot(_silu(cn).astype(BF16), pww_ref[...]) + pwb_ref[...]
    br_a = _dot((o_a * sga).astype(BF16), wpa_ref[...])
    br_b = _dot((cp * sgb).astype(BF16), wpb_ref[...])
    h = sma * br_a + smb * br_b
    xo = x + _dot(h.astype(BF16), wo_ref[...])
    return xo * lax.rsqrt(jnp.mean(xo * xo, axis=-1, keepdims=True) + EPS) * fg_ref[...]


HIST = 32


def _tail_prompt_kernel(x_ref, oa_ref, sga_ref, u_ref, sgb_ref, sma_ref, smb_ref, dw_ref, *rest):
    w, (y_ref, up_ref) = rest[:9], rest[9:]
    tm = x_ref.shape[0]

    @pl.when(pl.program_id(1) == 0)
    def _():
        up_ref[0:HIST, :] = jnp.zeros((HIST, CONV_C), F32)

    up_ref[HIST:HIST + tm, :] = u_ref[...]
    acc = jnp.zeros((tm, CONV_C), F32)
    for k in range(CONV_W):
        acc = acc + up_ref[pl.ds(HIST - (CONV_W - 1) + k, tm), :] * dw_ref[k:k + 1, :]
    up_ref[0:HIST, :] = up_ref[tm:tm + HIST, :]
    y_ref[...] = _tail_math(x_ref[...], oa_ref[...], sga_ref[...], acc, sgb_ref[...], sma_ref[...],
                            smb_ref[...], w)


def _tail_prompt(x, o_a, sga, u, sgb, sma, smb, dw, tail_w):
    n_batch, t_len, _ = x.shape
    tm = TM_TAIL

    def bt(width):
        return pl.BlockSpec((None, tm, width), lambda b, t: (b, t, 0))

    ins = [x, o_a, sga, u, sgb, sma, smb]
    return pl.pallas_call(
        _tail_prompt_kernel,
        grid=(n_batch, t_len // tm),
        in_specs=[bt(a.shape[-1]) for a in ins] + [_const_spec(dw.shape)] + [_const_spec(a.shape) for a in tail_w],
        out_specs=bt(D_MODEL),
        out_shape=jax.ShapeDtypeStruct((n_batch, t_len, D_MODEL), F32),
        scratch_shapes=[pltpu.VMEM((HIST + tm, CONV_C), F32)],
        compiler_params=pltpu.CompilerParams(dimension_semantics=("arbitrary", "arbitrary"),
                                             vmem_limit_bytes=VMEM_LIMIT),
        name="tail_prompt",
    )(*ins, dw, *tail_w)


S_ROWS = 8
UP_ROWS = 40


def _tail_sample_kernel(x_ref, oa_ref, sga_ref, ups_ref, sgb_ref, sma_ref, smb_ref, dws_ref, *rest,
                        n_new):
    w, (y_ref, c_ref) = rest[:9], rest[9:]
    n_batch = ups_ref.shape[0]
    c_ref[...] = jnp.zeros(c_ref.shape, F32)

    def body(b, carry):
        up = ups_ref[b]
        for t in range(n_new):
            c_ref[pl.ds(b * S_ROWS + t, 1), :] = jnp.sum(up * dws_ref[t], axis=0, keepdims=True)
        return carry

    lax.fori_loop(0, n_batch, body, 0)
    y_ref[...] = _tail_math(x_ref[...], oa_ref[...], sga_ref[...], c_ref[...], sgb_ref[...], sma_ref[...],
                            smb_ref[...], w)


def _tail_sample(x, o_a, sga, up_s, sgb, sma, smb, dw_shift, tail_w, n_new):
    ins = [x, o_a, sga, up_s, sgb, sma, smb, dw_shift] + list(tail_w)
    m = x.shape[0]
    return pl.pallas_call(
        functools.partial(_tail_sample_kernel, n_new=n_new),
        grid=(1,),
        in_specs=[_const_spec(a.shape) for a in ins],
        out_specs=_const_spec((m, D_MODEL)),
        out_shape=jax.ShapeDtypeStruct((m, D_MODEL), F32),
        scratch_shapes=[pltpu.VMEM((m, CONV_C), F32)],
        compiler_params=pltpu.CompilerParams(dimension_semantics=("arbitrary",),
                                             vmem_limit_bytes=VMEM_LIMIT),
        name="tail_sample",
    )(*ins)


def _group_q(q_ref, g):
    q = q_ref[g * HPG:(g + 1) * HPG].reshape(HPG * S_ROWS, LANES).astype(F32)
    return q if g == 0 else pltpu.roll(q, HEAD_DIM, axis=1)


def _group_out(o, g):
    return o if g == 0 else pltpu.roll(o, HEAD_DIM, axis=1)


def _scmp_kernel(pt_ref, *refs, n_pages_step, n_steps, past_len):
    p = n_pages_step
    kpages, vpages = refs[:p], refs[p:2 * p]
    (q_ref, perm_ref, pek_ref, w1k_ref, w2k_ref, pev_ref, w1v_ref, w2v_ref, agg_ref,
     oc_ref, idx_ref, ak_ref, av_ref) = refs[2 * p:]
    pg = pl.program_id(1)
    chunks_page = PAGE_SIZE // CMP_STRIDE
    rows = chunks_page * p

    for pages, w1_ref, a_ref in ((kpages, w1k_ref, ak_ref), (vpages, w1v_ref, av_ref)):
        blocks = []
        for pr in pages:
            y = _dot_nt(perm_ref[...], pr[...].astype(BF16))
            blocks.append(jnp.concatenate(
                [y[j * chunks_page:(j + 1) * chunks_page] for j in range(CMP_STRIDE)], axis=-1))
        xc = jnp.concatenate(blocks, axis=0).astype(BF16)
        a_ref[pl.ds(pl.multiple_of(pg * rows, rows), rows), :] = _dot(xc, w1_ref[...])

    @pl.when(pg == n_steps - 1)
    def _():
        kc = _compress_from_partials(ak_ref[...], pek_ref, w1k_ref, w2k_ref).astype(BF16)
        vc = _compress_from_partials(av_ref[...], pev_ref, w1v_ref, w2v_ref).astype(BF16)
        n_chunks = kc.shape[0]
        n_blocks = past_len // SLC_LEN + 1
        r32 = HPG * S_ROWS
        qpos32 = past_len + (_iota((r32, n_chunks), 0) & (S_ROWS - 1))
        cvalid = (_iota((r32, n_chunks), 1) * CMP_STRIDE + (CMP_LEN - 1)) <= qpos32
        nb_pad = agg_ref.shape[1]
        jl = _iota((S_ROWS, nb_pad), 1)
        jlf = jl.astype(F32)
        qpos8 = past_len + _iota((S_ROWS, nb_pad), 0)
        lane8 = _iota((S_ROWS, LANES), 1)
        for g in range(N_KV):
            qg = _group_q(q_ref, g).astype(BF16)
            p_c = _masked_softmax(_dot_nt(qg, kc), cvalid)
            oc_ref[g] = _group_out(_dot(p_c.astype(BF16), vc), g)
            psum = (p_c[0:S_ROWS] + p_c[S_ROWS:2 * S_ROWS] + p_c[2 * S_ROWS:3 * S_ROWS]
                    + p_c[3 * S_ROWS:4 * S_ROWS])
            imp = _dot(psum.astype(BF16), agg_ref[...])
            score = _block_scores(imp, jl, qpos8, n_blocks)
            idx = jnp.zeros((S_ROWS, LANES), F32)
            for k in range(TOP_N):
                mx = jnp.max(score, axis=-1, keepdims=True)
                am = jnp.min(jnp.where(score == mx, jlf, float(nb_pad)), axis=-1, keepdims=True)
                idx = jnp.where(lane8 == k, am, idx)
                score = jnp.where(jlf == am, -3.0, score)
            idx_ref[g] = idx.astype(jnp.int32)


def _scmp(page_table, kpool, vpool, qp_s, cmp_w, agg_s, past_len):
    n_batch, n_pages = page_table.shape
    p = PAGES_PER_STEP
    n_steps = n_pages // p
    n_chunks = n_pages * (PAGE_SIZE // CMP_STRIDE)

    def page_spec(i):
        return pl.BlockSpec((None, PAGE_SIZE, KV_W), lambda b, s, pt, i=i: (pt[b, s * p + i], 0, 0))

    def cspec(shape):
        nd = len(shape)
        return pl.BlockSpec(shape, lambda b, s, pt: (0,) * nd)

    r_out = jnp.arange(PAGE_SIZE)
    chunks_page = PAGE_SIZE // CMP_STRIDE
    src = (r_out % chunks_page) * CMP_STRIDE + r_out // chunks_page
    perm = (src[:, None] == jnp.arange(PAGE_SIZE)[None, :]).astype(BF16)

    page_specs = [page_spec(i) for i in range(p)]
    in_specs = (page_specs + page_specs
                + [pl.BlockSpec((None, N_HEADS, S_ROWS, LANES), lambda b, s, pt: (0, 0, b, 0))]
                + [cspec(perm.shape)] + [cspec(w.shape) for w in cmp_w] + [cspec(agg_s.shape)])
    out_specs = [pl.BlockSpec((None, N_KV, HPG * S_ROWS, LANES), lambda b, s, pt: (b, 0, 0, 0)),
                 pl.BlockSpec((None, N_KV, S_ROWS, LANES), lambda b, s, pt: (b, 0, 0, 0))]
    out_shape = [jax.ShapeDtypeStruct((n_batch, N_KV, HPG * S_ROWS, LANES), F32),
                 jax.ShapeDtypeStruct((n_batch, N_KV, S_ROWS, LANES), jnp.int32)]
    return pl.pallas_call(
        functools.partial(_scmp_kernel, n_pages_step=p, n_steps=n_steps, past_len=past_len),
        grid_spec=pltpu.PrefetchScalarGridSpec(
            num_scalar_prefetch=1, grid=(n_batch, n_steps), in_specs=in_specs, out_specs=out_specs,
            scratch_shapes=[pltpu.VMEM((n_chunks, 2 * KV_W), F32)] * 2),
        out_shape=out_shape,
        compiler_params=pltpu.CompilerParams(dimension_semantics=("arbitrary", "arbitrary"),
                                             vmem_limit_bytes=VMEM_LIMIT),
        name="sample_cmp",
    )(page_table, *([kpool] * p), *([vpool] * p), qp_s, perm, *cmp_w, agg_s)


def _ssel_kernel(pt_ref, idx_ref, kpool_ref, vpool_ref, q_ref, ksn_ref, vsn_ref, kwn_ref, vwn_ref,
                 kwb_ref, vwb_ref, gn_ref, oc_ref, o_ref, kbuf_ref, vbuf_ref, sem_ref,
                 *, n_new, past_len):
    b = pl.program_id(0)
    n_pool_blocks = past_len // SLC_LEN
    blocks_page = PAGE_SIZE // SLC_LEN

    def block_copy(pool_ref, buf_ref, sem, g, qi, k):
        j = jnp.minimum(idx_ref[b, (g * n_new + qi) * TOP_N + k], n_pool_blocks - 1)
        page = pt_ref[b, j // blocks_page]
        return pltpu.make_async_copy(pool_ref.at[page, pl.ds(g * HEAD_DIM, HEAD_DIM), :],
                                     buf_ref.at[g * n_new + qi, :, pl.ds(k * PAGE_SIZE, PAGE_SIZE)], sem)

    slots = [(g, qi, k) for g in range(N_KV) for qi in range(n_new) for k in range(TOP_N)]
    for g, qi, k in slots:
        block_copy(kpool_ref, kbuf_ref, sem_ref.at[0], g, qi, k).start()
        block_copy(vpool_ref, vbuf_ref, sem_ref.at[1], g, qi, k).start()

    r32 = HPG * S_ROWS
    qi32 = _iota((r32, LANES), 0) & (S_ROWS - 1)
    n_sel = TOP_N * PAGE_SIZE
    half_id = _iota((r32, n_sel), 1) >> 6
    new_rows = LANES
    newcol = _iota((r32, new_rows), 1)
    qrow_new = _iota((r32, new_rows), 0) & (S_ROWS - 1)

    def pad_new(ref):
        return jnp.concatenate([ref[...], jnp.zeros((new_rows - S_ROWS, KV_W), F32)], axis=0).astype(BF16)

    ks_new, vs_new, kw_new, vw_new = pad_new(ksn_ref), pad_new(vsn_ref), pad_new(kwn_ref), pad_new(vwn_ref)

    w_buf = kwb_ref.shape[1]
    wpos = jnp.concatenate([past_len - w_buf + _iota((r32, w_buf), 1), past_len + newcol], axis=1)
    qpos_w = past_len + (_iota((r32, w_buf + new_rows), 0) & (S_ROWS - 1))
    wmask = (wpos <= qpos_w) & (wpos >= qpos_w - WINDOW)
    wmask = wmask & jnp.concatenate([jnp.full((r32, w_buf), True), newcol < n_new], axis=1)
    kw_t = kwb_ref[...].astype(BF16)
    vw_t = vwb_ref[...].astype(BF16)
    gates = _sigmoid(gn_ref[...])

    o_w = []
    q_g = []
    for g in range(N_KV):
        qg = _group_q(q_ref, g).astype(BF16)
        q_g.append(qg)
        s_w = jnp.concatenate([_dot(qg, kw_t), _dot_nt(qg, kw_new)], axis=1)
        p_w = _masked_softmax(s_w, wmask).astype(BF16)
        o_w.append(_group_out(_dot_nt(p_w[:, :w_buf], vw_t) + _dot(p_w[:, w_buf:], vw_new), g))

    for g, qi, k in slots:
        block_copy(kpool_ref, kbuf_ref, sem_ref.at[0], g, qi, k).wait()
        block_copy(vpool_ref, vbuf_ref, sem_ref.at[1], g, qi, k).wait()

    pad64 = jnp.zeros((r32, LANES - HEAD_DIM), F32)
    for g in range(N_KV):
        q64 = q_ref[g * HPG:(g + 1) * HPG].reshape(r32, LANES)[:, :HEAD_DIM]
        o_s = jnp.zeros((r32, LANES), F32)
        for qi in range(n_new):
            smask = jnp.full((r32, n_sel), False)
            n_hit = jnp.int32(0)
            for k in range(TOP_N):
                j = idx_ref[b, (g * n_new + qi) * TOP_N + k]
                smask = smask | (half_id == jnp.where(j < n_pool_blocks, blocks_page * k + j % blocks_page, -1))
                n_hit = n_hit + jnp.where(j == n_pool_blocks, 1, 0)
            nmask = (newcol <= qrow_new) & (newcol < jnp.where(n_hit > 0, n_new, 0))
            s_s = jnp.concatenate([_dot(q64, kbuf_ref[g * n_new + qi].astype(BF16)),
                                   _dot_nt(q_g[g], ks_new)], axis=1)
            p_s = _masked_softmax(s_s, jnp.concatenate([smask, nmask], axis=1)).astype(BF16)
            o_pool = _dot_nt(p_s[:, :n_sel], vbuf_ref[g * n_new + qi].astype(BF16))
            o_qi = jnp.concatenate([o_pool, pad64], axis=1) + _group_out(_dot(p_s[:, n_sel:], vs_new), g)
            o_s = jnp.where(qi32 == qi, o_qi, o_s)
        o_c = oc_ref[g]
        heads = []
        for h in range(HPG):
            col = (g * HPG + h) * 3
            rs = slice(h * S_ROWS, (h + 1) * S_ROWS)
            heads.append(gates[:, col:col + 1] * o_c[rs] + gates[:, col + 1:col + 2] * o_s[rs]
                         + gates[:, col + 2:col + 3] * o_w[g][rs])
        for pr in range(HPG // 2):
            c0 = (g * HPG + 2 * pr) * HEAD_DIM
            o_ref[:, c0:c0 + LANES] = _pair_heads(heads[2 * pr], heads[2 * pr + 1])


def _ssel(page_table, idx, kpool, vpool, qp_s, ks_n, vs_n, kw_n, vw_n, kw_buf, vw_buf, gn_s, oc,
          n_new, past_len):
    n_batch = page_table.shape[0]
    w_buf = kw_buf.shape[2]
    rows8 = pl.BlockSpec((S_ROWS, KV_W), lambda b, pt, ix: (b, 0))
    wbuf_spec = pl.BlockSpec((None, KV_W, w_buf), lambda b, pt, ix: (b, 0, 0))
    in_specs = [pl.BlockSpec(memory_space=pl.ANY), pl.BlockSpec(memory_space=pl.ANY),
                pl.BlockSpec((None, N_HEADS, S_ROWS, LANES), lambda b, pt, ix: (0, 0, b, 0)),
                rows8, rows8, rows8, rows8, wbuf_spec, wbuf_spec,
                pl.BlockSpec((S_ROWS, LANES), lambda b, pt, ix: (b, 0)),
                pl.BlockSpec((None, N_KV, HPG * S_ROWS, LANES), lambda b, pt, ix: (b, 0, 0, 0))]
    return pl.pallas_call(
        functools.partial(_ssel_kernel, n_new=n_new, past_len=past_len),
        grid_spec=pltpu.PrefetchScalarGridSpec(
            num_scalar_prefetch=2, grid=(n_batch,), in_specs=in_specs,
            out_specs=pl.BlockSpec((S_ROWS, NSA_W), lambda b, pt, ix: (b, 0)),
            scratch_shapes=[pltpu.VMEM((N_KV * n_new, HEAD_DIM, TOP_N * PAGE_SIZE), F32)] * 2
            + [pltpu.SemaphoreType.DMA((2,))]),
        out_shape=jax.ShapeDtypeStruct((n_batch * S_ROWS, NSA_W), F32),
        compiler_params=pltpu.CompilerParams(dimension_semantics=("arbitrary",),
                                             vmem_limit_bytes=VMEM_LIMIT),
        name="sample_sel",
    )(page_table, idx, kpool, vpool, qp_s, ks_n, vs_n, kw_n, vw_n, kw_buf, vw_buf, gn_s, oc)


def _prep_w_in(w_in):
    q = w_in[:, :NSA_W].reshape(D_MODEL, N_HEADS, HEAD_DIM)
    q = jnp.pad(q, ((0, 0), (0, 0), (0, LANES - HEAD_DIM))).reshape(D_MODEL, N_HEADS * LANES)
    c = NSA_W
    kv = w_in[:, c:c + 6 * KV_W]
    c += 6 * KV_W
    gn = jnp.pad(w_in[:, c:c + 3 * N_HEADS], ((0, 0), (0, LANES - 3 * N_HEADS)))
    c += 3 * N_HEADS
    rest = w_in[:, c:]
    return jnp.concatenate([q, kv, gn, rest], axis=1).astype(BF16)


def _prep_cmp(pe, w1, w2):
    w1r = w1.reshape(CMP_R, CMP_STRIDE, HEAD_DIM, HEAD_DIM)
    eye = jnp.eye(N_KV, dtype=w1.dtype)
    w1big = jnp.einsum('rjdh,ge->jgdreh', w1r, eye).reshape(CMP_STRIDE * KV_W, CMP_R * KV_W)
    pe_rows = jnp.broadcast_to(pe.reshape(CMP_R, CMP_STRIDE, 1, HEAD_DIM),
                               (CMP_R, CMP_STRIDE, N_KV, HEAD_DIM)).reshape(CMP_R, CMP_STRIDE * KV_W)
    pe_rows = jnp.pad(pe_rows, ((0, SUBLANES - CMP_R), (0, 0)))
    w2big = jnp.einsum('hd,ge->ghed', w2, eye).reshape(KV_W, KV_W)
    return pe_rows.astype(BF16), w1big.astype(BF16), w2big.astype(BF16)


def _agg_matrix(n_rows, n_blocks, n_cols, col0):
    i = jnp.arange(n_rows)[:, None] * CMP_STRIDE
    jj = jnp.arange(n_cols)[None, :] - col0
    hit = (i < jj * SLC_LEN + SLC_LEN) & (i + CMP_LEN > jj * SLC_LEN) & (jj >= 0) & (jj < n_blocks)
    return hit.astype(BF16)


def kernel(x_prompt, x_sample, cache_k_cmp, cache_v_cmp, cache_k_slc, cache_v_slc, cache_k_win, cache_v_win, state_conv, page_table, ln_g, w_in, pe_k, w1_k, w2_k, pe_v, w1_v, w2_v, dw_k, dw_b, cln_g, cln_b, pw_w, pw_b, w_pa, w_pb, w_o, final_g):
    assert w_in.shape[0] == 1, "single-layer stack"
    n_b, t_len, _ = x_prompt.shape
    n_db, n_new, _ = x_sample.shape
    n_pages = page_table.shape[1]
    past_len = n_pages * PAGE_SIZE
    w_buf = cache_k_win.shape[2]
    assert n_new <= S_ROWS and t_len % TQ == 0 and t_len >= WINDOW + TQ
    assert n_new < CMP_STRIDE, "the new rows must not complete a compression chunk"
    assert t_len // SLC_LEN <= LANES - HEAD_DIM and past_len % PAGE_SIZE == 0

    w_all = _prep_w_in(w_in[0])
    lng = ln_g[0].reshape(1, D_MODEL)
    cmp_w = _prep_cmp(pe_k[0], w1_k[0], w2_k[0]) + _prep_cmp(pe_v[0], w1_v[0], w2_v[0])
    tail_w = (dw_b[0].reshape(1, CONV_C), cln_g[0].reshape(1, CONV_C), cln_b[0].reshape(1, CONV_C),
              pw_w[0].astype(BF16), pw_b[0].reshape(1, CONV_C), w_pa[0].astype(BF16), w_pb[0].astype(BF16),
              w_o[0].astype(BF16), final_g.reshape(1, D_MODEL))
    dw = jnp.pad(dw_k[0], ((0, 1), (0, 0)))

    (qt, kct, vct, kst, vst, kwt, vwt, gnt, sga, u, sgb, sma, smb, k_c, v_c, ksa, vst_g, kwp, vwt_g) = _proj(
        x_prompt.reshape(n_b * t_len, D_MODEL), lng, w_all, n_b, t_len, (0, 1), True)
    seq = lambda a: a.reshape(n_b, t_len, a.shape[-1])
    kcm, vcmt = _cmp_prompt(seq(k_c), seq(v_c), cmp_w)
    n_cmp_rows = t_len // CMP_STRIDE
    aggt_p = _agg_matrix(n_cmp_rows, t_len // SLC_LEN, LANES, 0).T
    o_a = _attn_prompt(qt, gnt, kcm, vcmt, ksa, vst_g, kwp, vwt_g, aggt_p)
    y_prompt = _tail_prompt(x_prompt, o_a, seq(sga), seq(u), seq(sgb), seq(sma), seq(smb), dw, tail_w)

    xs = jnp.pad(x_sample, ((0, 0), (0, S_ROWS - n_new), (0, 0))).reshape(n_db * S_ROWS, D_MODEL)
    (qp_s, kct_s, vct_s, kst_s, vst_s, kwt_s, vwt_s, gn_s, sga_s, u_s, sgb_s, sma_s, smb_s,
     k_s_s, v_s_s, k_w_s, v_w_s) = _proj(xs, lng, w_all, 1, n_db * S_ROWS, (2, 3, 4, 5), False)
    pool = lambda c: jnp.transpose(c[0], (0, 2, 3, 1)).reshape(c.shape[1], KV_W, PAGE_SIZE)
    n_blocks_s = past_len // SLC_LEN + 1
    nb_pad = -(-n_blocks_s // LANES) * LANES
    agg_s = _agg_matrix(past_len // CMP_STRIDE, n_blocks_s, nb_pad, 0)
    agg_s = agg_s.at[past_len // CMP_STRIDE - 1].set(0)
    oc_s, idx_s = _scmp(page_table, pool(cache_k_cmp), pool(cache_v_cmp), qp_s, cmp_w, agg_s, past_len)
    idx_flat = idx_s[:, :, :n_new, :TOP_N].reshape(n_db, N_KV * n_new * TOP_N)
    win = lambda c: jnp.transpose(c[0], (0, 2, 3, 1)).reshape(n_db, KV_W, w_buf)
    o_a_s = _ssel(page_table, idx_flat, pool(cache_k_slc), pool(cache_v_slc), qp_s, k_s_s, v_s_s, k_w_s,
                  v_w_s, win(cache_k_win), win(cache_v_win), gn_s, oc_s, n_new, past_len)
    u_new = u_s.reshape(n_db, S_ROWS, CONV_C)[:, :n_new]
    up_s = jnp.concatenate([state_conv[0], u_new], axis=1)
    up_pad = jnp.pad(up_s, ((0, 0), (0, UP_ROWS - up_s.shape[1]), (0, 0)))
    dw_shift = jnp.stack([jnp.pad(dw_k[0], ((t, UP_ROWS - CONV_W - t), (0, 0))) for t in range(n_new)])
    y_s = _tail_sample(xs, o_a_s, sga_s, up_pad, sgb_s, sma_s, smb_s, dw_shift, tail_w, n_new)
    y_sample = y_s.reshape(n_db, S_ROWS, D_MODEL)[:, :n_new]

    def p_rows(at):
        return jnp.transpose(at.reshape(n_b, N_KV, HEAD_DIM, at.shape[-1]), (0, 3, 1, 2))[None]

    def s_rows(at):
        a = at.reshape(N_KV, HEAD_DIM, n_db, S_ROWS)[:, :, :, :n_new]
        return jnp.transpose(a, (2, 3, 0, 1))[None]

    win_p = min(WINDOW, t_len)
    p_conv = seq(u)[:, -(CONV_W - 1):][None]
    s_k_win = jnp.concatenate([cache_k_win[0], s_rows(kwt_s)[0]], axis=1)[:, -w_buf:][None]
    s_v_win = jnp.concatenate([cache_v_win[0], s_rows(vwt_s)[0]], axis=1)[:, -w_buf:][None]
    s_conv = up_s[:, -(CONV_W - 1):][None]
    return (y_prompt, y_sample, p_rows(kct), p_rows(vct), p_rows(kst), p_rows(vst),
            p_rows(kwt[:, :, -win_p:]), p_rows(vwt[:, :, -win_p:]), p_conv,
            s_rows(kct_s), s_rows(vct_s), s_rows(kst_s), s_rows(vst_s), s_k_win, s_v_win, s_conv)
```

```python
import functools

import jax
import jax.numpy as jnp
from jax import lax
from jax.experimental import pallas as pl
from jax.experimental.pallas import tpu as pltpu

F32 = jnp.float32
BF16 = jnp.bfloat16

D_MODEL = 1024
N_HEADS = 8
HEAD_DIM = 64
N_KV = 2
HPG = N_HEADS // N_KV
NSA_W = N_HEADS * HEAD_DIM
KV_W = N_KV * HEAD_DIM
CMP_LEN = 32
CMP_STRIDE = 16
CMP_R = CMP_LEN // CMP_STRIDE
SLC_LEN = 64
TOP_N = 16
WINDOW = 512
CONV_C = 512
CONV_W = 31
PAGE_SIZE = 128
EPS = 1e-6
FORCED_SCORE = 1e4
NEG_INF = -1e30
MASK_BIAS = -1e4

LANES = 128
SUBLANES = 8
VMEM_LIMIT = 56 * 1024 * 1024

C_Q = 0
C_KV = C_Q + N_HEADS * LANES
C_GN = C_KV + 6 * KV_W
C_GA = C_GN + LANES
C_GLU = C_GA + NSA_W
C_GB = C_GLU + 2 * CONV_C
C_MRG = C_GB + CONV_C
C_END = C_MRG + 2 * D_MODEL

TM_PROJ = 256
TQ = 256
TM_TAIL = 256
PAGES_PER_STEP = 32


def _sigmoid(x):
    return jax.nn.sigmoid(x)


def _silu(x):
    return jax.nn.silu(x)


def _dot(a, b):
    return jnp.dot(a, b, preferred_element_type=F32)


def _dot_nt(a, b):
    return lax.dot_general(a, b, (((1,), (1,)), ((), ())), preferred_element_type=F32)


def _iota(shape, dim):
    return lax.broadcasted_iota(jnp.int32, shape, dim)


def _const_spec(shape):
    nd = len(shape)
    return pl.BlockSpec(shape, lambda *_: (0,) * nd)


def _proj_kernel(x_ref, lng_ref, w_ref, qp_ref, kct_ref, vct_ref, kst_ref, vst_ref, kwt_ref, vwt_ref,
                 gn_ref, sga_ref, u_ref, sgb_ref, sma_ref, smb_ref, *more_refs, tiles_per_batch, row_ids,
                 attn_ops):
    x = x_ref[...]
    tm = x.shape[0]
    xn = x * lax.rsqrt(jnp.mean(x * x, axis=-1, keepdims=True) + EPS) * lng_ref[...]
    xn = xn.astype(BF16)

    zq = _dot(xn, w_ref[:, C_Q:C_KV]) * (HEAD_DIM ** -0.5)
    zgn = _dot(xn, w_ref[:, C_GN:C_GA])
    for h in range(N_HEADS):
        qh = zq[:, h * LANES:(h + 1) * LANES]
        qp_ref[h] = (qh.T if attn_ops else qh).astype(BF16)
    gn_ref[...] = zgn.T if attn_ops else zgn

    zkv = _dot(xn, w_ref[:, C_KV:C_GN])
    kv = [zkv[:, i * KV_W:(i + 1) * KV_W] for i in range(6)]
    kvt = [v.T for v in kv]
    for r, v in zip((kct_ref, vct_ref, kst_ref, vst_ref, kwt_ref, vwt_ref), kvt):
        r[...] = v
    for r, i in zip(more_refs[:len(row_ids)], row_ids):
        r[...] = kv[i]

    if attn_ops:
        ksa_ref, vst_g_ref, kwp_ref, vwt_g_ref = more_refs[len(row_ids):]
        t0 = (pl.program_id(0) % tiles_per_batch) * tm
        lane = _iota((tm, LANES), 1)
        blk = (t0 + _iota((tm, LANES), 0)) >> 6
        onehot = jnp.where((lane - HEAD_DIM) == blk, 1.0, 0.0)
        ones_row = jnp.where(_iota((HEAD_DIM, tm), 0) == 0, 1.0, 0.0)
        for g in range(N_KV):
            rows_g = slice(g * HEAD_DIM, (g + 1) * HEAD_DIM)

            def grp(v):
                return v if g == 0 else pltpu.roll(v, HEAD_DIM, axis=1)

            ksa_ref[g] = jnp.where(lane < HEAD_DIM, grp(kv[2]), onehot).astype(BF16)
            kwp_ref[g] = grp(kv[4]).astype(BF16)
            for ref, vt in ((vst_g_ref, kvt[3]), (vwt_g_ref, kvt[5])):
                v_aug = jnp.concatenate([vt[rows_g], ones_row], axis=0).astype(BF16)
                for j in range(tm // KEY_CHUNK):
                    ref[g, j] = v_aug[:, j * KEY_CHUNK:(j + 1) * KEY_CHUNK]

    sga_ref[...] = _silu(_dot(xn, w_ref[:, C_GA:C_GLU]))
    zglu = _dot(xn, w_ref[:, C_GLU:C_GB])
    u_ref[...] = zglu[:, :CONV_C] * _sigmoid(zglu[:, CONV_C:])
    sgb_ref[...] = _silu(_dot(xn, w_ref[:, C_GB:C_MRG]))
    zm = _dot(xn, w_ref[:, C_MRG:C_END])
    sma_ref[...] = _sigmoid(zm[:, :D_MODEL])
    smb_ref[...] = _sigmoid(zm[:, D_MODEL:])


def _proj(x2d, ln_g, w_all, n_batch, t_len, row_ids, attn_ops):
    m = x2d.shape[0]
    tm = min(TM_PROJ, t_len)
    tpb = t_len // tm
    grid = (m // tm,)

    def row_spec(width):
        return pl.BlockSpec((tm, width), lambda i: (i, 0))

    def bt_spec(lead, width):
        return pl.BlockSpec((None, lead, tm, width), lambda i: (i // tpb, 0, i % tpb, 0))

    t_spec = pl.BlockSpec((None, KV_W, tm), lambda i: (i // tpb, 0, i % tpb))
    if attn_ops:
        out_shape = [jax.ShapeDtypeStruct((n_batch, N_HEADS, LANES, t_len), BF16)]
        out_specs = [pl.BlockSpec((None, N_HEADS, LANES, tm), lambda i: (i // tpb, 0, 0, i % tpb))]
    else:
        out_shape = [jax.ShapeDtypeStruct((n_batch, N_HEADS, t_len, LANES), BF16)]
        out_specs = [bt_spec(N_HEADS, LANES)]
    out_shape += [jax.ShapeDtypeStruct((n_batch, KV_W, t_len), F32)] * 6
    out_specs += [t_spec] * 6
    if attn_ops:
        out_shape.append(jax.ShapeDtypeStruct((n_batch, LANES, t_len), F32))
        out_specs.append(t_spec)
    else:
        out_shape.append(jax.ShapeDtypeStruct((m, LANES), F32))
        out_specs.append(row_spec(LANES))
    for width in (NSA_W, CONV_C, CONV_C, D_MODEL, D_MODEL):
        out_shape.append(jax.ShapeDtypeStruct((m, width), F32))
        out_specs.append(row_spec(width))
    out_shape += [jax.ShapeDtypeStruct((m, KV_W), F32)] * len(row_ids)
    out_specs += [row_spec(KV_W)] * len(row_ids)
    if attn_ops:
        cpt = tm // KEY_CHUNK
        k_shape = jax.ShapeDtypeStruct((n_batch, N_KV, t_len, LANES), BF16)
        vt_shape = jax.ShapeDtypeStruct((n_batch, N_KV, t_len // KEY_CHUNK, LANES, KEY_CHUNK), BF16)
        vt_spec = pl.BlockSpec((None, N_KV, cpt, LANES, KEY_CHUNK), lambda i: (i // tpb, 0, i % tpb, 0, 0))
        out_shape += [k_shape, vt_shape, k_shape, vt_shape]
        out_specs += [bt_spec(N_KV, LANES), vt_spec, bt_spec(N_KV, LANES), vt_spec]

    return pl.pallas_call(
        functools.partial(_proj_kernel, tiles_per_batch=tpb, row_ids=row_ids, attn_ops=attn_ops),
        grid=grid,
        in_specs=[row_spec(D_MODEL), _const_spec((1, D_MODEL)), _const_spec((D_MODEL, C_END))],
        out_specs=out_specs,
        out_shape=out_shape,
        compiler_params=pltpu.CompilerParams(dimension_semantics=("arbitrary",),
                                             vmem_limit_bytes=VMEM_LIMIT),
        name="proj",
    )(x2d, ln_g, w_all)


def _compress_from_partials(per, pe_ref, w1_ref, w2_ref):
    n = per.shape[0]
    hp = _dot(pe_ref[...], w1_ref[...])
    h0 = hp[0:1, :KV_W] + hp[1:2, KV_W:]
    h = h0 + per[:, :KV_W] + pltpu.roll(per[:, KV_W:], n - 1, axis=0)
    return _dot(_silu(h).astype(BF16), w2_ref[...])


def _chunk_rows(ref, n_chunks):
    return jnp.concatenate([ref[pl.ds(j, n_chunks, stride=CMP_STRIDE), :] for j in range(CMP_STRIDE)],
                           axis=-1)


def _cmp_prompt_kernel(kc_ref, vc_ref, pek_ref, w1k_ref, w2k_ref, pev_ref, w1v_ref, w2v_ref,
                       ok_ref, ov_ref):
    n_chunks = kc_ref.shape[0] // CMP_STRIDE
    for x_ref, pe_ref, w1_ref, w2_ref, o_ref, transposed in (
            (kc_ref, pek_ref, w1k_ref, w2k_ref, ok_ref, False), (vc_ref, pev_ref, w1v_ref, w2v_ref, ov_ref, True)):
        per = _dot(_chunk_rows(x_ref, n_chunks).astype(BF16), w1_ref[...])
        c = _compress_from_partials(per, pe_ref, w1_ref, w2_ref)
        for g, cg in enumerate((c, pltpu.roll(c, HEAD_DIM, axis=1))):
            o_ref[g] = (cg.T if transposed else cg).astype(BF16)


def _cmp_prompt(k_c, v_c, cmp_w):
    n_batch, t_len, _ = k_c.shape
    n_chunks = t_len // CMP_STRIDE
    seq = pl.BlockSpec((None, t_len, KV_W), lambda b: (b, 0, 0))
    out = pl.BlockSpec((None, N_KV, n_chunks, KV_W), lambda b: (b, 0, 0, 0))
    w_specs = [_const_spec(w.shape) for w in cmp_w]
    return pl.pallas_call(
        _cmp_prompt_kernel,
        grid=(n_batch,),
        in_specs=[seq, seq] + w_specs,
        out_specs=[out, out],
        out_shape=[jax.ShapeDtypeStruct((n_batch, N_KV, n_chunks, KV_W), BF16)] * 2,
        compiler_params=pltpu.CompilerParams(dimension_semantics=("arbitrary",),
                                             vmem_limit_bytes=VMEM_LIMIT),
        name="cmp_prompt",
    )(k_c, v_c, *cmp_w)


def _masked_softmax(s, mask):
    s = jnp.where(mask, s, NEG_INF)
    e = jnp.where(mask, jnp.exp(s - jnp.max(s, axis=-1, keepdims=True)), 0.0)
    return e / jnp.maximum(jnp.sum(e, axis=-1, keepdims=True), 1e-30)


def _softmax_rows(s, mask):
    s = jnp.where(mask, s, NEG_INF)
    e = jnp.where(mask, jnp.exp(s - jnp.max(s, axis=0, keepdims=True)), 0.0)
    return e / jnp.maximum(jnp.sum(e, axis=0, keepdims=True), 1e-30)


def _block_scores(imp, j, qpos, n_blocks):
    cur = qpos >> 6
    forced = (j == 0) | (j == cur) | (j == cur - 1)
    score = jnp.where(forced, FORCED_SCORE, imp)
    score = jnp.where(j * SLC_LEN <= qpos, score, -1.0)
    return jnp.where((j >= 0) & (j < n_blocks), score, -2.0)


def _pair_heads(o_even, o_odd):
    lane = _iota(o_even.shape, 1)
    return jnp.where(lane < HEAD_DIM, o_even, pltpu.roll(o_odd, HEAD_DIM, axis=1))


KEY_CHUNK = 128
SEL_ROWS = 32


def _attn_prompt_kernel(qt_ref, gnt_ref, kcm_ref, vcmt_ref, ksa_ref, vst_ref, kwp_ref, vwt_ref, aggt_ref,
                        o_ref, oacc_ref, qsel_ref, m_ref, acc_ref, *, t_len):
    tq = gnt_ref.shape[1]
    n_blocks = t_len // SLC_LEN
    i = pl.program_id(1)
    t0 = i * tq
    chunks_tile = tq // KEY_CHUNK

    gates = _sigmoid(gnt_ref[...])
    qpos_c = t0 + _iota((KV_W, tq), 1)
    cvalid = (_iota((KV_W, tq), 0) * CMP_STRIDE + (CMP_LEN - 1)) <= qpos_c
    jblk = _iota((SEL_ROWS, tq), 0)
    qpos_j = t0 + _iota((SEL_ROWS, tq), 1)
    row_minus_lane = _iota((KEY_CHUNK, tq), 0) - _iota((KEY_CHUNK, tq), 1)

    def gate(hh, branch):
        r = hh * 3 + branch
        return gates[r:r + 1, :]

    def flash(q_ref, k_ref, vt_ref, c_lo, banded, branch):
        def step(c, hh, diagonal):
            g = hh // HPG
            kc = k_ref[g, pl.ds(pl.multiple_of(c * KEY_CHUNK, KEY_CHUNK), KEY_CHUNK), :]
            s = _dot(kc, q_ref[hh])
            off = c * KEY_CHUNK - t0
            if diagonal:
                s = jnp.where(row_minus_lane + off <= 0, s, NEG_INF)
            elif banded:
                s = jnp.where(row_minus_lane + off >= -WINDOW, s, NEG_INF)
            m = m_ref[hh]
            m_new = jnp.maximum(m, jnp.max(s, axis=0, keepdims=True))
            p = jnp.exp(s - m_new).astype(BF16)
            acc_ref[hh] = jnp.exp(m - m_new) * acc_ref[hh] + _dot(vt_ref[g, c], p)
            m_ref[hh] = m_new

        m_ref[...] = jnp.full(m_ref.shape, NEG_INF, F32)
        acc_ref[...] = jnp.zeros(acc_ref.shape, F32)

        def body(ct, carry):
            for cd in range(chunks_tile):
                for hh in range(N_HEADS):
                    step(ct * chunks_tile + cd, hh, False)
            return carry

        lax.fori_loop(c_lo // chunks_tile, i, body, 0)
        for cd in range(chunks_tile):
            for hh in range(N_HEADS):
                step(i * chunks_tile + cd, hh, True)
        for hh in range(N_HEADS):
            acc = acc_ref[hh]
            oacc_ref[hh] = oacc_ref[hh] + gate(hh, branch) * (acc / acc[HEAD_DIM:HEAD_DIM + 1, :])

    for g in range(N_KV):
        psum = None
        for h in range(HPG):
            hh = g * HPG + h
            p_c = _softmax_rows(_dot(kcm_ref[g], qt_ref[hh]), cvalid)
            oacc_ref[hh] = gate(hh, 0) * _dot(vcmt_ref[g], p_c.astype(BF16))
            psum = p_c if psum is None else psum + p_c
        imp = _dot(aggt_ref[...], psum.astype(BF16))[:SEL_ROWS]
        score = _block_scores(imp, jblk, qpos_j, n_blocks)
        rank = jnp.zeros((SEL_ROWS, tq), jnp.int32)
        for jp in range(n_blocks):
            other = score[jp:jp + 1, :]
            ahead = (other > score) | ((other == score) & (jblk > jp))
            rank = rank + jnp.where(ahead, 1, 0)
        allowed = (rank < TOP_N) & (jblk < n_blocks) & (jblk * SLC_LEN <= qpos_j)
        bias = jnp.where(allowed, 0.0, MASK_BIAS).astype(BF16)
        pad = jnp.zeros((LANES - HEAD_DIM - SEL_ROWS, tq), BF16)
        for h in range(HPG):
            qsel_ref[g * HPG + h] = jnp.concatenate([qt_ref[g * HPG + h, 0:HEAD_DIM, :], bias, pad], axis=0)

    flash(qsel_ref, ksa_ref, vst_ref, 0, False, 1)
    flash(qt_ref, kwp_ref, vwt_ref, jnp.maximum(i * chunks_tile - WINDOW // KEY_CHUNK, 0), True, 2)
    for pr in range(N_HEADS // 2):
        two = jnp.concatenate([oacc_ref[2 * pr, 0:HEAD_DIM, :], oacc_ref[2 * pr + 1, 0:HEAD_DIM, :]], axis=0)
        o_ref[:, pr * LANES:(pr + 1) * LANES] = two.T


def _attn_prompt(qt, gnt, kcm, vcmt, ksa, vst, kwp, vwt, aggt):
    n_batch, _, _, t_len = qt.shape
    tq = TQ
    n_cmp = kcm.shape[2]
    assert tq % KEY_CHUNK == 0 and WINDOW % KEY_CHUNK == 0 and t_len // SLC_LEN <= SEL_ROWS
    assert n_cmp == KV_W, "one lane tile of compressed blocks"
    k_spec = pl.BlockSpec((None, N_KV, t_len, LANES), lambda b, i: (b, 0, 0, 0))
    vt_spec = pl.BlockSpec((None, N_KV, t_len // KEY_CHUNK, LANES, KEY_CHUNK), lambda b, i: (b, 0, 0, 0, 0))
    cmp_spec = pl.BlockSpec((None, N_KV, n_cmp, KV_W), lambda b, i: (b, 0, 0, 0))
    return pl.pallas_call(
        functools.partial(_attn_prompt_kernel, t_len=t_len),
        grid=(n_batch, t_len // tq),
        in_specs=[pl.BlockSpec((None, N_HEADS, LANES, tq), lambda b, i: (b, 0, 0, i)),
                  pl.BlockSpec((None, LANES, tq), lambda b, i: (b, 0, i)),
                  cmp_spec, cmp_spec, k_spec, vt_spec, k_spec, vt_spec, _const_spec(aggt.shape)],
        out_specs=pl.BlockSpec((None, tq, NSA_W), lambda b, i: (b, i, 0)),
        out_shape=jax.ShapeDtypeStruct((n_batch, t_len, NSA_W), F32),
        scratch_shapes=[pltpu.VMEM((N_HEADS, LANES, tq), F32), pltpu.VMEM((N_HEADS, LANES, tq), BF16),
                        pltpu.VMEM((N_HEADS, 1, tq), F32), pltpu.VMEM((N_HEADS, LANES, tq), F32)],
        compiler_params=pltpu.CompilerParams(dimension_semantics=("arbitrary", "arbitrary"),
                                             vmem_limit_bytes=VMEM_LIMIT),
        name="attn_prompt",
    )(qt, gnt, kcm, vcmt, ksa, vst, kwp, vwt, aggt)


def _tail_math(x, o_a, sga, c_pre, sgb, sma, smb, w):
    (dwb_ref, clg_ref, clb_ref, pww_ref, pwb_ref, wpa_ref, wpb_ref, wo_ref, fg_ref) = w
    c = c_pre + dwb_ref[...]
    mu = jnp.mean(c, axis=-1, keepdims=True)
    var = jnp.mean(jnp.square(c - mu), axis=-1, keepdims=True)
    cn = (c - mu) * lax.rsqrt(var + EPS) * clg_ref[...] + clb_ref[...]
    cp = _dot(_silu(cn).astype(BF16), pww_ref[...]) + pwb_ref[...]
    br_a = _dot((o_a * sga).astype(BF16), wpa_ref[...])
    br_b = _dot((cp * sgb).astype(BF16), wpb_ref[...])
    h = sma * br_a + smb * br_b
    xo = x + _dot(h.astype(BF16), wo_ref[...])
    return xo * lax.rsqrt(jnp.mean(xo * xo, axis=-1, keepdims=True) + EPS) * fg_ref[...]


HIST = 32


def _tail_prompt_kernel(x_ref, oa_ref, sga_ref, u_ref, sgb_ref, sma_ref, smb_ref, dw_ref, *rest):
    w, (y_ref, up_ref, sh_ref) = rest[:9], rest[9:]
    tm = x_ref.shape[0]

    @pl.when(pl.program_id(1) == 0)
    def _():
        up_ref[0:HIST, :] = jnp.zeros((HIST, CONV_C), F32)

    up_ref[HIST:HIST + tm, :] = u_ref[...]
    first = HIST - (CONV_W - 1)
    n_sh = HIST + tm - SUBLANES
    acc = jnp.zeros((tm, CONV_C), F32)
    for phase in range(SUBLANES):
        tiles = [(first + k) // SUBLANES for k in range(CONV_W) if (first + k) % SUBLANES == phase]
        if not tiles:
            continue
        if phase:
            sh_ref[phase, 0:n_sh, :] = up_ref[pl.ds(phase, n_sh), :]
        for a in tiles:
            k = a * SUBLANES + phase - first
            rows = up_ref[pl.ds(a * SUBLANES, tm), :] if phase == 0 else sh_ref[phase, pl.ds(a * SUBLANES, tm), :]
            acc = acc + rows * dw_ref[k:k + 1, :]
    up_ref[0:HIST, :] = up_ref[tm:tm + HIST, :]
    y_ref[...] = _tail_math(x_ref[...], oa_ref[...], sga_ref[...], acc, sgb_ref[...], sma_ref[...],
                            smb_ref[...], w)


def _tail_prompt(x, o_a, sga, u, sgb, sma, smb, dw, tail_w):
    n_batch, t_len, _ = x.shape
    tm = TM_TAIL

    def bt(width):
        return pl.BlockSpec((None, tm, width), lambda b, t: (b, t, 0))

    ins = [x, o_a, sga, u, sgb, sma, smb]
    return pl.pallas_call(
        _tail_prompt_kernel,
        grid=(n_batch, t_len // tm),
        in_specs=[bt(a.shape[-1]) for a in ins] + [_const_spec(dw.shape)] + [_const_spec(a.shape) for a in tail_w],
        out_specs=bt(D_MODEL),
        out_shape=jax.ShapeDtypeStruct((n_batch, t_len, D_MODEL), F32),
        scratch_shapes=[pltpu.VMEM((HIST + tm, CONV_C), F32), pltpu.VMEM((SUBLANES, HIST + tm, CONV_C), F32)],
        compiler_params=pltpu.CompilerParams(dimension_semantics=("arbitrary", "arbitrary"),
                                             vmem_limit_bytes=VMEM_LIMIT),
        name="tail_prompt",
    )(*ins, dw, *tail_w)


S_ROWS = 8
UP_ROWS = 40


def _tail_sample_kernel(x_ref, oa_ref, sga_ref, ups_ref, sgb_ref, sma_ref, smb_ref, dws_ref, *rest,
                        n_new):
    w, (y_ref, c_ref) = rest[:9], rest[9:]
    n_batch = ups_ref.shape[0]
    c_ref[...] = jnp.zeros(c_ref.shape, F32)

    def body(b, carry):
        up = ups_ref[b]
        for t in range(n_new):
            c_ref[pl.ds(b * S_ROWS + t, 1), :] = jnp.sum(up * dws_ref[t], axis=0, keepdims=True)
        return carry

    lax.fori_loop(0, n_batch, body, 0)
    y_ref[...] = _tail_math(x_ref[...], oa_ref[...], sga_ref[...], c_ref[...], sgb_ref[...], sma_ref[...],
                            smb_ref[...], w)


def _tail_sample(x, o_a, sga, up_s, sgb, sma, smb, dw_shift, tail_w, n_new):
    ins = [x, o_a, sga, up_s, sgb, sma, smb, dw_shift] + list(tail_w)
    m = x.shape[0]
    return pl.pallas_call(
        functools.partial(_tail_sample_kernel, n_new=n_new),
        grid=(1,),
        in_specs=[_const_spec(a.shape) for a in ins],
        out_specs=_const_spec((m, D_MODEL)),
        out_shape=jax.ShapeDtypeStruct((m, D_MODEL), F32),
        scratch_shapes=[pltpu.VMEM((m, CONV_C), F32)],
        compiler_params=pltpu.CompilerParams(dimension_semantics=("arbitrary",),
                                             vmem_limit_bytes=VMEM_LIMIT),
        name="tail_sample",
    )(*ins)


def _group_q(q_ref, g):
    q = q_ref[g * HPG:(g + 1) * HPG].reshape(HPG * S_ROWS, LANES).astype(F32)
    return q if g == 0 else pltpu.roll(q, HEAD_DIM, axis=1)


def _group_out(o, g):
    return o if g == 0 else pltpu.roll(o, HEAD_DIM, axis=1)


def _scmp_kernel(pt_ref, *refs, n_pages_step, n_steps, past_len):
    p = n_pages_step
    kpages, vpages = refs[:p], refs[p:2 * p]
    (q_ref, perm_ref, pek_ref, w1k_ref, w2k_ref, pev_ref, w1v_ref, w2v_ref, agg_ref,
     oc_ref, score_ref, ak_ref, av_ref) = refs[2 * p:]
    pg = pl.program_id(1)
    chunks_page = PAGE_SIZE // CMP_STRIDE
    rows = chunks_page * p

    for pages, w1_ref, a_ref in ((kpages, w1k_ref, ak_ref), (vpages, w1v_ref, av_ref)):
        blocks = []
        for pr in pages:
            y = _dot_nt(perm_ref[...], pr[...].astype(BF16))
            blocks.append(jnp.concatenate(
                [y[j * chunks_page:(j + 1) * chunks_page] for j in range(CMP_STRIDE)], axis=-1))
        xc = jnp.concatenate(blocks, axis=0).astype(BF16)
        a_ref[pl.ds(pl.multiple_of(pg * rows, rows), rows), :] = _dot(xc, w1_ref[...])

    @pl.when(pg == n_steps - 1)
    def _():
        kc = _compress_from_partials(ak_ref[...], pek_ref, w1k_ref, w2k_ref).astype(BF16)
        vc = _compress_from_partials(av_ref[...], pev_ref, w1v_ref, w2v_ref).astype(BF16)
        n_chunks = kc.shape[0]
        n_blocks = past_len // SLC_LEN + 1
        r32 = HPG * S_ROWS
        qpos32 = past_len + (_iota((r32, n_chunks), 0) & (S_ROWS - 1))
        cvalid = (_iota((r32, n_chunks), 1) * CMP_STRIDE + (CMP_LEN - 1)) <= qpos32
        nb_pad = agg_ref.shape[1]
        jl = _iota((S_ROWS, nb_pad), 1)
        qpos8 = past_len + _iota((S_ROWS, nb_pad), 0)
        for g in range(N_KV):
            qg = _group_q(q_ref, g).astype(BF16)
            p_c = _masked_softmax(_dot_nt(qg, kc), cvalid)
            oc_ref[g] = _group_out(_dot(p_c.astype(BF16), vc), g)
            psum = (p_c[0:S_ROWS] + p_c[S_ROWS:2 * S_ROWS] + p_c[2 * S_ROWS:3 * S_ROWS]
                    + p_c[3 * S_ROWS:4 * S_ROWS])
            imp = _dot(psum.astype(BF16), agg_ref[...])
            score_ref[g] = _block_scores(imp, jl, qpos8, n_blocks)


def _scmp(page_table, kpool, vpool, qp_s, cmp_w, agg_s, past_len):
    n_batch, n_pages = page_table.shape
    p = PAGES_PER_STEP
    n_steps = n_pages // p
    n_chunks = n_pages * (PAGE_SIZE // CMP_STRIDE)

    def page_spec(i):
        return pl.BlockSpec((None, PAGE_SIZE, KV_W), lambda b, s, pt, i=i: (pt[b, s * p + i], 0, 0))

    def cspec(shape):
        nd = len(shape)
        return pl.BlockSpec(shape, lambda b, s, pt: (0,) * nd)

    r_out = jnp.arange(PAGE_SIZE)
    chunks_page = PAGE_SIZE // CMP_STRIDE
    src = (r_out % chunks_page) * CMP_STRIDE + r_out // chunks_page
    perm = (src[:, None] == jnp.arange(PAGE_SIZE)[None, :]).astype(BF16)

    page_specs = [page_spec(i) for i in range(p)]
    in_specs = (page_specs + page_specs
                + [pl.BlockSpec((None, N_HEADS, S_ROWS, LANES), lambda b, s, pt: (0, 0, b, 0))]
                + [cspec(perm.shape)] + [cspec(w.shape) for w in cmp_w] + [cspec(agg_s.shape)])
    nb_pad = agg_s.shape[1]
    out_specs = [pl.BlockSpec((None, N_KV, HPG * S_ROWS, LANES), lambda b, s, pt: (b, 0, 0, 0)),
                 pl.BlockSpec((None, N_KV, S_ROWS, nb_pad), lambda b, s, pt: (b, 0, 0, 0))]
    out_shape = [jax.ShapeDtypeStruct((n_batch, N_KV, HPG * S_ROWS, LANES), F32),
                 jax.ShapeDtypeStruct((n_batch, N_KV, S_ROWS, nb_pad), F32)]
    oc, score = pl.pallas_call(
        functools.partial(_scmp_kernel, n_pages_step=p, n_steps=n_steps, past_len=past_len),
        grid_spec=pltpu.PrefetchScalarGridSpec(
            num_scalar_prefetch=1, grid=(n_batch, n_steps), in_specs=in_specs, out_specs=out_specs,
            scratch_shapes=[pltpu.VMEM((n_chunks, 2 * KV_W), F32)] * 2),
        out_shape=out_shape,
        compiler_params=pltpu.CompilerParams(dimension_semantics=("arbitrary", "arbitrary"),
                                             vmem_limit_bytes=VMEM_LIMIT),
        name="sample_cmp",
    )(page_table, *([kpool] * p), *([vpool] * p), qp_s, perm, *cmp_w, agg_s)
    return oc, _top_blocks(score.reshape(n_batch * N_KV * S_ROWS, nb_pad))


def _top_blocks_kernel(score_ref, idx_ref):
    score = score_ref[...]
    rows, nb_pad = score.shape
    jlf = _iota((rows, nb_pad), 1).astype(F32)
    lane = _iota((rows, LANES), 1)
    idx = jnp.zeros((rows, LANES), F32)
    for k in range(TOP_N):
        mx = jnp.max(score, axis=-1, keepdims=True)
        am = jnp.min(jnp.where(score == mx, jlf, float(nb_pad)), axis=-1, keepdims=True)
        idx = jnp.where(lane == k, am, idx)
        score = jnp.where(jlf == am, -3.0, score)
    idx_ref[...] = idx.astype(jnp.int32)


def _top_blocks(score):
    rows = score.shape[0]
    return pl.pallas_call(
        _top_blocks_kernel,
        grid=(1,),
        in_specs=[_const_spec(score.shape)],
        out_specs=_const_spec((rows, LANES)),
        out_shape=jax.ShapeDtypeStruct((rows, LANES), jnp.int32),
        compiler_params=pltpu.CompilerParams(dimension_semantics=("arbitrary",), vmem_limit_bytes=VMEM_LIMIT),
        name="sample_topk",
    )(score)


def _ssel_kernel(pt_ref, idx_ref, kpool_ref, vpool_ref, q_ref, ksn_ref, vsn_ref, kwn_ref, vwn_ref,
                 kwb_ref, vwb_ref, gn_ref, oc_ref, o_ref, kbuf_ref, vbuf_ref, sem_ref,
                 *, n_new, past_len):
    b = pl.program_id(0)
    n_pool_blocks = past_len // SLC_LEN
    blocks_page = PAGE_SIZE // SLC_LEN

    def block_copy(pool_ref, buf_ref, sem, g, qi, k):
        j = jnp.minimum(idx_ref[b, (g * n_new + qi) * TOP_N + k], n_pool_blocks - 1)
        page = pt_ref[b, j // blocks_page]
        return pltpu.make_async_copy(pool_ref.at[page, pl.ds(g * HEAD_DIM, HEAD_DIM), :],
                                     buf_ref.at[g * n_new + qi, :, pl.ds(k * PAGE_SIZE, PAGE_SIZE)], sem)

    slots = [(g, qi, k) for g in range(N_KV) for qi in range(n_new) for k in range(TOP_N)]
    for g, qi, k in slots:
        block_copy(kpool_ref, kbuf_ref, sem_ref.at[0], g, qi, k).start()
        block_copy(vpool_ref, vbuf_ref, sem_ref.at[1], g, qi, k).start()

    r32 = HPG * S_ROWS
    qi32 = _iota((r32, LANES), 0) & (S_ROWS - 1)
    n_sel = TOP_N * PAGE_SIZE
    half_id = _iota((r32, n_sel), 1) >> 6
    new_rows = LANES
    newcol = _iota((r32, new_rows), 1)
    qrow_new = _iota((r32, new_rows), 0) & (S_ROWS - 1)

    def pad_new(ref):
        return jnp.concatenate([ref[...], jnp.zeros((new_rows - S_ROWS, KV_W), F32)], axis=0).astype(BF16)

    ks_new, vs_new, kw_new, vw_new = pad_new(ksn_ref), pad_new(vsn_ref), pad_new(kwn_ref), pad_new(vwn_ref)

    w_buf = kwb_ref.shape[1]
    wpos = jnp.concatenate([past_len - w_buf + _iota((r32, w_buf), 1), past_len + newcol], axis=1)
    qpos_w = past_len + (_iota((r32, w_buf + new_rows), 0) & (S_ROWS - 1))
    wmask = (wpos <= qpos_w) & (wpos >= qpos_w - WINDOW)
    wmask = wmask & jnp.concatenate([jnp.full((r32, w_buf), True), newcol < n_new], axis=1)
    kw_t = kwb_ref[...].astype(BF16)
    vw_t = vwb_ref[...].astype(BF16)
    gates = _sigmoid(gn_ref[...])

    o_w = []
    q_g = []
    for g in range(N_KV):
        qg = _group_q(q_ref, g).astype(BF16)
        q_g.append(qg)
        s_w = jnp.concatenate([_dot(qg, kw_t), _dot_nt(qg, kw_new)], axis=1)
        p_w = _masked_softmax(s_w, wmask).astype(BF16)
        o_w.append(_group_out(_dot_nt(p_w[:, :w_buf], vw_t) + _dot(p_w[:, w_buf:], vw_new), g))

    for g, qi, k in slots:
        block_copy(kpool_ref, kbuf_ref, sem_ref.at[0], g, qi, k).wait()
        block_copy(vpool_ref, vbuf_ref, sem_ref.at[1], g, qi, k).wait()

    pad64 = jnp.zeros((r32, LANES - HEAD_DIM), F32)
    for g in range(N_KV):
        q64 = q_ref[g * HPG:(g + 1) * HPG].reshape(r32, LANES)[:, :HEAD_DIM]
        o_s = jnp.zeros((r32, LANES), F32)
        for qi in range(n_new):
            smask = jnp.full((r32, n_sel), False)
            n_hit = jnp.int32(0)
            for k in range(TOP_N):
                j = idx_ref[b, (g * n_new + qi) * TOP_N + k]
                smask = smask | (half_id == jnp.where(j < n_pool_blocks, blocks_page * k + j % blocks_page, -1))
                n_hit = n_hit + jnp.where(j == n_pool_blocks, 1, 0)
            nmask = (newcol <= qrow_new) & (newcol < jnp.where(n_hit > 0, n_new, 0))
            s_s = jnp.concatenate([_dot(q64, kbuf_ref[g * n_new + qi].astype(BF16)),
                                   _dot_nt(q_g[g], ks_new)], axis=1)
            p_s = _masked_softmax(s_s, jnp.concatenate([smask, nmask], axis=1)).astype(BF16)
            o_pool = _dot_nt(p_s[:, :n_sel], vbuf_ref[g * n_new + qi].astype(BF16))
            o_qi = jnp.concatenate([o_pool, pad64], axis=1) + _group_out(_dot(p_s[:, n_sel:], vs_new), g)
            o_s = jnp.where(qi32 == qi, o_qi, o_s)
        o_c = oc_ref[g]
        heads = []
        for h in range(HPG):
            col = (g * HPG + h) * 3
            rs = slice(h * S_ROWS, (h + 1) * S_ROWS)
            heads.append(gates[:, col:col + 1] * o_c[rs] + gates[:, col + 1:col + 2] * o_s[rs]
                         + gates[:, col + 2:col + 3] * o_w[g][rs])
        for pr in range(HPG // 2):
            c0 = (g * HPG + 2 * pr) * HEAD_DIM
            o_ref[:, c0:c0 + LANES] = _pair_heads(heads[2 * pr], heads[2 * pr + 1])


def _ssel(page_table, idx, kpool, vpool, qp_s, ks_n, vs_n, kw_n, vw_n, kw_buf, vw_buf, gn_s, oc,
          n_new, past_len):
    n_batch = page_table.shape[0]
    w_buf = kw_buf.shape[2]
    rows8 = pl.BlockSpec((S_ROWS, KV_W), lambda b, pt, ix: (b, 0))
    wbuf_spec = pl.BlockSpec((None, KV_W, w_buf), lambda b, pt, ix: (b, 0, 0))
    in_specs = [pl.BlockSpec(memory_space=pl.ANY), pl.BlockSpec(memory_space=pl.ANY),
                pl.BlockSpec((None, N_HEADS, S_ROWS, LANES), lambda b, pt, ix: (0, 0, b, 0)),
                rows8, rows8, rows8, rows8, wbuf_spec, wbuf_spec,
                pl.BlockSpec((S_ROWS, LANES), lambda b, pt, ix: (b, 0)),
                pl.BlockSpec((None, N_KV, HPG * S_ROWS, LANES), lambda b, pt, ix: (b, 0, 0, 0))]
    return pl.pallas_call(
        functools.partial(_ssel_kernel, n_new=n_new, past_len=past_len),
        grid_spec=pltpu.PrefetchScalarGridSpec(
            num_scalar_prefetch=2, grid=(n_batch,), in_specs=in_specs,
            out_specs=pl.BlockSpec((S_ROWS, NSA_W), lambda b, pt, ix: (b, 0)),
            scratch_shapes=[pltpu.VMEM((N_KV * n_new, HEAD_DIM, TOP_N * PAGE_SIZE), F32)] * 2
            + [pltpu.SemaphoreType.DMA((2,))]),
        out_shape=jax.ShapeDtypeStruct((n_batch * S_ROWS, NSA_W), F32),
        compiler_params=pltpu.CompilerParams(dimension_semantics=("arbitrary",),
                                             vmem_limit_bytes=VMEM_LIMIT),
        name="sample_sel",
    )(page_table, idx, kpool, vpool, qp_s, ks_n, vs_n, kw_n, vw_n, kw_buf, vw_buf, gn_s, oc)


def _prep_w_in(w_in):
    q = w_in[:, :NSA_W].reshape(D_MODEL, N_HEADS, HEAD_DIM)
    q = jnp.pad(q, ((0, 0), (0, 0), (0, LANES - HEAD_DIM))).reshape(D_MODEL, N_HEADS * LANES)
    c = NSA_W
    kv = w_in[:, c:c + 6 * KV_W]
    c += 6 * KV_W
    gn = jnp.pad(w_in[:, c:c + 3 * N_HEADS], ((0, 0), (0, LANES - 3 * N_HEADS)))
    c += 3 * N_HEADS
    rest = w_in[:, c:]
    return jnp.concatenate([q, kv, gn, rest], axis=1).astype(BF16)


def _prep_cmp(pe, w1, w2):
    w1r = w1.reshape(CMP_R, CMP_STRIDE, HEAD_DIM, HEAD_DIM)
    eye = jnp.eye(N_KV, dtype=w1.dtype)
    w1big = jnp.einsum('rjdh,ge->jgdreh', w1r, eye).reshape(CMP_STRIDE * KV_W, CMP_R * KV_W)
    pe_rows = jnp.broadcast_to(pe.reshape(CMP_R, CMP_STRIDE, 1, HEAD_DIM),
                               (CMP_R, CMP_STRIDE, N_KV, HEAD_DIM)).reshape(CMP_R, CMP_STRIDE * KV_W)
    pe_rows = jnp.pad(pe_rows, ((0, SUBLANES - CMP_R), (0, 0)))
    w2big = jnp.einsum('hd,ge->ghed', w2, eye).reshape(KV_W, KV_W)
    return pe_rows.astype(BF16), w1big.astype(BF16), w2big.astype(BF16)


def _agg_matrix(n_rows, n_blocks, n_cols, col0):
    i = jnp.arange(n_rows)[:, None] * CMP_STRIDE
    jj = jnp.arange(n_cols)[None, :] - col0
    hit = (i < jj * SLC_LEN + SLC_LEN) & (i + CMP_LEN > jj * SLC_LEN) & (jj >= 0) & (jj < n_blocks)
    return hit.astype(BF16)


def kernel(x_prompt, x_sample, cache_k_cmp, cache_v_cmp, cache_k_slc, cache_v_slc, cache_k_win, cache_v_win, state_conv, page_table, ln_g, w_in, pe_k, w1_k, w2_k, pe_v, w1_v, w2_v, dw_k, dw_b, cln_g, cln_b, pw_w, pw_b, w_pa, w_pb, w_o, final_g):
    assert w_in.shape[0] == 1, "single-layer stack"
    n_b, t_len, _ = x_prompt.shape
    n_db, n_new, _ = x_sample.shape
    n_pages = page_table.shape[1]
    past_len = n_pages * PAGE_SIZE
    w_buf = cache_k_win.shape[2]
    assert n_new <= S_ROWS and t_len % TQ == 0 and t_len >= WINDOW + TQ
    assert n_new < CMP_STRIDE, "the new rows must not complete a compression chunk"
    assert t_len // SLC_LEN <= LANES - HEAD_DIM and past_len % PAGE_SIZE == 0

    w_all = _prep_w_in(w_in[0])
    lng = ln_g[0].reshape(1, D_MODEL)
    cmp_w = _prep_cmp(pe_k[0], w1_k[0], w2_k[0]) + _prep_cmp(pe_v[0], w1_v[0], w2_v[0])
    tail_w = (dw_b[0].reshape(1, CONV_C), cln_g[0].reshape(1, CONV_C), cln_b[0].reshape(1, CONV_C),
              pw_w[0].astype(BF16), pw_b[0].reshape(1, CONV_C), w_pa[0].astype(BF16), w_pb[0].astype(BF16),
              w_o[0].astype(BF16), final_g.reshape(1, D_MODEL))
    dw = jnp.pad(dw_k[0], ((0, 1), (0, 0)))

    (qt, kct, vct, kst, vst, kwt, vwt, gnt, sga, u, sgb, sma, smb, k_c, v_c, ksa, vst_g, kwp, vwt_g) = _proj(
        x_prompt.reshape(n_b * t_len, D_MODEL), lng, w_all, n_b, t_len, (0, 1), True)
    seq = lambda a: a.reshape(n_b, t_len, a.shape[-1])
    kcm, vcmt = _cmp_prompt(seq(k_c), seq(v_c), cmp_w)
    n_cmp_rows = t_len // CMP_STRIDE
    aggt_p = _agg_matrix(n_cmp_rows, t_len // SLC_LEN, LANES, 0).T
    o_a = _attn_prompt(qt, gnt, kcm, vcmt, ksa, vst_g, kwp, vwt_g, aggt_p)
    y_prompt = _tail_prompt(x_prompt, o_a, seq(sga), seq(u), seq(sgb), seq(sma), seq(smb), dw, tail_w)

    xs = jnp.pad(x_sample, ((0, 0), (0, S_ROWS - n_new), (0, 0))).reshape(n_db * S_ROWS, D_MODEL)
    (qp_s, kct_s, vct_s, kst_s, vst_s, kwt_s, vwt_s, gn_s, sga_s, u_s, sgb_s, sma_s, smb_s,
     k_s_s, v_s_s, k_w_s, v_w_s) = _proj(xs, lng, w_all, 1, n_db * S_ROWS, (2, 3, 4, 5), False)
    pool = lambda c: jnp.transpose(c[0], (0, 2, 3, 1)).reshape(c.shape[1], KV_W, PAGE_SIZE)
    n_blocks_s = past_len // SLC_LEN + 1
    nb_pad = -(-n_blocks_s // LANES) * LANES
    agg_s = _agg_matrix(past_len // CMP_STRIDE, n_blocks_s, nb_pad, 0)
    agg_s = agg_s.at[past_len // CMP_STRIDE - 1].set(0)
    oc_s, idx_s = _scmp(page_table, pool(cache_k_cmp), pool(cache_v_cmp), qp_s, cmp_w, agg_s, past_len)
    idx_flat = idx_s.reshape(n_db, N_KV, S_ROWS, LANES)[:, :, :n_new, :TOP_N].reshape(n_db, N_KV * n_new * TOP_N)
    win = lambda c: jnp.transpose(c[0], (0, 2, 3, 1)).reshape(n_db, KV_W, w_buf)
    o_a_s = _ssel(page_table, idx_flat, pool(cache_k_slc), pool(cache_v_slc), qp_s, k_s_s, v_s_s, k_w_s,
                  v_w_s, win(cache_k_win), win(cache_v_win), gn_s, oc_s, n_new, past_len)
    u_new = u_s.reshape(n_db, S_ROWS, CONV_C)[:, :n_new]
    up_s = jnp.concatenate([state_conv[0], u_new], axis=1)
    up_pad = jnp.pad(up_s, ((0, 0), (0, UP_ROWS - up_s.shape[1]), (0, 0)))
    dw_shift = jnp.stack([jnp.pad(dw_k[0], ((t, UP_ROWS - CONV_W - t), (0, 0))) for t in range(n_new)])
    y_s = _tail_sample(xs, o_a_s, sga_s, up_pad, sgb_s, sma_s, smb_s, dw_shift, tail_w, n_new)
    y_sample = y_s.reshape(n_db, S_ROWS, D_MODEL)[:, :n_new]

    def p_rows(at):
        return jnp.transpose(at.reshape(n_b, N_KV, HEAD_DIM, at.shape[-1]), (0, 3, 1, 2))[None]

    def s_rows(at):
        a = at.reshape(N_KV, HEAD_DIM, n_db, S_ROWS)[:, :, :, :n_new]
        return jnp.transpose(a, (2, 3, 0, 1))[None]

    win_p = min(WINDOW, t_len)
    p_conv = seq(u)[:, -(CONV_W - 1):][None]
    s_k_win = jnp.concatenate([cache_k_win[0], s_rows(kwt_s)[0]], axis=1)[:, -w_buf:][None]
    s_v_win = jnp.concatenate([cache_v_win[0], s_rows(vwt_s)[0]], axis=1)[:, -w_buf:][None]
    s_conv = up_s[:, -(CONV_W - 1):][None]
    return (y_prompt, y_sample, p_rows(kct), p_rows(vct), p_rows(kst), p_rows(vst),
            p_rows(kwt[:, :, -win_p:]), p_rows(vwt[:, :, -win_p:]), p_conv,
            s_rows(kct_s), s_rows(vct_s), s_rows(kst_s), s_rows(vst_s), s_k_win, s_v_win, s_conv)
```

```python
import functools

import jax
import jax.numpy as jnp
from jax import lax
from jax.experimental import pallas as pl
from jax.experimental.pallas import tpu as pltpu

F32 = jnp.float32
BF16 = jnp.bfloat16

D_MODEL = 1024
N_HEADS = 8
HEAD_DIM = 64
N_KV = 2
HPG = N_HEADS // N_KV
NSA_W = N_HEADS * HEAD_DIM
KV_W = N_KV * HEAD_DIM
CMP_LEN = 32
CMP_STRIDE = 16
CMP_R = CMP_LEN // CMP_STRIDE
SLC_LEN = 64
TOP_N = 16
WINDOW = 512
CONV_C = 512
CONV_W = 31
PAGE_SIZE = 128
EPS = 1e-6
FORCED_SCORE = 1e4
NEG_INF = -1e30
MASK_BIAS = -1e4

LANES = 128
SUBLANES = 8
VMEM_LIMIT = 56 * 1024 * 1024

C_Q = 0
C_KV = C_Q + N_HEADS * LANES
C_GN = C_KV + 6 * KV_W
C_GA = C_GN + LANES
C_GLU = C_GA + NSA_W
C_GB = C_GLU + 2 * CONV_C
C_MRG = C_GB + CONV_C
C_END = C_MRG + 2 * D_MODEL

TM_PROJ = 256
TQ = 256
TM_TAIL = 256
PAGES_PER_STEP = 32


def _sigmoid(x):
    return jax.nn.sigmoid(x)


def _silu(x):
    return jax.nn.silu(x)


def _dot(a, b):
    return jnp.dot(a, b, preferred_element_type=F32)


def _dot_nt(a, b):
    return lax.dot_general(a, b, (((1,), (1,)), ((), ())), preferred_element_type=F32)


def _iota(shape, dim):
    return lax.broadcasted_iota(jnp.int32, shape, dim)


def _const_spec(shape):
    nd = len(shape)
    return pl.BlockSpec(shape, lambda *_: (0,) * nd)


def _proj_kernel(x_ref, lng_ref, w_ref, qp_ref, kct_ref, vct_ref, kst_ref, vst_ref, kwt_ref, vwt_ref,
                 gn_ref, sga_ref, u_ref, sgb_ref, sma_ref, smb_ref, *more_refs, tiles_per_batch, row_ids,
                 attn_ops):
    x = x_ref[...]
    tm = x.shape[0]
    xn = x * lax.rsqrt(jnp.mean(x * x, axis=-1, keepdims=True) + EPS) * lng_ref[...]
    xn = xn.astype(BF16)

    zq = _dot(xn, w_ref[:, C_Q:C_KV]) * (HEAD_DIM ** -0.5)
    zgn = _dot(xn, w_ref[:, C_GN:C_GA])
    for h in range(N_HEADS):
        qh = zq[:, h * LANES:(h + 1) * LANES]
        qp_ref[h] = (qh.T if attn_ops else qh).astype(BF16)
    gn_ref[...] = zgn.T if attn_ops else zgn

    zkv = _dot(xn, w_ref[:, C_KV:C_GN])
    kv = [zkv[:, i * KV_W:(i + 1) * KV_W] for i in range(6)]
    kvt = [v.T for v in kv]
    for r, v in zip((kct_ref, vct_ref, kst_ref, vst_ref, kwt_ref, vwt_ref), kvt):
        r[...] = v
    for r, i in zip(more_refs[:len(row_ids)], row_ids):
        r[...] = kv[i]

    if attn_ops:
        ksa_ref, vst_g_ref, kwp_ref, vwt_g_ref = more_refs[len(row_ids):]
        t0 = (pl.program_id(0) % tiles_per_batch) * tm
        lane = _iota((tm, LANES), 1)
        blk = (t0 + _iota((tm, LANES), 0)) >> 6
        onehot = jnp.where((lane - HEAD_DIM) == blk, 1.0, 0.0)
        ones_row = jnp.where(_iota((HEAD_DIM, tm), 0) == 0, 1.0, 0.0)
        for g in range(N_KV):
            rows_g = slice(g * HEAD_DIM, (g + 1) * HEAD_DIM)

            def grp(v):
                return v if g == 0 else pltpu.roll(v, HEAD_DIM, axis=1)

            ksa_ref[g] = jnp.where(lane < HEAD_DIM, grp(kv[2]), onehot).astype(BF16)
            kwp_ref[g] = grp(kv[4]).astype(BF16)
            for ref, vt in ((vst_g_ref, kvt[3]), (vwt_g_ref, kvt[5])):
                v_aug = jnp.concatenate([vt[rows_g], ones_row], axis=0).astype(BF16)
                for j in range(tm // KEY_CHUNK):
                    ref[g, j] = v_aug[:, j * KEY_CHUNK:(j + 1) * KEY_CHUNK]

    sga_ref[...] = _silu(_dot(xn, w_ref[:, C_GA:C_GLU]))
    zglu = _dot(xn, w_ref[:, C_GLU:C_GB])
    u_ref[...] = zglu[:, :CONV_C] * _sigmoid(zglu[:, CONV_C:])
    sgb_ref[...] = _silu(_dot(xn, w_ref[:, C_GB:C_MRG]))
    zm = _dot(xn, w_ref[:, C_MRG:C_END])
    sma_ref[...] = _sigmoid(zm[:, :D_MODEL])
    smb_ref[...] = _sigmoid(zm[:, D_MODEL:])


def _proj(x2d, ln_g, w_all, n_batch, t_len, row_ids, attn_ops):
    m = x2d.shape[0]
    tm = min(TM_PROJ, t_len)
    tpb = t_len // tm
    grid = (m // tm,)

    def row_spec(width):
        return pl.BlockSpec((tm, width), lambda i: (i, 0))

    def bt_spec(lead, width):
        return pl.BlockSpec((None, lead, tm, width), lambda i: (i // tpb, 0, i % tpb, 0))

    t_spec = pl.BlockSpec((None, KV_W, tm), lambda i: (i // tpb, 0, i % tpb))
    if attn_ops:
        out_shape = [jax.ShapeDtypeStruct((n_batch, N_HEADS, LANES, t_len), BF16)]
        out_specs = [pl.BlockSpec((None, N_HEADS, LANES, tm), lambda i: (i // tpb, 0, 0, i % tpb))]
    else:
        out_shape = [jax.ShapeDtypeStruct((n_batch, N_HEADS, t_len, LANES), BF16)]
        out_specs = [bt_spec(N_HEADS, LANES)]
    out_shape += [jax.ShapeDtypeStruct((n_batch, KV_W, t_len), F32)] * 6
    out_specs += [t_spec] * 6
    if attn_ops:
        out_shape.append(jax.ShapeDtypeStruct((n_batch, LANES, t_len), F32))
        out_specs.append(t_spec)
    else:
        out_shape.append(jax.ShapeDtypeStruct((m, LANES), F32))
        out_specs.append(row_spec(LANES))
    for width in (NSA_W, CONV_C, CONV_C, D_MODEL, D_MODEL):
        out_shape.append(jax.ShapeDtypeStruct((m, width), F32))
        out_specs.append(row_spec(width))
    out_shape += [jax.ShapeDtypeStruct((m, KV_W), F32)] * len(row_ids)
    out_specs += [row_spec(KV_W)] * len(row_ids)
    if attn_ops:
        cpt = tm // KEY_CHUNK
        k_shape = jax.ShapeDtypeStruct((n_batch, N_KV, t_len, LANES), BF16)
        vt_shape = jax.ShapeDtypeStruct((n_batch, N_KV, t_len // KEY_CHUNK, LANES, KEY_CHUNK), BF16)
        vt_spec = pl.BlockSpec((None, N_KV, cpt, LANES, KEY_CHUNK), lambda i: (i // tpb, 0, i % tpb, 0, 0))
        out_shape += [k_shape, vt_shape, k_shape, vt_shape]
        out_specs += [bt_spec(N_KV, LANES), vt_spec, bt_spec(N_KV, LANES), vt_spec]

    return pl.pallas_call(
        functools.partial(_proj_kernel, tiles_per_batch=tpb, row_ids=row_ids, attn_ops=attn_ops),
        grid=grid,
        in_specs=[row_spec(D_MODEL), _const_spec((1, D_MODEL)), _const_spec((D_MODEL, C_END))],
        out_specs=out_specs,
        out_shape=out_shape,
        compiler_params=pltpu.CompilerParams(dimension_semantics=("arbitrary",),
                                             vmem_limit_bytes=VMEM_LIMIT),
        name="proj",
    )(x2d, ln_g, w_all)


def _compress_from_partials(per, pe_ref, w1_ref, w2_ref):
    n = per.shape[0]
    hp = _dot(pe_ref[...], w1_ref[...])
    h0 = hp[0:1, :KV_W] + hp[1:2, KV_W:]
    h = h0 + per[:, :KV_W] + pltpu.roll(per[:, KV_W:], n - 1, axis=0)
    return _dot(_silu(h).astype(BF16), w2_ref[...])


def _chunk_rows(ref, n_chunks):
    return jnp.concatenate([ref[pl.ds(j, n_chunks, stride=CMP_STRIDE), :] for j in range(CMP_STRIDE)],
                           axis=-1)


def _cmp_prompt_kernel(kc_ref, vc_ref, pek_ref, w1k_ref, w2k_ref, pev_ref, w1v_ref, w2v_ref,
                       ok_ref, ov_ref):
    n_chunks = kc_ref.shape[0] // CMP_STRIDE
    for x_ref, pe_ref, w1_ref, w2_ref, o_ref, transposed in (
            (kc_ref, pek_ref, w1k_ref, w2k_ref, ok_ref, False), (vc_ref, pev_ref, w1v_ref, w2v_ref, ov_ref, True)):
        per = _dot(_chunk_rows(x_ref, n_chunks).astype(BF16), w1_ref[...])
        c = _compress_from_partials(per, pe_ref, w1_ref, w2_ref)
        for g, cg in enumerate((c, pltpu.roll(c, HEAD_DIM, axis=1))):
            o_ref[g] = (cg.T if transposed else cg).astype(BF16)


def _cmp_prompt(k_c, v_c, cmp_w):
    n_batch, t_len, _ = k_c.shape
    n_chunks = t_len // CMP_STRIDE
    seq = pl.BlockSpec((None, t_len, KV_W), lambda b: (b, 0, 0))
    out = pl.BlockSpec((None, N_KV, n_chunks, KV_W), lambda b: (b, 0, 0, 0))
    w_specs = [_const_spec(w.shape) for w in cmp_w]
    return pl.pallas_call(
        _cmp_prompt_kernel,
        grid=(n_batch,),
        in_specs=[seq, seq] + w_specs,
        out_specs=[out, out],
        out_shape=[jax.ShapeDtypeStruct((n_batch, N_KV, n_chunks, KV_W), BF16)] * 2,
        compiler_params=pltpu.CompilerParams(dimension_semantics=("arbitrary",),
                                             vmem_limit_bytes=VMEM_LIMIT),
        name="cmp_prompt",
    )(k_c, v_c, *cmp_w)


def _masked_softmax(s, mask):
    s = jnp.where(mask, s, NEG_INF)
    e = jnp.where(mask, jnp.exp(s - jnp.max(s, axis=-1, keepdims=True)), 0.0)
    return e / jnp.maximum(jnp.sum(e, axis=-1, keepdims=True), 1e-30)


def _softmax_rows(s, mask):
    s = jnp.where(mask, s, NEG_INF)
    e = jnp.where(mask, jnp.exp(s - jnp.max(s, axis=0, keepdims=True)), 0.0)
    return e / jnp.maximum(jnp.sum(e, axis=0, keepdims=True), 1e-30)


def _block_scores(imp, j, qpos, n_blocks):
    cur = qpos >> 6
    forced = (j == 0) | (j == cur) | (j == cur - 1)
    score = jnp.where(forced, FORCED_SCORE, imp)
    score = jnp.where(j * SLC_LEN <= qpos, score, -1.0)
    return jnp.where((j >= 0) & (j < n_blocks), score, -2.0)


def _pair_heads(o_even, o_odd):
    lane = _iota(o_even.shape, 1)
    return jnp.where(lane < HEAD_DIM, o_even, pltpu.roll(o_odd, HEAD_DIM, axis=1))


KEY_CHUNK = 128
SEL_ROWS = 32


def _attn_prompt_kernel(qt_ref, gnt_ref, kcm_ref, vcmt_ref, ksa_ref, vst_ref, kwp_ref, vwt_ref, aggt_ref,
                        o_ref, oacc_ref, qsel_ref, m_ref, acc_ref, *, t_len):
    tq = gnt_ref.shape[1]
    n_blocks = t_len // SLC_LEN
    i = pl.program_id(1)
    t0 = i * tq
    chunks_tile = tq // KEY_CHUNK

    gates = _sigmoid(gnt_ref[...])
    qpos_c = t0 + _iota((KV_W, tq), 1)
    cvalid = (_iota((KV_W, tq), 0) * CMP_STRIDE + (CMP_LEN - 1)) <= qpos_c
    jblk = _iota((SEL_ROWS, tq), 0)
    qpos_j = t0 + _iota((SEL_ROWS, tq), 1)
    row_minus_lane = _iota((KEY_CHUNK, tq), 0) - _iota((KEY_CHUNK, tq), 1)

    def gate(hh, branch):
        r = hh * 3 + branch
        return gates[r:r + 1, :]

    def flash(q_ref, k_ref, vt_ref, c_lo, banded, branch):
        def step(c, hh, diagonal):
            g = hh // HPG
            kc = k_ref[g, pl.ds(pl.multiple_of(c * KEY_CHUNK, KEY_CHUNK), KEY_CHUNK), :]
            s = _dot(kc, q_ref[hh])
            off = c * KEY_CHUNK - t0
            if diagonal:
                s = jnp.where(row_minus_lane + off <= 0, s, NEG_INF)
            elif banded:
                s = jnp.where(row_minus_lane + off >= -WINDOW, s, NEG_INF)
            m = m_ref[hh]
            m_new = jnp.maximum(m, jnp.max(s, axis=0, keepdims=True))
            p = jnp.exp(s - m_new).astype(BF16)
            acc_ref[hh] = jnp.exp(m - m_new) * acc_ref[hh] + _dot(vt_ref[g, c], p)
            m_ref[hh] = m_new

        m_ref[...] = jnp.full(m_ref.shape, NEG_INF, F32)
        acc_ref[...] = jnp.zeros(acc_ref.shape, F32)

        def body(ct, carry):
            for cd in range(chunks_tile):
                for hh in range(N_HEADS):
                    step(ct * chunks_tile + cd, hh, False)
            return carry

        lax.fori_loop(c_lo // chunks_tile, i, body, 0)
        for cd in range(chunks_tile):
            for hh in range(N_HEADS):
                step(i * chunks_tile + cd, hh, True)
        for hh in range(N_HEADS):
            acc = acc_ref[hh]
            oacc_ref[hh] = oacc_ref[hh] + gate(hh, branch) * (acc / acc[HEAD_DIM:HEAD_DIM + 1, :])

    for g in range(N_KV):
        psum = None
        for h in range(HPG):
            hh = g * HPG + h
            p_c = _softmax_rows(_dot(kcm_ref[g], qt_ref[hh]), cvalid)
            oacc_ref[hh] = gate(hh, 0) * _dot(vcmt_ref[g], p_c.astype(BF16))
            psum = p_c if psum is None else psum + p_c
        imp = _dot(aggt_ref[...], psum.astype(BF16))[:SEL_ROWS]
        score = _block_scores(imp, jblk, qpos_j, n_blocks)
        rank = jnp.zeros((SEL_ROWS, tq), jnp.int32)
        for jp in range(n_blocks):
            other = score[jp:jp + 1, :]
            ahead = (other > score) | ((other == score) & (jblk > jp))
            rank = rank + jnp.where(ahead, 1, 0)
        allowed = (rank < TOP_N) & (jblk < n_blocks) & (jblk * SLC_LEN <= qpos_j)
        bias = jnp.where(allowed, 0.0, MASK_BIAS).astype(BF16)
        pad = jnp.zeros((LANES - HEAD_DIM - SEL_ROWS, tq), BF16)
        for h in range(HPG):
            qsel_ref[g * HPG + h] = jnp.concatenate([qt_ref[g * HPG + h, 0:HEAD_DIM, :], bias, pad], axis=0)

    flash(qsel_ref, ksa_ref, vst_ref, 0, False, 1)
    flash(qt_ref, kwp_ref, vwt_ref, jnp.maximum(i * chunks_tile - WINDOW // KEY_CHUNK, 0), True, 2)
    for pr in range(N_HEADS // 2):
        two = jnp.concatenate([oacc_ref[2 * pr, 0:HEAD_DIM, :], oacc_ref[2 * pr + 1, 0:HEAD_DIM, :]], axis=0)
        o_ref[:, pr * LANES:(pr + 1) * LANES] = two.T


def _attn_prompt(qt, gnt, kcm, vcmt, ksa, vst, kwp, vwt, aggt):
    n_batch, _, _, t_len = qt.shape
    tq = TQ
    n_cmp = kcm.shape[2]
    assert tq % KEY_CHUNK == 0 and WINDOW % KEY_CHUNK == 0 and t_len // SLC_LEN <= SEL_ROWS
    assert n_cmp == KV_W, "one lane tile of compressed blocks"
    k_spec = pl.BlockSpec((None, N_KV, t_len, LANES), lambda b, i: (b, 0, 0, 0))
    vt_spec = pl.BlockSpec((None, N_KV, t_len // KEY_CHUNK, LANES, KEY_CHUNK), lambda b, i: (b, 0, 0, 0, 0))
    cmp_spec = pl.BlockSpec((None, N_KV, n_cmp, KV_W), lambda b, i: (b, 0, 0, 0))
    return pl.pallas_call(
        functools.partial(_attn_prompt_kernel, t_len=t_len),
        grid=(n_batch, t_len // tq),
        in_specs=[pl.BlockSpec((None, N_HEADS, LANES, tq), lambda b, i: (b, 0, 0, i)),
                  pl.BlockSpec((None, LANES, tq), lambda b, i: (b, 0, i)),
                  cmp_spec, cmp_spec, k_spec, vt_spec, k_spec, vt_spec, _const_spec(aggt.shape)],
        out_specs=pl.BlockSpec((None, tq, NSA_W), lambda b, i: (b, i, 0)),
        out_shape=jax.ShapeDtypeStruct((n_batch, t_len, NSA_W), F32),
        scratch_shapes=[pltpu.VMEM((N_HEADS, LANES, tq), F32), pltpu.VMEM((N_HEADS, LANES, tq), BF16),
                        pltpu.VMEM((N_HEADS, 1, tq), F32), pltpu.VMEM((N_HEADS, LANES, tq), F32)],
        compiler_params=pltpu.CompilerParams(dimension_semantics=("arbitrary", "arbitrary"),
                                             vmem_limit_bytes=VMEM_LIMIT),
        name="attn_prompt",
    )(qt, gnt, kcm, vcmt, ksa, vst, kwp, vwt, aggt)


def _tail_math(x, o_a, sga, c_pre, sgb, sma, smb, w):
    (dwb_ref, clg_ref, clb_ref, pww_ref, pwb_ref, wpa_ref, wpb_ref, wo_ref, fg_ref) = w
    c = c_pre + dwb_ref[...]
    mu = jnp.mean(c, axis=-1, keepdims=True)
    var = jnp.mean(jnp.square(c - mu), axis=-1, keepdims=True)
    cn = (c - mu) * lax.rsqrt(var + EPS) * clg_ref[...] + clb_ref[...]
    cp = _dot(_silu(cn).astype(BF16), pww_ref[...]) + pwb_ref[...]
    br_a = _dot((o_a * sga).astype(BF16), wpa_ref[...])
    br_b = _dot((cp * sgb).astype(BF16), wpb_ref[...])
    h = sma * br_a + smb * br_b
    xo = x + _dot(h.astype(BF16), wo_ref[...])
    return xo * lax.rsqrt(jnp.mean(xo * xo, axis=-1, keepdims=True) + EPS) * fg_ref[...]


HIST = 32


def _tail_prompt_kernel(x_ref, oa_ref, sga_ref, u_ref, sgb_ref, sma_ref, smb_ref, dw_ref, *rest):
    w, (y_ref, up_ref, sh_ref) = rest[:9], rest[9:]
    tm = x_ref.shape[0]

    @pl.when(pl.program_id(1) == 0)
    def _():
        up_ref[0:HIST, :] = jnp.zeros((HIST, CONV_C), F32)

    up_ref[HIST:HIST + tm, :] = u_ref[...]
    first = HIST - (CONV_W - 1)
    n_sh = HIST + tm - SUBLANES
    acc = jnp.zeros((tm, CONV_C), F32)
    for phase in range(SUBLANES):
        tiles = [(first + k) // SUBLANES for k in range(CONV_W) if (first + k) % SUBLANES == phase]
        if not tiles:
            continue
        if phase:
            sh_ref[phase, 0:n_sh, :] = up_ref[pl.ds(phase, n_sh), :]
        for a in tiles:
            k = a * SUBLANES + phase - first
            rows = up_ref[pl.ds(a * SUBLANES, tm), :] if phase == 0 else sh_ref[phase, pl.ds(a * SUBLANES, tm), :]
            acc = acc + rows * dw_ref[k:k + 1, :]
    up_ref[0:HIST, :] = up_ref[tm:tm + HIST, :]
    y_ref[...] = _tail_math(x_ref[...], oa_ref[...], sga_ref[...], acc, sgb_ref[...], sma_ref[...],
                            smb_ref[...], w)


def _tail_prompt(x, o_a, sga, u, sgb, sma, smb, dw, tail_w):
    n_batch, t_len, _ = x.shape
    tm = TM_TAIL

    def bt(width):
        return pl.BlockSpec((None, tm, width), lambda b, t: (b, t, 0))

    ins = [x, o_a, sga, u, sgb, sma, smb]
    return pl.pallas_call(
        _tail_prompt_kernel,
        grid=(n_batch, t_len // tm),
        in_specs=[bt(a.shape[-1]) for a in ins] + [_const_spec(dw.shape)] + [_const_spec(a.shape) for a in tail_w],
        out_specs=bt(D_MODEL),
        out_shape=jax.ShapeDtypeStruct((n_batch, t_len, D_MODEL), F32),
        scratch_shapes=[pltpu.VMEM((HIST + tm, CONV_C), F32), pltpu.VMEM((SUBLANES, HIST + tm, CONV_C), F32)],
        compiler_params=pltpu.CompilerParams(dimension_semantics=("arbitrary", "arbitrary"),
                                             vmem_limit_bytes=VMEM_LIMIT),
        name="tail_prompt",
    )(*ins, dw, *tail_w)


S_ROWS = 8
UP_ROWS = 40


def _tail_sample_kernel(x_ref, oa_ref, sga_ref, ups_ref, sgb_ref, sma_ref, smb_ref, dws_ref, *rest,
                        n_new):
    w, (y_ref, c_ref) = rest[:9], rest[9:]
    n_batch = ups_ref.shape[0]
    c_ref[...] = jnp.zeros(c_ref.shape, F32)

    def body(b, carry):
        up = ups_ref[b]
        for t in range(n_new):
            c_ref[pl.ds(b * S_ROWS + t, 1), :] = jnp.sum(up * dws_ref[t], axis=0, keepdims=True)
        return carry

    lax.fori_loop(0, n_batch, body, 0)
    y_ref[...] = _tail_math(x_ref[...], oa_ref[...], sga_ref[...], c_ref[...], sgb_ref[...], sma_ref[...],
                            smb_ref[...], w)


def _tail_sample(x, o_a, sga, up_s, sgb, sma, smb, dw_shift, tail_w, n_new):
    ins = [x, o_a, sga, up_s, sgb, sma, smb, dw_shift] + list(tail_w)
    m = x.shape[0]
    return pl.pallas_call(
        functools.partial(_tail_sample_kernel, n_new=n_new),
        grid=(1,),
        in_specs=[_const_spec(a.shape) for a in ins],
        out_specs=_const_spec((m, D_MODEL)),
        out_shape=jax.ShapeDtypeStruct((m, D_MODEL), F32),
        scratch_shapes=[pltpu.VMEM((m, CONV_C), F32)],
        compiler_params=pltpu.CompilerParams(dimension_semantics=("arbitrary",),
                                             vmem_limit_bytes=VMEM_LIMIT),
        name="tail_sample",
    )(*ins)


def _group_q(q_ref, g):
    q = q_ref[g * HPG:(g + 1) * HPG].reshape(HPG * S_ROWS, LANES).astype(F32)
    return q if g == 0 else pltpu.roll(q, HEAD_DIM, axis=1)


def _group_out(o, g):
    return o if g == 0 else pltpu.roll(o, HEAD_DIM, axis=1)


def _scmp_kernel(pt_ref, kpool_ref, vpool_ref, q_ref, perm_ref, pek_ref, w1k_ref, w2k_ref, pev_ref, w1v_ref,
                 w2v_ref, agg_ref, oc_ref, score_ref, ak_ref, av_ref, kbuf_ref, vbuf_ref, sem_ref,
                 *, n_pages_step, n_steps, past_len):
    p = n_pages_step
    b, pg = pl.program_id(0), pl.program_id(1)
    step = b * n_steps + pg
    n_total = pl.num_programs(0) * n_steps
    chunks_page = PAGE_SIZE // CMP_STRIDE
    rows = chunks_page * p

    def page_copies(step_idx, slot):
        bb = step_idx // n_steps
        first = (step_idx - bb * n_steps) * p
        for i in range(p):
            page = pt_ref[bb, first + i]
            for pool_ref, buf_ref in ((kpool_ref, kbuf_ref), (vpool_ref, vbuf_ref)):
                yield pltpu.make_async_copy(pool_ref.at[page], buf_ref.at[slot, i], sem_ref.at[slot])

    @pl.when(step == 0)
    def _():
        for cp in page_copies(step, 0):
            cp.start()

    @pl.when(step + 1 < n_total)
    def _():
        for cp in page_copies(step + 1, (step + 1) % 2):
            cp.start()

    slot = step % 2
    for cp in page_copies(step, slot):
        cp.wait()

    for buf_ref, w1_ref, a_ref in ((kbuf_ref, w1k_ref, ak_ref), (vbuf_ref, w1v_ref, av_ref)):
        blocks = []
        for i in range(p):
            y = _dot_nt(perm_ref[...], buf_ref[slot, i].astype(BF16))
            blocks.append(jnp.concatenate(
                [y[j * chunks_page:(j + 1) * chunks_page] for j in range(CMP_STRIDE)], axis=-1))
        xc = jnp.concatenate(blocks, axis=0).astype(BF16)
        a_ref[pl.ds(pl.multiple_of(pg * rows, rows), rows), :] = _dot(xc, w1_ref[...])

    @pl.when(pg == n_steps - 1)
    def _():
        kc = _compress_from_partials(ak_ref[...], pek_ref, w1k_ref, w2k_ref).astype(BF16)
        vc = _compress_from_partials(av_ref[...], pev_ref, w1v_ref, w2v_ref).astype(BF16)
        n_chunks = kc.shape[0]
        n_blocks = past_len // SLC_LEN + 1
        r32 = HPG * S_ROWS
        qpos32 = past_len + (_iota((r32, n_chunks), 0) & (S_ROWS - 1))
        cvalid = (_iota((r32, n_chunks), 1) * CMP_STRIDE + (CMP_LEN - 1)) <= qpos32
        nb_pad = agg_ref.shape[1]
        jl = _iota((S_ROWS, nb_pad), 1)
        qpos8 = past_len + _iota((S_ROWS, nb_pad), 0)
        for g in range(N_KV):
            qg = _group_q(q_ref, g).astype(BF16)
            p_c = _masked_softmax(_dot_nt(qg, kc), cvalid)
            oc_ref[g] = _group_out(_dot(p_c.astype(BF16), vc), g)
            psum = (p_c[0:S_ROWS] + p_c[S_ROWS:2 * S_ROWS] + p_c[2 * S_ROWS:3 * S_ROWS]
                    + p_c[3 * S_ROWS:4 * S_ROWS])
            imp = _dot(psum.astype(BF16), agg_ref[...])
            score_ref[g] = _block_scores(imp, jl, qpos8, n_blocks)


def _scmp(page_table, kpool, vpool, qp_s, cmp_w, agg_s, past_len):
    n_batch, n_pages = page_table.shape
    p = PAGES_PER_STEP
    n_steps = n_pages // p
    n_chunks = n_pages * (PAGE_SIZE // CMP_STRIDE)

    def cspec(shape):
        nd = len(shape)
        return pl.BlockSpec(shape, lambda b, s, pt: (0,) * nd)

    r_out = jnp.arange(PAGE_SIZE)
    chunks_page = PAGE_SIZE // CMP_STRIDE
    src = (r_out % chunks_page) * CMP_STRIDE + r_out // chunks_page
    perm = (src[:, None] == jnp.arange(PAGE_SIZE)[None, :]).astype(BF16)

    in_specs = ([pl.BlockSpec(memory_space=pl.ANY), pl.BlockSpec(memory_space=pl.ANY),
                 pl.BlockSpec((None, N_HEADS, S_ROWS, LANES), lambda b, s, pt: (0, 0, b, 0))]
                + [cspec(perm.shape)] + [cspec(w.shape) for w in cmp_w] + [cspec(agg_s.shape)])
    page_buf = pltpu.VMEM((2, p, KV_W, PAGE_SIZE), F32)
    nb_pad = agg_s.shape[1]
    out_specs = [pl.BlockSpec((None, N_KV, HPG * S_ROWS, LANES), lambda b, s, pt: (b, 0, 0, 0)),
                 pl.BlockSpec((None, N_KV, S_ROWS, nb_pad), lambda b, s, pt: (b, 0, 0, 0))]
    out_shape = [jax.ShapeDtypeStruct((n_batch, N_KV, HPG * S_ROWS, LANES), F32),
                 jax.ShapeDtypeStruct((n_batch, N_KV, S_ROWS, nb_pad), F32)]
    oc, score = pl.pallas_call(
        functools.partial(_scmp_kernel, n_pages_step=p, n_steps=n_steps, past_len=past_len),
        grid_spec=pltpu.PrefetchScalarGridSpec(
            num_scalar_prefetch=1, grid=(n_batch, n_steps), in_specs=in_specs, out_specs=out_specs,
            scratch_shapes=[pltpu.VMEM((n_chunks, 2 * KV_W), F32)] * 2
            + [page_buf, page_buf, pltpu.SemaphoreType.DMA((2,))]),
        out_shape=out_shape,
        compiler_params=pltpu.CompilerParams(dimension_semantics=("arbitrary", "arbitrary"),
                                             vmem_limit_bytes=VMEM_LIMIT),
        name="sample_cmp",
    )(page_table, kpool, vpool, qp_s, perm, *cmp_w, agg_s)
    return oc, _top_blocks(score.reshape(n_batch * N_KV * S_ROWS, nb_pad))


def _top_blocks_kernel(score_ref, idx_ref):
    score = score_ref[...]
    rows, nb_pad = score.shape
    jlf = _iota((rows, nb_pad), 1).astype(F32)
    lane = _iota((rows, LANES), 1)
    idx = jnp.zeros((rows, LANES), F32)
    for k in range(TOP_N):
        mx = jnp.max(score, axis=-1, keepdims=True)
        am = jnp.min(jnp.where(score == mx, jlf, float(nb_pad)), axis=-1, keepdims=True)
        idx = jnp.where(lane == k, am, idx)
        score = jnp.where(jlf == am, -3.0, score)
    idx_ref[...] = idx.astype(jnp.int32)


def _top_blocks(score):
    rows = score.shape[0]
    return pl.pallas_call(
        _top_blocks_kernel,
        grid=(1,),
        in_specs=[_const_spec(score.shape)],
        out_specs=_const_spec((rows, LANES)),
        out_shape=jax.ShapeDtypeStruct((rows, LANES), jnp.int32),
        compiler_params=pltpu.CompilerParams(dimension_semantics=("arbitrary",), vmem_limit_bytes=VMEM_LIMIT),
        name="sample_topk",
    )(score)


def _ssel_kernel(pt_ref, idx_ref, kpool_ref, vpool_ref, q_ref, ksn_ref, vsn_ref, kwn_ref, vwn_ref,
                 kwb_ref, vwb_ref, gn_ref, oc_ref, o_ref, kbuf_ref, vbuf_ref, sem_ref,
                 *, n_new, past_len):
    b = pl.program_id(0)
    n_pool_blocks = past_len // SLC_LEN
    blocks_page = PAGE_SIZE // SLC_LEN

    page_shift = blocks_page.bit_length() - 1
    assert blocks_page == 1 << page_shift

    def block_copy(pool_ref, buf_ref, sem, g, qi, k):
        j = jnp.minimum(idx_ref[b, (g * n_new + qi) * TOP_N + k], n_pool_blocks - 1)
        page = pt_ref[b, lax.shift_right_logical(j, page_shift)]
        return pltpu.make_async_copy(
            pool_ref.at[page, pl.ds(g * HEAD_DIM, HEAD_DIM), :],
            buf_ref.at[g, :, pl.ds((qi * TOP_N + k) * PAGE_SIZE, PAGE_SIZE)], sem)

    slots = [(g, qi, k) for g in range(N_KV) for qi in range(n_new) for k in range(TOP_N)]
    for g, qi, k in slots:
        block_copy(kpool_ref, kbuf_ref, sem_ref.at[0], g, qi, k).start()
        block_copy(vpool_ref, vbuf_ref, sem_ref.at[1], g, qi, k).start()

    r32 = HPG * S_ROWS
    n_sel = n_new * TOP_N * PAGE_SIZE
    new_rows = LANES
    newcol = _iota((r32, new_rows), 1)
    qrow_new = _iota((r32, new_rows), 0) & (S_ROWS - 1)

    def pad_new(ref):
        return jnp.concatenate([ref[...], jnp.zeros((new_rows - S_ROWS, KV_W), F32)], axis=0).astype(BF16)

    ks_new, vs_new, kw_new, vw_new = pad_new(ksn_ref), pad_new(vsn_ref), pad_new(kwn_ref), pad_new(vwn_ref)

    w_buf = kwb_ref.shape[1]
    wpos = jnp.concatenate([past_len - w_buf + _iota((r32, w_buf), 1), past_len + newcol], axis=1)
    qpos_w = past_len + (_iota((r32, w_buf + new_rows), 0) & (S_ROWS - 1))
    wmask = (wpos <= qpos_w) & (wpos >= qpos_w - WINDOW)
    wmask = wmask & jnp.concatenate([jnp.full((r32, w_buf), True), newcol < n_new], axis=1)
    kw_t = kwb_ref[...].astype(BF16)
    vw_t = vwb_ref[...].astype(BF16)
    gates = _sigmoid(gn_ref[...])

    o_w = []
    q_g = []
    for g in range(N_KV):
        qg = _group_q(q_ref, g).astype(BF16)
        q_g.append(qg)
        s_w = jnp.concatenate([_dot(qg, kw_t), _dot_nt(qg, kw_new)], axis=1)
        p_w = _masked_softmax(s_w, wmask).astype(BF16)
        o_w.append(_group_out(_dot_nt(p_w[:, :w_buf], vw_t) + _dot(p_w[:, w_buf:], vw_new), g))

    for g, qi, k in slots:
        block_copy(kpool_ref, kbuf_ref, sem_ref.at[0], g, qi, k).wait()
        block_copy(vpool_ref, vbuf_ref, sem_ref.at[1], g, qi, k).wait()

    pad64 = jnp.zeros((r32, LANES - HEAD_DIM), F32)
    row_qi = _iota((r32, PAGE_SIZE), 0) & (S_ROWS - 1)
    lane_part = lax.shift_right_logical(_iota((r32, PAGE_SIZE), 1), SLC_LEN.bit_length() - 1)
    for g in range(N_KV):
        q64 = q_ref[g * HPG:(g + 1) * HPG].reshape(r32, LANES)[:, :HEAD_DIM]
        pieces = []
        new_limit = jnp.zeros((r32, new_rows), jnp.int32)
        for qi in range(n_new):
            n_hit = jnp.int32(0)
            for k in range(TOP_N):
                j = idx_ref[b, (g * n_new + qi) * TOP_N + k]
                part = jnp.where(j < n_pool_blocks, j & (blocks_page - 1), -1)
                pieces.append((row_qi == qi) & (lane_part == part))
                n_hit = n_hit + jnp.where(j == n_pool_blocks, 1, 0)
            new_limit = jnp.where(qrow_new == qi, jnp.where(n_hit > 0, n_new, 0), new_limit)
        nmask = (newcol <= qrow_new) & (newcol < new_limit)
        s_s = jnp.concatenate([_dot(q64, kbuf_ref[g].astype(BF16)), _dot_nt(q_g[g], ks_new)], axis=1)
        p_s = _masked_softmax(s_s, jnp.concatenate(pieces + [nmask], axis=1)).astype(BF16)
        o_pool = _dot_nt(p_s[:, :n_sel], vbuf_ref[g].astype(BF16))
        o_s = jnp.concatenate([o_pool, pad64], axis=1) + _group_out(_dot(p_s[:, n_sel:], vs_new), g)
        o_c = oc_ref[g]
        heads = []
        for h in range(HPG):
            col = (g * HPG + h) * 3
            rs = slice(h * S_ROWS, (h + 1) * S_ROWS)
            heads.append(gates[:, col:col + 1] * o_c[rs] + gates[:, col + 1:col + 2] * o_s[rs]
                         + gates[:, col + 2:col + 3] * o_w[g][rs])
        for pr in range(HPG // 2):
            c0 = (g * HPG + 2 * pr) * HEAD_DIM
            o_ref[:, c0:c0 + LANES] = _pair_heads(heads[2 * pr], heads[2 * pr + 1])


def _ssel(page_table, idx, kpool, vpool, qp_s, ks_n, vs_n, kw_n, vw_n, kw_buf, vw_buf, gn_s, oc,
          n_new, past_len):
    n_batch = page_table.shape[0]
    w_buf = kw_buf.shape[2]
    rows8 = pl.BlockSpec((S_ROWS, KV_W), lambda b, pt, ix: (b, 0))
    wbuf_spec = pl.BlockSpec((None, KV_W, w_buf), lambda b, pt, ix: (b, 0, 0))
    in_specs = [pl.BlockSpec(memory_space=pl.ANY), pl.BlockSpec(memory_space=pl.ANY),
                pl.BlockSpec((None, N_HEADS, S_ROWS, LANES), lambda b, pt, ix: (0, 0, b, 0)),
                rows8, rows8, rows8, rows8, wbuf_spec, wbuf_spec,
                pl.BlockSpec((S_ROWS, LANES), lambda b, pt, ix: (b, 0)),
                pl.BlockSpec((None, N_KV, HPG * S_ROWS, LANES), lambda b, pt, ix: (b, 0, 0, 0))]
    return pl.pallas_call(
        functools.partial(_ssel_kernel, n_new=n_new, past_len=past_len),
        grid_spec=pltpu.PrefetchScalarGridSpec(
            num_scalar_prefetch=2, grid=(n_batch,), in_specs=in_specs,
            out_specs=pl.BlockSpec((S_ROWS, NSA_W), lambda b, pt, ix: (b, 0)),
            scratch_shapes=[pltpu.VMEM((N_KV, HEAD_DIM, n_new * TOP_N * PAGE_SIZE), F32)] * 2
            + [pltpu.SemaphoreType.DMA((2,))]),
        out_shape=jax.ShapeDtypeStruct((n_batch * S_ROWS, NSA_W), F32),
        compiler_params=pltpu.CompilerParams(dimension_semantics=("arbitrary",),
                                             vmem_limit_bytes=VMEM_LIMIT),
        name="sample_sel",
    )(page_table, idx, kpool, vpool, qp_s, ks_n, vs_n, kw_n, vw_n, kw_buf, vw_buf, gn_s, oc)


def _prep_w_in(w_in):
    q = w_in[:, :NSA_W].reshape(D_MODEL, N_HEADS, HEAD_DIM)
    q = jnp.pad(q, ((0, 0), (0, 0), (0, LANES - HEAD_DIM))).reshape(D_MODEL, N_HEADS * LANES)
    c = NSA_W
    kv = w_in[:, c:c + 6 * KV_W]
    c += 6 * KV_W
    gn = jnp.pad(w_in[:, c:c + 3 * N_HEADS], ((0, 0), (0, LANES - 3 * N_HEADS)))
    c += 3 * N_HEADS
    rest = w_in[:, c:]
    return jnp.concatenate([q, kv, gn, rest], axis=1).astype(BF16)


def _prep_cmp(pe, w1, w2):
    w1r = w1.reshape(CMP_R, CMP_STRIDE, HEAD_DIM, HEAD_DIM)
    eye = jnp.eye(N_KV, dtype=w1.dtype)
    w1big = jnp.einsum('rjdh,ge->jgdreh', w1r, eye).reshape(CMP_STRIDE * KV_W, CMP_R * KV_W)
    pe_rows = jnp.broadcast_to(pe.reshape(CMP_R, CMP_STRIDE, 1, HEAD_DIM),
                               (CMP_R, CMP_STRIDE, N_KV, HEAD_DIM)).reshape(CMP_R, CMP_STRIDE * KV_W)
    pe_rows = jnp.pad(pe_rows, ((0, SUBLANES - CMP_R), (0, 0)))
    w2big = jnp.einsum('hd,ge->ghed', w2, eye).reshape(KV_W, KV_W)
    return pe_rows.astype(BF16), w1big.astype(BF16), w2big.astype(BF16)


def _agg_matrix(n_rows, n_blocks, n_cols, col0):
    i = jnp.arange(n_rows)[:, None] * CMP_STRIDE
    jj = jnp.arange(n_cols)[None, :] - col0
    hit = (i < jj * SLC_LEN + SLC_LEN) & (i + CMP_LEN > jj * SLC_LEN) & (jj >= 0) & (jj < n_blocks)
    return hit.astype(BF16)


def kernel(x_prompt, x_sample, cache_k_cmp, cache_v_cmp, cache_k_slc, cache_v_slc, cache_k_win, cache_v_win, state_conv, page_table, ln_g, w_in, pe_k, w1_k, w2_k, pe_v, w1_v, w2_v, dw_k, dw_b, cln_g, cln_b, pw_w, pw_b, w_pa, w_pb, w_o, final_g):
    assert w_in.shape[0] == 1, "single-layer stack"
    n_b, t_len, _ = x_prompt.shape
    n_db, n_new, _ = x_sample.shape
    n_pages = page_table.shape[1]
    past_len = n_pages * PAGE_SIZE
    w_buf = cache_k_win.shape[2]
    assert n_new <= S_ROWS and t_len % TQ == 0 and t_len >= WINDOW + TQ
    assert n_new < CMP_STRIDE, "the new rows must not complete a compression chunk"
    assert t_len // SLC_LEN <= LANES - HEAD_DIM and past_len % PAGE_SIZE == 0

    w_all = _prep_w_in(w_in[0])
    lng = ln_g[0].reshape(1, D_MODEL)
    cmp_w = _prep_cmp(pe_k[0], w1_k[0], w2_k[0]) + _prep_cmp(pe_v[0], w1_v[0], w2_v[0])
    tail_w = (dw_b[0].reshape(1, CONV_C), cln_g[0].reshape(1, CONV_C), cln_b[0].reshape(1, CONV_C),
              pw_w[0].astype(BF16), pw_b[0].reshape(1, CONV_C), w_pa[0].astype(BF16), w_pb[0].astype(BF16),
              w_o[0].astype(BF16), final_g.reshape(1, D_MODEL))
    dw = jnp.pad(dw_k[0], ((0, 1), (0, 0)))

    (qt, kct, vct, kst, vst, kwt, vwt, gnt, sga, u, sgb, sma, smb, k_c, v_c, ksa, vst_g, kwp, vwt_g) = _proj(
        x_prompt.reshape(n_b * t_len, D_MODEL), lng, w_all, n_b, t_len, (0, 1), True)
    seq = lambda a: a.reshape(n_b, t_len, a.shape[-1])
    kcm, vcmt = _cmp_prompt(seq(k_c), seq(v_c), cmp_w)
    n_cmp_rows = t_len // CMP_STRIDE
    aggt_p = _agg_matrix(n_cmp_rows, t_len // SLC_LEN, LANES, 0).T
    o_a = _attn_prompt(qt, gnt, kcm, vcmt, ksa, vst_g, kwp, vwt_g, aggt_p)
    y_prompt = _tail_prompt(x_prompt, o_a, seq(sga), seq(u), seq(sgb), seq(sma), seq(smb), dw, tail_w)

    xs = jnp.pad(x_sample, ((0, 0), (0, S_ROWS - n_new), (0, 0))).reshape(n_db * S_ROWS, D_MODEL)
    (qp_s, kct_s, vct_s, kst_s, vst_s, kwt_s, vwt_s, gn_s, sga_s, u_s, sgb_s, sma_s, smb_s,
     k_s_s, v_s_s, k_w_s, v_w_s) = _proj(xs, lng, w_all, 1, n_db * S_ROWS, (2, 3, 4, 5), False)
    pool = lambda c: jnp.transpose(c[0], (0, 2, 3, 1)).reshape(c.shape[1], KV_W, PAGE_SIZE)
    n_blocks_s = past_len // SLC_LEN + 1
    nb_pad = -(-n_blocks_s // LANES) * LANES
    agg_s = _agg_matrix(past_len // CMP_STRIDE, n_blocks_s, nb_pad, 0)
    agg_s = agg_s.at[past_len // CMP_STRIDE - 1].set(0)
    oc_s, idx_s = _scmp(page_table, pool(cache_k_cmp), pool(cache_v_cmp), qp_s, cmp_w, agg_s, past_len)
    idx_flat = idx_s.reshape(n_db, N_KV, S_ROWS, LANES)[:, :, :n_new, :TOP_N].reshape(n_db, N_KV * n_new * TOP_N)
    win = lambda c: jnp.transpose(c[0], (0, 2, 3, 1)).reshape(n_db, KV_W, w_buf)
    o_a_s = _ssel(page_table, idx_flat, pool(cache_k_slc), pool(cache_v_slc), qp_s, k_s_s, v_s_s, k_w_s,
                  v_w_s, win(cache_k_win), win(cache_v_win), gn_s, oc_s, n_new, past_len)
    u_new = u_s.reshape(n_db, S_ROWS, CONV_C)[:, :n_new]
    up_s = jnp.concatenate([state_conv[0], u_new], axis=1)
    up_pad = jnp.pad(up_s, ((0, 0), (0, UP_ROWS - up_s.shape[1]), (0, 0)))
    dw_shift = jnp.stack([jnp.pad(dw_k[0], ((t, UP_ROWS - CONV_W - t), (0, 0))) for t in range(n_new)])
    y_s = _tail_sample(xs, o_a_s, sga_s, up_pad, sgb_s, sma_s, smb_s, dw_shift, tail_w, n_new)
    y_sample = y_s.reshape(n_db, S_ROWS, D_MODEL)[:, :n_new]

    def p_rows(at):
        return jnp.transpose(at.reshape(n_b, N_KV, HEAD_DIM, at.shape[-1]), (0, 3, 1, 2))[None]

    def s_rows(at):
        a = at.reshape(N_KV, HEAD_DIM, n_db, S_ROWS)[:, :, :, :n_new]
        return jnp.transpose(a, (2, 3, 0, 1))[None]

    win_p = min(WINDOW, t_len)
    p_conv = seq(u)[:, -(CONV_W - 1):][None]
    s_k_win = jnp.concatenate([cache_k_win[0], s_rows(kwt_s)[0]], axis=1)[:, -w_buf:][None]
    s_v_win = jnp.concatenate([cache_v_win[0], s_rows(vwt_s)[0]], axis=1)[:, -w_buf:][None]
    s_conv = up_s[:, -(CONV_W - 1):][None]
    return (y_prompt, y_sample, p_rows(kct), p_rows(vct), p_rows(kst), p_rows(vst),
            p_rows(kwt[:, :, -win_p:]), p_rows(vwt[:, :, -win_p:]), p_conv,
            s_rows(kct_s), s_rows(vct_s), s_rows(kst_s), s_rows(vst_s), s_k_win, s_v_win, s_conv)
```

```python
import functools

import jax
import jax.numpy as jnp
from jax import lax
from jax.experimental import pallas as pl
from jax.experimental.pallas import tpu as pltpu

F32 = jnp.float32
BF16 = jnp.bfloat16

D_MODEL = 1024
N_HEADS = 8
HEAD_DIM = 64
N_KV = 2
HPG = N_HEADS // N_KV
NSA_W = N_HEADS * HEAD_DIM
KV_W = N_KV * HEAD_DIM
CMP_LEN = 32
CMP_STRIDE = 16
CMP_R = CMP_LEN // CMP_STRIDE
SLC_LEN = 64
TOP_N = 16
WINDOW = 512
CONV_C = 512
CONV_W = 31
PAGE_SIZE = 128
EPS = 1e-6
FORCED_SCORE = 1e4
NEG_INF = -1e30
MASK_BIAS = -1e4

LANES = 128
SUBLANES = 8
VMEM_LIMIT = 56 * 1024 * 1024

C_Q = 0
C_KV = C_Q + N_HEADS * LANES
C_GN = C_KV + 6 * KV_W
C_GA = C_GN + LANES
C_GLU = C_GA + NSA_W
C_GB = C_GLU + 2 * CONV_C
C_MRG = C_GB + CONV_C
C_END = C_MRG + 2 * D_MODEL

TM_PROJ = 256
TQ = 256
TM_TAIL = 256
PAGES_PER_STEP = 32


def _sigmoid(x):
    return jax.nn.sigmoid(x)


def _silu(x):
    return jax.nn.silu(x)


def _dot(a, b):
    return jnp.dot(a, b, preferred_element_type=F32)


def _dot_nt(a, b):
    return lax.dot_general(a, b, (((1,), (1,)), ((), ())), preferred_element_type=F32)


def _iota(shape, dim):
    return lax.broadcasted_iota(jnp.int32, shape, dim)


def _const_spec(shape):
    nd = len(shape)
    return pl.BlockSpec(shape, lambda *_: (0,) * nd)


def _proj_kernel(x_ref, lng_ref, w_ref, qp_ref, kct_ref, vct_ref, kst_ref, vst_ref, kwt_ref, vwt_ref,
                 gn_ref, sga_ref, u_ref, sgb_ref, sma_ref, smb_ref, *more_refs, tiles_per_batch, row_ids,
                 attn_ops):
    x = x_ref[...]
    tm = x.shape[0]
    xn = x * lax.rsqrt(jnp.mean(x * x, axis=-1, keepdims=True) + EPS) * lng_ref[...]
    xn = xn.astype(BF16)

    zq = _dot(xn, w_ref[:, C_Q:C_KV]) * (HEAD_DIM ** -0.5 * (LOG2_E if attn_ops else 1.0))
    zgn = _dot(xn, w_ref[:, C_GN:C_GA])
    for h in range(N_HEADS):
        qh = zq[:, h * LANES:(h + 1) * LANES]
        qp_ref[h] = (qh.T if attn_ops else qh).astype(BF16)
    gn_ref[...] = zgn.T if attn_ops else zgn

    zkv = _dot(xn, w_ref[:, C_KV:C_GN])
    kv = [zkv[:, i * KV_W:(i + 1) * KV_W] for i in range(6)]
    kvt = [v.T for v in kv]
    for r, v in zip((kct_ref, vct_ref, kst_ref, vst_ref, kwt_ref, vwt_ref), kvt):
        r[...] = v
    for r, i in zip(more_refs[:len(row_ids)], row_ids):
        r[...] = kv[i]

    if attn_ops:
        ksa_ref, vst_g_ref, kwp_ref, vwt_g_ref = more_refs[len(row_ids):]
        t0 = (pl.program_id(0) % tiles_per_batch) * tm
        lane = _iota((tm, LANES), 1)
        blk = (t0 + _iota((tm, LANES), 0)) >> 6
        onehot = jnp.where((lane - HEAD_DIM) == blk, 1.0, 0.0)
        ones_row = jnp.where(_iota((V_ROWS - HEAD_DIM, tm), 0) == 0, 1.0, 0.0)
        for g in range(N_KV):
            rows_g = slice(g * HEAD_DIM, (g + 1) * HEAD_DIM)

            def grp(v):
                return v if g == 0 else pltpu.roll(v, HEAD_DIM, axis=1)

            ksa_ref[g] = jnp.where(lane < HEAD_DIM, grp(kv[2]), onehot).astype(BF16)
            kwp_ref[g] = grp(kv[4]).astype(BF16)
            for ref, vt in ((vst_g_ref, kvt[3]), (vwt_g_ref, kvt[5])):
                v_aug = jnp.concatenate([vt[rows_g], ones_row], axis=0).astype(BF16)
                for j in range(tm // KEY_CHUNK):
                    ref[g, j] = v_aug[:, j * KEY_CHUNK:(j + 1) * KEY_CHUNK]

    sga_ref[...] = _silu(_dot(xn, w_ref[:, C_GA:C_GLU]))
    zglu = _dot(xn, w_ref[:, C_GLU:C_GB])
    u_ref[...] = zglu[:, :CONV_C] * _sigmoid(zglu[:, CONV_C:])
    sgb_ref[...] = _silu(_dot(xn, w_ref[:, C_GB:C_MRG]))
    zm = _dot(xn, w_ref[:, C_MRG:C_END])
    sma_ref[...] = _sigmoid(zm[:, :D_MODEL])
    smb_ref[...] = _sigmoid(zm[:, D_MODEL:])


def _proj(x2d, ln_g, w_all, n_batch, t_len, row_ids, attn_ops):
    m = x2d.shape[0]
    tm = min(TM_PROJ, t_len)
    tpb = t_len // tm
    grid = (m // tm,)

    def row_spec(width):
        return pl.BlockSpec((tm, width), lambda i: (i, 0))

    def bt_spec(lead, width):
        return pl.BlockSpec((None, lead, tm, width), lambda i: (i // tpb, 0, i % tpb, 0))

    t_spec = pl.BlockSpec((None, KV_W, tm), lambda i: (i // tpb, 0, i % tpb))
    if attn_ops:
        out_shape = [jax.ShapeDtypeStruct((n_batch, N_HEADS, LANES, t_len), BF16)]
        out_specs = [pl.BlockSpec((None, N_HEADS, LANES, tm), lambda i: (i // tpb, 0, 0, i % tpb))]
    else:
        out_shape = [jax.ShapeDtypeStruct((n_batch, N_HEADS, t_len, LANES), BF16)]
        out_specs = [bt_spec(N_HEADS, LANES)]
    out_shape += [jax.ShapeDtypeStruct((n_batch, KV_W, t_len), F32)] * 6
    out_specs += [t_spec] * 6
    if attn_ops:
        out_shape.append(jax.ShapeDtypeStruct((n_batch, LANES, t_len), F32))
        out_specs.append(t_spec)
    else:
        out_shape.append(jax.ShapeDtypeStruct((m, LANES), F32))
        out_specs.append(row_spec(LANES))
    for width in (NSA_W, CONV_C, CONV_C, D_MODEL, D_MODEL):
        out_shape.append(jax.ShapeDtypeStruct((m, width), F32))
        out_specs.append(row_spec(width))
    out_shape += [jax.ShapeDtypeStruct((m, KV_W), F32)] * len(row_ids)
    out_specs += [row_spec(KV_W)] * len(row_ids)
    if attn_ops:
        cpt = tm // KEY_CHUNK
        k_shape = jax.ShapeDtypeStruct((n_batch, N_KV, t_len, LANES), BF16)
        vt_shape = jax.ShapeDtypeStruct((n_batch, N_KV, t_len // KEY_CHUNK, V_ROWS, KEY_CHUNK), BF16)
        vt_spec = pl.BlockSpec((None, N_KV, cpt, V_ROWS, KEY_CHUNK), lambda i: (i // tpb, 0, i % tpb, 0, 0))
        out_shape += [k_shape, vt_shape, k_shape, vt_shape]
        out_specs += [bt_spec(N_KV, LANES), vt_spec, bt_spec(N_KV, LANES), vt_spec]

    return pl.pallas_call(
        functools.partial(_proj_kernel, tiles_per_batch=tpb, row_ids=row_ids, attn_ops=attn_ops),
        grid=grid,
        in_specs=[row_spec(D_MODEL), _const_spec((1, D_MODEL)), _const_spec((D_MODEL, C_END))],
        out_specs=out_specs,
        out_shape=out_shape,
        compiler_params=pltpu.CompilerParams(dimension_semantics=("arbitrary",),
                                             vmem_limit_bytes=VMEM_LIMIT),
        name="proj",
    )(x2d, ln_g, w_all)


def _compress_from_partials(per, pe_ref, w1_ref, w2_ref):
    n = per.shape[0]
    hp = _dot(pe_ref[...], w1_ref[...])
    h0 = hp[0:1, :KV_W] + hp[1:2, KV_W:]
    h = h0 + per[:, :KV_W] + pltpu.roll(per[:, KV_W:], n - 1, axis=0)
    return _dot(_silu(h).astype(BF16), w2_ref[...])


def _chunk_rows(ref, n_chunks):
    return jnp.concatenate([ref[pl.ds(j, n_chunks, stride=CMP_STRIDE), :] for j in range(CMP_STRIDE)],
                           axis=-1)


def _cmp_prompt_kernel(kc_ref, vc_ref, pek_ref, w1k_ref, w2k_ref, pev_ref, w1v_ref, w2v_ref,
                       ok_ref, ov_ref):
    n_chunks = kc_ref.shape[0] // CMP_STRIDE
    for x_ref, pe_ref, w1_ref, w2_ref, o_ref, transposed in (
            (kc_ref, pek_ref, w1k_ref, w2k_ref, ok_ref, False), (vc_ref, pev_ref, w1v_ref, w2v_ref, ov_ref, True)):
        per = _dot(_chunk_rows(x_ref, n_chunks).astype(BF16), w1_ref[...])
        c = _compress_from_partials(per, pe_ref, w1_ref, w2_ref)
        for g, cg in enumerate((c, pltpu.roll(c, HEAD_DIM, axis=1))):
            o_ref[g] = (cg.T if transposed else cg).astype(BF16)


def _cmp_prompt(k_c, v_c, cmp_w):
    n_batch, t_len, _ = k_c.shape
    n_chunks = t_len // CMP_STRIDE
    seq = pl.BlockSpec((None, t_len, KV_W), lambda b: (b, 0, 0))
    out = pl.BlockSpec((None, N_KV, n_chunks, KV_W), lambda b: (b, 0, 0, 0))
    w_specs = [_const_spec(w.shape) for w in cmp_w]
    return pl.pallas_call(
        _cmp_prompt_kernel,
        grid=(n_batch,),
        in_specs=[seq, seq] + w_specs,
        out_specs=[out, out],
        out_shape=[jax.ShapeDtypeStruct((n_batch, N_KV, n_chunks, KV_W), BF16)] * 2,
        compiler_params=pltpu.CompilerParams(dimension_semantics=("arbitrary",),
                                             vmem_limit_bytes=VMEM_LIMIT),
        name="cmp_prompt",
    )(k_c, v_c, *cmp_w)


def _masked_softmax(s, mask):
    s = jnp.where(mask, s, NEG_INF)
    e = jnp.where(mask, jnp.exp(s - jnp.max(s, axis=-1, keepdims=True)), 0.0)
    return e / jnp.maximum(jnp.sum(e, axis=-1, keepdims=True), 1e-30)


def _softmax_rows(s, mask):
    s = jnp.where(mask, s, NEG_INF)
    e = jnp.where(mask, jnp.exp2(s - jnp.max(s, axis=0, keepdims=True)), 0.0)
    return e / jnp.maximum(jnp.sum(e, axis=0, keepdims=True), 1e-30)


def _block_scores(imp, j, qpos, n_blocks):
    cur = qpos >> 6
    forced = (j == 0) | (j == cur) | (j == cur - 1)
    score = jnp.where(forced, FORCED_SCORE, imp)
    score = jnp.where(j * SLC_LEN <= qpos, score, -1.0)
    return jnp.where((j >= 0) & (j < n_blocks), score, -2.0)


def _pair_heads(o_even, o_odd):
    lane = _iota(o_even.shape, 1)
    return jnp.where(lane < HEAD_DIM, o_even, pltpu.roll(o_odd, HEAD_DIM, axis=1))


KEY_CHUNK = 128
V_ROWS = LANES
LOG2_E = 1.4426950408889634
SEL_ROWS = 32


def _attn_prompt_kernel(qt_ref, gnt_ref, kcm_ref, vcmt_ref, ksa_ref, vst_ref, kwp_ref, vwt_ref, aggt_ref,
                        o_ref, oacc_ref, qsel_ref, m_ref, acc_ref, *, t_len):
    tq = gnt_ref.shape[1]
    n_blocks = t_len // SLC_LEN
    i = pl.program_id(1)
    t0 = i * tq
    chunks_tile = tq // KEY_CHUNK

    gates = _sigmoid(gnt_ref[...])
    qpos_c = t0 + _iota((KV_W, tq), 1)
    cvalid = (_iota((KV_W, tq), 0) * CMP_STRIDE + (CMP_LEN - 1)) <= qpos_c
    jblk = _iota((SEL_ROWS, tq), 0)
    qpos_j = t0 + _iota((SEL_ROWS, tq), 1)
    row_minus_lane = _iota((KEY_CHUNK, tq), 0) - _iota((KEY_CHUNK, tq), 1)

    def gate(hh, branch):
        r = hh * 3 + branch
        return gates[r:r + 1, :]

    def flash(q_ref, k_ref, vt_ref, t_lo, banded, branch):
        def step(c, hh, diagonal):
            g = hh // HPG
            kc = k_ref[g, pl.ds(pl.multiple_of(c * KEY_CHUNK, KEY_CHUNK), KEY_CHUNK), :]
            s = _dot(kc, q_ref[hh])
            off = c * KEY_CHUNK - t0
            if diagonal:
                s = jnp.where(row_minus_lane + off <= 0, s, NEG_INF)
            elif banded:
                s = jnp.where(row_minus_lane + off >= -WINDOW, s, NEG_INF)
            m = m_ref[hh]
            m_new = jnp.maximum(m, jnp.max(s, axis=0, keepdims=True))
            p = jnp.exp2(s - m_new).astype(BF16)
            acc_ref[hh] = jnp.exp2(m - m_new) * acc_ref[hh] + _dot(vt_ref[g, c], p)
            m_ref[hh] = m_new

        m_ref[...] = jnp.full(m_ref.shape, NEG_INF, F32)
        acc_ref[...] = jnp.zeros(acc_ref.shape, F32)

        def tile_steps(t, diagonal):
            for cd in range(chunks_tile):
                for hh in range(N_HEADS):
                    step(t * chunks_tile + cd, hh, diagonal)

        n_pairs = lax.shift_right_logical(i - t_lo, 1)

        def body(u, carry):
            tile_steps(t_lo + 2 * u, False)
            tile_steps(t_lo + 2 * u + 1, False)
            return carry

        lax.fori_loop(0, n_pairs, body, 0)

        @pl.when(t_lo + 2 * n_pairs < i)
        def _():
            tile_steps(i - 1, False)

        tile_steps(i, True)
        for hh in range(N_HEADS):
            acc = acc_ref[hh]
            o_t = acc[0:HEAD_DIM, :] / acc[HEAD_DIM:HEAD_DIM + 1, :]
            oacc_ref[hh] = oacc_ref[hh] + gate(hh, branch) * o_t

    for g in range(N_KV):
        psum = None
        for h in range(HPG):
            hh = g * HPG + h
            p_c = _softmax_rows(_dot(kcm_ref[g], qt_ref[hh]), cvalid)
            oacc_ref[hh] = gate(hh, 0) * _dot(vcmt_ref[g, 0:HEAD_DIM, :], p_c.astype(BF16))
            psum = p_c if psum is None else psum + p_c
        imp = _dot(aggt_ref[...], psum.astype(BF16))[:SEL_ROWS]
        score = _block_scores(imp, jblk, qpos_j, n_blocks)
        rank = jnp.zeros((SEL_ROWS, tq), jnp.int32)
        for jp in range(n_blocks):
            other = score[jp:jp + 1, :]
            ahead = (other > score) | ((other == score) & (jblk > jp))
            rank = rank + jnp.where(ahead, 1, 0)
        allowed = (rank < TOP_N) & (jblk < n_blocks) & (jblk * SLC_LEN <= qpos_j)
        bias = jnp.where(allowed, 0.0, MASK_BIAS).astype(BF16)
        pad = jnp.zeros((LANES - HEAD_DIM - SEL_ROWS, tq), BF16)
        for h in range(HPG):
            qsel_ref[g * HPG + h] = jnp.concatenate([qt_ref[g * HPG + h, 0:HEAD_DIM, :], bias, pad], axis=0)

    flash(qsel_ref, ksa_ref, vst_ref, 0, False, 1)
    flash(qt_ref, kwp_ref, vwt_ref, jnp.maximum(i - WINDOW // tq, 0), True, 2)
    for pr in range(N_HEADS // 2):
        o_ref[:, pr * LANES:(pr + 1) * LANES] = jnp.concatenate([oacc_ref[2 * pr], oacc_ref[2 * pr + 1]], axis=0).T


def _attn_prompt(qt, gnt, kcm, vcmt, ksa, vst, kwp, vwt, aggt):
    n_batch, _, _, t_len = qt.shape
    tq = TQ
    n_cmp = kcm.shape[2]
    assert tq % KEY_CHUNK == 0 and WINDOW % KEY_CHUNK == 0 and t_len // SLC_LEN <= SEL_ROWS
    assert n_cmp == KV_W, "one lane tile of compressed blocks"
    k_spec = pl.BlockSpec((None, N_KV, t_len, LANES), lambda b, i: (b, 0, 0, 0))
    vt_spec = pl.BlockSpec((None, N_KV, t_len // KEY_CHUNK, V_ROWS, KEY_CHUNK), lambda b, i: (b, 0, 0, 0, 0))
    cmp_spec = pl.BlockSpec((None, N_KV, n_cmp, KV_W), lambda b, i: (b, 0, 0, 0))
    return pl.pallas_call(
        functools.partial(_attn_prompt_kernel, t_len=t_len),
        grid=(n_batch, t_len // tq),
        in_specs=[pl.BlockSpec((None, N_HEADS, LANES, tq), lambda b, i: (b, 0, 0, i)),
                  pl.BlockSpec((None, LANES, tq), lambda b, i: (b, 0, i)),
                  cmp_spec, cmp_spec, k_spec, vt_spec, k_spec, vt_spec, _const_spec(aggt.shape)],
        out_specs=pl.BlockSpec((None, tq, NSA_W), lambda b, i: (b, i, 0)),
        out_shape=jax.ShapeDtypeStruct((n_batch, t_len, NSA_W), F32),
        scratch_shapes=[pltpu.VMEM((N_HEADS, HEAD_DIM, tq), F32), pltpu.VMEM((N_HEADS, LANES, tq), BF16),
                        pltpu.VMEM((N_HEADS, 1, tq), F32), pltpu.VMEM((N_HEADS, V_ROWS, tq), F32)],
        compiler_params=pltpu.CompilerParams(dimension_semantics=("arbitrary", "arbitrary"),
                                             vmem_limit_bytes=VMEM_LIMIT),
        name="attn_prompt",
    )(qt, gnt, kcm, vcmt, ksa, vst, kwp, vwt, aggt)


def _tail_math(x, o_a, sga, c_pre, sgb, sma, smb, w):
    (dwb_ref, clg_ref, clb_ref, pww_ref, pwb_ref, wpa_ref, wpb_ref, wo_ref, fg_ref) = w
    c = c_pre + dwb_ref[...]
    mu = jnp.mean(c, axis=-1, keepdims=True)
    var = jnp.mean(jnp.square(c - mu), axis=-1, keepdims=True)
    cn = (c - mu) * lax.rsqrt(var + EPS) * clg_ref[...] + clb_ref[...]
    cp = _dot(_silu(cn).astype(BF16), pww_ref[...]) + pwb_ref[...]
    br_a = _dot((o_a * sga).astype(BF16), wpa_ref[...])
    br_b = _dot((cp * sgb).astype(BF16), wpb_ref[...])
    h = sma * br_a + smb * br_b
    xo = x + _dot(h.astype(BF16), wo_ref[...])
    return xo * lax.rsqrt(jnp.mean(xo * xo, axis=-1, keepdims=True) + EPS) * fg_ref[...]


HIST = 32


def _tail_prompt_kernel(x_ref, oa_ref, sga_ref, u_ref, sgb_ref, sma_ref, smb_ref, dw_ref, *rest):
    w, (y_ref, up_ref, sh_ref) = rest[:9], rest[9:]
    tm = x_ref.shape[0]

    @pl.when(pl.program_id(1) == 0)
    def _():
        up_ref[0:HIST, :] = jnp.zeros((HIST, CONV_C), F32)

    up_ref[HIST:HIST + tm, :] = u_ref[...]
    first = HIST - (CONV_W - 1)
    n_sh = HIST + tm - SUBLANES
    acc = jnp.zeros((tm, CONV_C), F32)
    for phase in range(SUBLANES):
        tiles = [(first + k) // SUBLANES for k in range(CONV_W) if (first + k) % SUBLANES == phase]
        if not tiles:
            continue
        if phase:
            sh_ref[phase, 0:n_sh, :] = up_ref[pl.ds(phase, n_sh), :]
        for a in tiles:
            k = a * SUBLANES + phase - first
            rows = up_ref[pl.ds(a * SUBLANES, tm), :] if phase == 0 else sh_ref[phase, pl.ds(a * SUBLANES, tm), :]
            acc = acc + rows * dw_ref[k:k + 1, :]
    up_ref[0:HIST, :] = up_ref[tm:tm + HIST, :]
    y_ref[...] = _tail_math(x_ref[...], oa_ref[...], sga_ref[...], acc, sgb_ref[...], sma_ref[...],
                            smb_ref[...], w)


def _tail_prompt(x, o_a, sga, u, sgb, sma, smb, dw, tail_w):
    n_batch, t_len, _ = x.shape
    tm = TM_TAIL

    def bt(width):
        return pl.BlockSpec((None, tm, width), lambda b, t: (b, t, 0))

    ins = [x, o_a, sga, u, sgb, sma, smb]
    return pl.pallas_call(
        _tail_prompt_kernel,
        grid=(n_batch, t_len // tm),
        in_specs=[bt(a.shape[-1]) for a in ins] + [_const_spec(dw.shape)] + [_const_spec(a.shape) for a in tail_w],
        out_specs=bt(D_MODEL),
        out_shape=jax.ShapeDtypeStruct((n_batch, t_len, D_MODEL), F32),
        scratch_shapes=[pltpu.VMEM((HIST + tm, CONV_C), F32), pltpu.VMEM((SUBLANES, HIST + tm, CONV_C), F32)],
        compiler_params=pltpu.CompilerParams(dimension_semantics=("arbitrary", "arbitrary"),
                                             vmem_limit_bytes=VMEM_LIMIT),
        name="tail_prompt",
    )(*ins, dw, *tail_w)


S_ROWS = 8
UP_ROWS = 40


def _tail_sample_kernel(x_ref, oa_ref, sga_ref, ups_ref, sgb_ref, sma_ref, smb_ref, dws_ref, *rest,
                        n_new):
    w, (y_ref, c_ref) = rest[:9], rest[9:]
    n_batch = ups_ref.shape[0]
    c_ref[...] = jnp.zeros(c_ref.shape, F32)

    def body(b, carry):
        up = ups_ref[b]
        for t in range(n_new):
            c_ref[pl.ds(b * S_ROWS + t, 1), :] = jnp.sum(up * dws_ref[t], axis=0, keepdims=True)
        return carry

    lax.fori_loop(0, n_batch, body, 0)
    y_ref[...] = _tail_math(x_ref[...], oa_ref[...], sga_ref[...], c_ref[...], sgb_ref[...], sma_ref[...],
                            smb_ref[...], w)


def _tail_sample(x, o_a, sga, up_s, sgb, sma, smb, dw_shift, tail_w, n_new):
    ins = [x, o_a, sga, up_s, sgb, sma, smb, dw_shift] + list(tail_w)
    m = x.shape[0]
    return pl.pallas_call(
        functools.partial(_tail_sample_kernel, n_new=n_new),
        grid=(1,),
        in_specs=[_const_spec(a.shape) for a in ins],
        out_specs=_const_spec((m, D_MODEL)),
        out_shape=jax.ShapeDtypeStruct((m, D_MODEL), F32),
        scratch_shapes=[pltpu.VMEM((m, CONV_C), F32)],
        compiler_params=pltpu.CompilerParams(dimension_semantics=("arbitrary",),
                                             vmem_limit_bytes=VMEM_LIMIT),
        name="tail_sample",
    )(*ins)


def _group_q(q_ref, g):
    q = q_ref[g * HPG:(g + 1) * HPG].reshape(HPG * S_ROWS, LANES).astype(F32)
    return q if g == 0 else pltpu.roll(q, HEAD_DIM, axis=1)


def _group_out(o, g):
    return o if g == 0 else pltpu.roll(o, HEAD_DIM, axis=1)


def _scmp_kernel(pt_ref, kpool_ref, vpool_ref, q_ref, perm_ref, pek_ref, w1k_ref, w2k_ref, pev_ref, w1v_ref,
                 w2v_ref, agg_ref, oc_ref, score_ref, ak_ref, av_ref, kbuf_ref, vbuf_ref, sem_ref,
                 *, n_pages_step, n_steps, past_len):
    p = n_pages_step
    b, pg = pl.program_id(0), pl.program_id(1)
    step = b * n_steps + pg
    n_total = pl.num_programs(0) * n_steps
    chunks_page = PAGE_SIZE // CMP_STRIDE
    rows = chunks_page * p

    def page_copies(step_idx, slot, lookup):
        if lookup:
            bb = step_idx // n_steps
            first = (step_idx - bb * n_steps) * p
        for i in range(p):
            page = pt_ref[bb, first + i] if lookup else 0
            for pool_ref, buf_ref in ((kpool_ref, kbuf_ref), (vpool_ref, vbuf_ref)):
                yield pltpu.make_async_copy(pool_ref.at[page], buf_ref.at[slot, i], sem_ref.at[slot])

    @pl.when(step == 0)
    def _():
        for cp in page_copies(step, 0, True):
            cp.start()

    @pl.when(step + 1 < n_total)
    def _():
        for cp in page_copies(step + 1, (step + 1) % 2, True):
            cp.start()

    slot = step % 2
    for cp in page_copies(step, slot, False):
        cp.wait()

    for buf_ref, w1_ref, a_ref in ((kbuf_ref, w1k_ref, ak_ref), (vbuf_ref, w1v_ref, av_ref)):
        blocks = []
        for i in range(p):
            y = _dot_nt(perm_ref[...], buf_ref[slot, i].astype(BF16))
            blocks.append(jnp.concatenate(
                [y[j * chunks_page:(j + 1) * chunks_page] for j in range(CMP_STRIDE)], axis=-1))
        xc = jnp.concatenate(blocks, axis=0).astype(BF16)
        a_ref[pl.ds(pl.multiple_of(pg * rows, rows), rows), :] = _dot(xc, w1_ref[...])

    @pl.when(pg == n_steps - 1)
    def _():
        kc = _compress_from_partials(ak_ref[...], pek_ref, w1k_ref, w2k_ref).astype(BF16)
        vc = _compress_from_partials(av_ref[...], pev_ref, w1v_ref, w2v_ref).astype(BF16)
        n_chunks = kc.shape[0]
        n_blocks = past_len // SLC_LEN + 1
        r32 = HPG * S_ROWS
        qpos32 = past_len + (_iota((r32, n_chunks), 0) & (S_ROWS - 1))
        cvalid = (_iota((r32, n_chunks), 1) * CMP_STRIDE + (CMP_LEN - 1)) <= qpos32
        nb_pad = agg_ref.shape[1]
        jl = _iota((S_ROWS, nb_pad), 1)
        qpos8 = past_len + _iota((S_ROWS, nb_pad), 0)
        for g in range(N_KV):
            qg = _group_q(q_ref, g).astype(BF16)
            p_c = _masked_softmax(_dot_nt(qg, kc), cvalid)
            oc_ref[g] = _group_out(_dot(p_c.astype(BF16), vc), g)
            psum = (p_c[0:S_ROWS] + p_c[S_ROWS:2 * S_ROWS] + p_c[2 * S_ROWS:3 * S_ROWS]
                    + p_c[3 * S_ROWS:4 * S_ROWS])
            imp = _dot(psum.astype(BF16), agg_ref[...])
            score_ref[g] = _block_scores(imp, jl, qpos8, n_blocks)


def _scmp(page_table, kpool, vpool, qp_s, cmp_w, agg_s, past_len):
    n_batch, n_pages = page_table.shape
    p = PAGES_PER_STEP
    n_steps = n_pages // p
    n_chunks = n_pages * (PAGE_SIZE // CMP_STRIDE)

    def cspec(shape):
        nd = len(shape)
        return pl.BlockSpec(shape, lambda b, s, pt: (0,) * nd)

    r_out = jnp.arange(PAGE_SIZE)
    chunks_page = PAGE_SIZE // CMP_STRIDE
    src = (r_out % chunks_page) * CMP_STRIDE + r_out // chunks_page
    perm = (src[:, None] == jnp.arange(PAGE_SIZE)[None, :]).astype(BF16)

    in_specs = ([pl.BlockSpec(memory_space=pl.ANY), pl.BlockSpec(memory_space=pl.ANY),
                 pl.BlockSpec((None, N_HEADS, S_ROWS, LANES), lambda b, s, pt: (0, 0, b, 0))]
                + [cspec(perm.shape)] + [cspec(w.shape) for w in cmp_w] + [cspec(agg_s.shape)])
    page_buf = pltpu.VMEM((2, p, KV_W, PAGE_SIZE), F32)
    nb_pad = agg_s.shape[1]
    out_specs = [pl.BlockSpec((None, N_KV, HPG * S_ROWS, LANES), lambda b, s, pt: (b, 0, 0, 0)),
                 pl.BlockSpec((None, N_KV, S_ROWS, nb_pad), lambda b, s, pt: (b, 0, 0, 0))]
    out_shape = [jax.ShapeDtypeStruct((n_batch, N_KV, HPG * S_ROWS, LANES), F32),
                 jax.ShapeDtypeStruct((n_batch, N_KV, S_ROWS, nb_pad), F32)]
    oc, score = pl.pallas_call(
        functools.partial(_scmp_kernel, n_pages_step=p, n_steps=n_steps, past_len=past_len),
        grid_spec=pltpu.PrefetchScalarGridSpec(
            num_scalar_prefetch=1, grid=(n_batch, n_steps), in_specs=in_specs, out_specs=out_specs,
            scratch_shapes=[pltpu.VMEM((n_chunks, 2 * KV_W), F32)] * 2
            + [page_buf, page_buf, pltpu.SemaphoreType.DMA((2,))]),
        out_shape=out_shape,
        compiler_params=pltpu.CompilerParams(dimension_semantics=("arbitrary", "arbitrary"),
                                             vmem_limit_bytes=VMEM_LIMIT),
        name="sample_cmp",
    )(page_table, kpool, vpool, qp_s, perm, *cmp_w, agg_s)
    return oc, _top_blocks(score.reshape(n_batch * N_KV * S_ROWS, nb_pad))


def _top_blocks_kernel(score_ref, idx_ref):
    score = score_ref[...]
    rows, nb_pad = score.shape
    jlf = _iota((rows, nb_pad), 1).astype(F32)
    lane = _iota((rows, LANES), 1)
    idx = jnp.zeros((rows, LANES), F32)
    for k in range(TOP_N):
        mx = jnp.max(score, axis=-1, keepdims=True)
        am = jnp.min(jnp.where(score == mx, jlf, float(nb_pad)), axis=-1, keepdims=True)
        idx = jnp.where(lane == k, am, idx)
        score = jnp.where(jlf == am, -3.0, score)
    idx_ref[...] = idx.astype(jnp.int32)


def _top_blocks(score):
    rows = score.shape[0]
    return pl.pallas_call(
        _top_blocks_kernel,
        grid=(1,),
        in_specs=[_const_spec(score.shape)],
        out_specs=_const_spec((rows, LANES)),
        out_shape=jax.ShapeDtypeStruct((rows, LANES), jnp.int32),
        compiler_params=pltpu.CompilerParams(dimension_semantics=("arbitrary",), vmem_limit_bytes=VMEM_LIMIT),
        name="sample_topk",
    )(score)


def _ssel_kernel(pt_ref, idx_ref, kpool_ref, vpool_ref, q_ref, ksn_ref, vsn_ref, kwn_ref, vwn_ref,
                 kwb_ref, vwb_ref, gn_ref, oc_ref, o_ref, kbuf_ref, vbuf_ref, sem_ref,
                 *, n_new, past_len):
    b = pl.program_id(0)
    n_pool_blocks = past_len // SLC_LEN
    blocks_page = PAGE_SIZE // SLC_LEN

    page_shift = blocks_page.bit_length() - 1
    assert blocks_page == 1 << page_shift

    def block_copies(bb, ring, lookup):
        for g in range(N_KV):
            for qi in range(n_new):
                for k in range(TOP_N):
                    page = 0
                    if lookup:
                        j = jnp.minimum(idx_ref[bb, (g * n_new + qi) * TOP_N + k], n_pool_blocks - 1)
                        page = pt_ref[bb, lax.shift_right_logical(j, page_shift)]
                    for pool_ref, buf_ref in ((kpool_ref, kbuf_ref), (vpool_ref, vbuf_ref)):
                        yield pltpu.make_async_copy(
                            pool_ref.at[page, pl.ds(g * HEAD_DIM, HEAD_DIM), :],
                            buf_ref.at[ring, g, :, pl.ds((qi * TOP_N + k) * PAGE_SIZE, PAGE_SIZE)],
                            sem_ref.at[ring])

    @pl.when(b == 0)
    def _():
        for cp in block_copies(b, 0, True):
            cp.start()

    @pl.when(b + 1 < pl.num_programs(0))
    def _():
        for cp in block_copies(b + 1, (b + 1) % 2, True):
            cp.start()

    ring = b % 2

    r32 = HPG * S_ROWS
    n_sel = n_new * TOP_N * PAGE_SIZE
    new_rows = LANES
    newcol = _iota((r32, new_rows), 1)
    qrow_new = _iota((r32, new_rows), 0) & (S_ROWS - 1)

    def pad_new(ref):
        return jnp.concatenate([ref[...], jnp.zeros((new_rows - S_ROWS, KV_W), F32)], axis=0).astype(BF16)

    ks_new, vs_new, kw_new, vw_new = pad_new(ksn_ref), pad_new(vsn_ref), pad_new(kwn_ref), pad_new(vwn_ref)

    w_buf = kwb_ref.shape[1]
    wpos = jnp.concatenate([past_len - w_buf + _iota((r32, w_buf), 1), past_len + newcol], axis=1)
    qpos_w = past_len + (_iota((r32, w_buf + new_rows), 0) & (S_ROWS - 1))
    wmask = (wpos <= qpos_w) & (wpos >= qpos_w - WINDOW)
    wmask = wmask & jnp.concatenate([jnp.full((r32, w_buf), True), newcol < n_new], axis=1)
    kw_t = kwb_ref[...].astype(BF16)
    vw_t = vwb_ref[...].astype(BF16)
    gates = _sigmoid(gn_ref[...])

    o_w = []
    q_g = []
    for g in range(N_KV):
        qg = _group_q(q_ref, g).astype(BF16)
        q_g.append(qg)
        s_w = jnp.concatenate([_dot(qg, kw_t), _dot_nt(qg, kw_new)], axis=1)
        p_w = _masked_softmax(s_w, wmask).astype(BF16)
        o_w.append(_group_out(_dot_nt(p_w[:, :w_buf], vw_t) + _dot(p_w[:, w_buf:], vw_new), g))

    for cp in block_copies(b, ring, False):
        cp.wait()

    pad64 = jnp.zeros((r32, LANES - HEAD_DIM), F32)
    row_qi = _iota((r32, PAGE_SIZE), 0) & (S_ROWS - 1)
    lane_part = lax.shift_right_logical(_iota((r32, PAGE_SIZE), 1), SLC_LEN.bit_length() - 1)
    for g in range(N_KV):
        q64 = q_ref[g * HPG:(g + 1) * HPG].reshape(r32, LANES)[:, :HEAD_DIM]
        pieces = []
        new_limit = jnp.zeros((r32, new_rows), jnp.int32)
        for qi in range(n_new):
            n_hit = jnp.int32(0)
            for k in range(TOP_N):
                j = idx_ref[b, (g * n_new + qi) * TOP_N + k]
                part = jnp.where(j < n_pool_blocks, j & (blocks_page - 1), -1)
                pieces.append((row_qi == qi) & (lane_part == part))
                n_hit = n_hit + jnp.where(j == n_pool_blocks, 1, 0)
            new_limit = jnp.where(qrow_new == qi, jnp.where(n_hit > 0, n_new, 0), new_limit)
        nmask = (newcol <= qrow_new) & (newcol < new_limit)
        s_s = jnp.concatenate([_dot(q64, kbuf_ref[ring, g].astype(BF16)), _dot_nt(q_g[g], ks_new)], axis=1)
        p_s = _masked_softmax(s_s, jnp.concatenate(pieces + [nmask], axis=1)).astype(BF16)
        o_pool = _dot_nt(p_s[:, :n_sel], vbuf_ref[ring, g].astype(BF16))
        o_s = jnp.concatenate([o_pool, pad64], axis=1) + _group_out(_dot(p_s[:, n_sel:], vs_new), g)
        o_c = oc_ref[g]
        heads = []
        for h in range(HPG):
            col = (g * HPG + h) * 3
            rs = slice(h * S_ROWS, (h + 1) * S_ROWS)
            heads.append(gates[:, col:col + 1] * o_c[rs] + gates[:, col + 1:col + 2] * o_s[rs]
                         + gates[:, col + 2:col + 3] * o_w[g][rs])
        for pr in range(HPG // 2):
            c0 = (g * HPG + 2 * pr) * HEAD_DIM
            o_ref[:, c0:c0 + LANES] = _pair_heads(heads[2 * pr], heads[2 * pr + 1])


def _ssel(page_table, idx, kpool, vpool, qp_s, ks_n, vs_n, kw_n, vw_n, kw_buf, vw_buf, gn_s, oc,
          n_new, past_len):
    n_batch = page_table.shape[0]
    w_buf = kw_buf.shape[2]
    rows8 = pl.BlockSpec((S_ROWS, KV_W), lambda b, pt, ix: (b, 0))
    wbuf_spec = pl.BlockSpec((None, KV_W, w_buf), lambda b, pt, ix: (b, 0, 0))
    in_specs = [pl.BlockSpec(memory_space=pl.ANY), pl.BlockSpec(memory_space=pl.ANY),
                pl.BlockSpec((None, N_HEADS, S_ROWS, LANES), lambda b, pt, ix: (0, 0, b, 0)),
                rows8, rows8, rows8, rows8, wbuf_spec, wbuf_spec,
                pl.BlockSpec((S_ROWS, LANES), lambda b, pt, ix: (b, 0)),
                pl.BlockSpec((None, N_KV, HPG * S_ROWS, LANES), lambda b, pt, ix: (b, 0, 0, 0))]
    return pl.pallas_call(
        functools.partial(_ssel_kernel, n_new=n_new, past_len=past_len),
        grid_spec=pltpu.PrefetchScalarGridSpec(
            num_scalar_prefetch=2, grid=(n_batch,), in_specs=in_specs,
            out_specs=pl.BlockSpec((S_ROWS, NSA_W), lambda b, pt, ix: (b, 0)),
            scratch_shapes=[pltpu.VMEM((2, N_KV, HEAD_DIM, n_new * TOP_N * PAGE_SIZE), F32)] * 2
            + [pltpu.SemaphoreType.DMA((2,))]),
        out_shape=jax.ShapeDtypeStruct((n_batch * S_ROWS, NSA_W), F32),
        compiler_params=pltpu.CompilerParams(dimension_semantics=("arbitrary",),
                                             vmem_limit_bytes=VMEM_LIMIT),
        name="sample_sel",
    )(page_table, idx, kpool, vpool, qp_s, ks_n, vs_n, kw_n, vw_n, kw_buf, vw_buf, gn_s, oc)


def _prep_w_in(w_in):
    q = w_in[:, :NSA_W].reshape(D_MODEL, N_HEADS, HEAD_DIM)
    q = jnp.pad(q, ((0, 0), (0, 0), (0, LANES - HEAD_DIM))).reshape(D_MODEL, N_HEADS * LANES)
    c = NSA_W
    kv = w_in[:, c:c + 6 * KV_W]
    c += 6 * KV_W
    gn = jnp.pad(w_in[:, c:c + 3 * N_HEADS], ((0, 0), (0, LANES - 3 * N_HEADS)))
    c += 3 * N_HEADS
    rest = w_in[:, c:]
    return jnp.concatenate([q, kv, gn, rest], axis=1).astype(BF16)


def _prep_cmp(pe, w1, w2):
    w1r = w1.reshape(CMP_R, CMP_STRIDE, HEAD_DIM, HEAD_DIM)
    eye = jnp.eye(N_KV, dtype=w1.dtype)
    w1big = jnp.einsum('rjdh,ge->jgdreh', w1r, eye).reshape(CMP_STRIDE * KV_W, CMP_R * KV_W)
    pe_rows = jnp.broadcast_to(pe.reshape(CMP_R, CMP_STRIDE, 1, HEAD_DIM),
                               (CMP_R, CMP_STRIDE, N_KV, HEAD_DIM)).reshape(CMP_R, CMP_STRIDE * KV_W)
    pe_rows = jnp.pad(pe_rows, ((0, SUBLANES - CMP_R), (0, 0)))
    w2big = jnp.einsum('hd,ge->ghed', w2, eye).reshape(KV_W, KV_W)
    return pe_rows.astype(BF16), w1big.astype(BF16), w2big.astype(BF16)


def _agg_matrix(n_rows, n_blocks, n_cols, col0):
    i = jnp.arange(n_rows)[:, None] * CMP_STRIDE
    jj = jnp.arange(n_cols)[None, :] - col0
    hit = (i < jj * SLC_LEN + SLC_LEN) & (i + CMP_LEN > jj * SLC_LEN) & (jj >= 0) & (jj < n_blocks)
    return hit.astype(BF16)


def kernel(x_prompt, x_sample, cache_k_cmp, cache_v_cmp, cache_k_slc, cache_v_slc, cache_k_win, cache_v_win, state_conv, page_table, ln_g, w_in, pe_k, w1_k, w2_k, pe_v, w1_v, w2_v, dw_k, dw_b, cln_g, cln_b, pw_w, pw_b, w_pa, w_pb, w_o, final_g):
    assert w_in.shape[0] == 1, "single-layer stack"
    n_b, t_len, _ = x_prompt.shape
    n_db, n_new, _ = x_sample.shape
    n_pages = page_table.shape[1]
    past_len = n_pages * PAGE_SIZE
    w_buf = cache_k_win.shape[2]
    assert n_new <= S_ROWS and t_len % TQ == 0 and t_len >= WINDOW + TQ
    assert n_new < CMP_STRIDE, "the new rows must not complete a compression chunk"
    assert t_len // SLC_LEN <= LANES - HEAD_DIM and past_len % PAGE_SIZE == 0

    w_all = _prep_w_in(w_in[0])
    lng = ln_g[0].reshape(1, D_MODEL)
    cmp_w = _prep_cmp(pe_k[0], w1_k[0], w2_k[0]) + _prep_cmp(pe_v[0], w1_v[0], w2_v[0])
    tail_w = (dw_b[0].reshape(1, CONV_C), cln_g[0].reshape(1, CONV_C), cln_b[0].reshape(1, CONV_C),
              pw_w[0].astype(BF16), pw_b[0].reshape(1, CONV_C), w_pa[0].astype(BF16), w_pb[0].astype(BF16),
              w_o[0].astype(BF16), final_g.reshape(1, D_MODEL))
    dw = jnp.pad(dw_k[0], ((0, 1), (0, 0)))

    (qt, kct, vct, kst, vst, kwt, vwt, gnt, sga, u, sgb, sma, smb, k_c, v_c, ksa, vst_g, kwp, vwt_g) = _proj(
        x_prompt.reshape(n_b * t_len, D_MODEL), lng, w_all, n_b, t_len, (0, 1), True)
    seq = lambda a: a.reshape(n_b, t_len, a.shape[-1])
    kcm, vcmt = _cmp_prompt(seq(k_c), seq(v_c), cmp_w)
    n_cmp_rows = t_len // CMP_STRIDE
    aggt_p = _agg_matrix(n_cmp_rows, t_len // SLC_LEN, LANES, 0).T
    o_a = _attn_prompt(qt, gnt, kcm, vcmt, ksa, vst_g, kwp, vwt_g, aggt_p)
    y_prompt = _tail_prompt(x_prompt, o_a, seq(sga), seq(u), seq(sgb), seq(sma), seq(smb), dw, tail_w)

    xs = jnp.pad(x_sample, ((0, 0), (0, S_ROWS - n_new), (0, 0))).reshape(n_db * S_ROWS, D_MODEL)
    (qp_s, kct_s, vct_s, kst_s, vst_s, kwt_s, vwt_s, gn_s, sga_s, u_s, sgb_s, sma_s, smb_s,
     k_s_s, v_s_s, k_w_s, v_w_s) = _proj(xs, lng, w_all, 1, n_db * S_ROWS, (2, 3, 4, 5), False)
    pool = lambda c: jnp.transpose(c[0], (0, 2, 3, 1)).reshape(c.shape[1], KV_W, PAGE_SIZE)
    n_blocks_s = past_len // SLC_LEN + 1
    nb_pad = -(-n_blocks_s // LANES) * LANES
    agg_s = _agg_matrix(past_len // CMP_STRIDE, n_blocks_s, nb_pad, 0)
    agg_s = agg_s.at[past_len // CMP_STRIDE - 1].set(0)
    oc_s, idx_s = _scmp(page_table, pool(cache_k_cmp), pool(cache_v_cmp), qp_s, cmp_w, agg_s, past_len)
    idx_flat = idx_s.reshape(n_db, N_KV, S_ROWS, LANES)[:, :, :n_new, :TOP_N].reshape(n_db, N_KV * n_new * TOP_N)
    win = lambda c: jnp.transpose(c[0], (0, 2, 3, 1)).reshape(n_db, KV_W, w_buf)
    o_a_s = _ssel(page_table, idx_flat, pool(cache_k_slc), pool(cache_v_slc), qp_s, k_s_s, v_s_s, k_w_s,
                  v_w_s, win(cache_k_win), win(cache_v_win), gn_s, oc_s, n_new, past_len)
    u_new = u_s.reshape(n_db, S_ROWS, CONV_C)[:, :n_new]
    up_s = jnp.concatenate([state_conv[0], u_new], axis=1)
    up_pad = jnp.pad(up_s, ((0, 0), (0, UP_ROWS - up_s.shape[1]), (0, 0)))
    dw_shift = jnp.stack([jnp.pad(dw_k[0], ((t, UP_ROWS - CONV_W - t), (0, 0))) for t in range(n_new)])
    y_s = _tail_sample(xs, o_a_s, sga_s, up_pad, sgb_s, sma_s, smb_s, dw_shift, tail_w, n_new)
    y_sample = y_s.reshape(n_db, S_ROWS, D_MODEL)[:, :n_new]

    def p_rows(at):
        return jnp.transpose(at.reshape(n_b, N_KV, HEAD_DIM, at.shape[-1]), (0, 3, 1, 2))[None]

    def s_rows(at):
        a = at.reshape(N_KV, HEAD_DIM, n_db, S_ROWS)[:, :, :, :n_new]
        return jnp.transpose(a, (2, 3, 0, 1))[None]

    win_p = min(WINDOW, t_len)
    p_conv = seq(u)[:, -(CONV_W - 1):][None]
    s_k_win = jnp.concatenate([cache_k_win[0], s_rows(kwt_s)[0]], axis=1)[:, -w_buf:][None]
    s_v_win = jnp.concatenate([cache_v_win[0], s_rows(vwt_s)[0]], axis=1)[:, -w_buf:][None]
    s_conv = up_s[:, -(CONV_W - 1):][None]
    return (y_prompt, y_sample, p_rows(kct), p_rows(vct), p_rows(kst), p_rows(vst),
            p_rows(kwt[:, :, -win_p:]), p_rows(vwt[:, :, -win_p:]), p_conv,
            s_rows(kct_s), s_rows(vct_s), s_rows(kst_s), s_rows(vst_s), s_k_win, s_v_win, s_conv)
```

```python
import functools

import jax
import jax.numpy as jnp
from jax import lax
from jax.experimental import pallas as pl
from jax.experimental.pallas import tpu as pltpu

F32 = jnp.float32
BF16 = jnp.bfloat16

D_MODEL = 1024
N_HEADS = 8
HEAD_DIM = 64
N_KV = 2
HPG = N_HEADS // N_KV
NSA_W = N_HEADS * HEAD_DIM
KV_W = N_KV * HEAD_DIM
CMP_LEN = 32
CMP_STRIDE = 16
CMP_R = CMP_LEN // CMP_STRIDE
SLC_LEN = 64
TOP_N = 16
WINDOW = 512
CONV_C = 512
CONV_W = 31
PAGE_SIZE = 128
EPS = 1e-6
FORCED_SCORE = 1e4
NEG_INF = -1e30
MASK_BIAS = -1e4

LANES = 128
SUBLANES = 8
VMEM_LIMIT = 56 * 1024 * 1024

A_KV = NSA_W
A_END = A_KV + 6 * KV_W
R_GLU = NSA_W
R_GB = R_GLU + 2 * CONV_C
R_MRG = R_GB + CONV_C
R_END = R_MRG + 2 * D_MODEL

TM_PROJ = 256
TQ = 256
TM_TAIL = 256
PAGES_PER_STEP = 32


def _sigmoid(x):
    return jax.nn.sigmoid(x)


def _silu(x):
    return jax.nn.silu(x)


def _dot(a, b):
    return jnp.dot(a, b, preferred_element_type=F32)


def _dot_nt(a, b):
    return lax.dot_general(a, b, (((1,), (1,)), ((), ())), preferred_element_type=F32)


def _iota(shape, dim):
    return lax.broadcasted_iota(jnp.int32, shape, dim)


def _const_spec(shape):
    nd = len(shape)
    return pl.BlockSpec(shape, lambda *_: (0,) * nd)


def _proj_kernel(x_ref, lng_ref, wa_ref, wgn_ref, w_ref, qp_ref, kct_ref, vct_ref, kst_ref, vst_ref, kwt_ref,
                 vwt_ref, gn_ref, sga_ref, u_ref, sgb_ref, sma_ref, smb_ref, *more_refs, tiles_per_batch,
                 row_ids, attn_ops):
    x = x_ref[...]
    tm = x.shape[0]
    xn = x * lax.rsqrt(jnp.mean(x * x, axis=-1, keepdims=True) + EPS) * lng_ref[...]
    xn = xn.astype(BF16)

    zq = _dot(xn, wa_ref[:, :A_KV]) * (HEAD_DIM ** -0.5 * (LOG2_E if attn_ops else 1.0))
    zgn = _dot(xn, wgn_ref[...])
    lane = _iota((tm, LANES), 1)
    for pr in range(N_HEADS // 2):
        two = zq[:, pr * LANES:(pr + 1) * LANES]
        if attn_ops:
            two_t = two.T
            pad = jnp.zeros((LANES - HEAD_DIM, tm), F32)
            for o in range(2):
                qp_ref[2 * pr + o] = jnp.concatenate(
                    [two_t[o * HEAD_DIM:(o + 1) * HEAD_DIM], pad], axis=0).astype(BF16)
        else:
            qp_ref[2 * pr] = jnp.where(lane < HEAD_DIM, two, 0.0).astype(BF16)
            qp_ref[2 * pr + 1] = jnp.where(lane < HEAD_DIM, pltpu.roll(two, HEAD_DIM, axis=1), 0.0).astype(BF16)
    gn_ref[...] = zgn.T if attn_ops else zgn

    zkv = _dot(xn, wa_ref[:, A_KV:A_END])
    kv = [zkv[:, i * KV_W:(i + 1) * KV_W] for i in range(6)]
    kvt = [v.T for v in kv]
    for r, v in zip((kct_ref, vct_ref, kst_ref, vst_ref, kwt_ref, vwt_ref), kvt):
        r[...] = v
    for r, i in zip(more_refs[:len(row_ids)], row_ids):
        r[...] = kv[i]

    if attn_ops:
        ksa_ref, vst_g_ref, kwp_ref, vwt_g_ref = more_refs[len(row_ids):]
        t0 = (pl.program_id(0) % tiles_per_batch) * tm
        blk = (t0 + _iota((tm, LANES), 0)) >> 6
        onehot = jnp.where((lane - HEAD_DIM) == blk, 1.0, 0.0)
        ones_row = jnp.where(_iota((V_ROWS - HEAD_DIM, tm), 0) == 0, 1.0, 0.0)
        for g in range(N_KV):
            rows_g = slice(g * HEAD_DIM, (g + 1) * HEAD_DIM)

            def grp(v):
                return v if g == 0 else pltpu.roll(v, HEAD_DIM, axis=1)

            ksa_ref[g] = jnp.where(lane < HEAD_DIM, grp(kv[2]), onehot).astype(BF16)
            kwp_ref[g] = grp(kv[4]).astype(BF16)
            for ref, vt in ((vst_g_ref, kvt[3]), (vwt_g_ref, kvt[5])):
                v_aug = jnp.concatenate([vt[rows_g], ones_row], axis=0).astype(BF16)
                for j in range(tm // KEY_CHUNK):
                    ref[g, j] = v_aug[:, j * KEY_CHUNK:(j + 1) * KEY_CHUNK]

    sga_ref[...] = _silu(_dot(xn, w_ref[:, :R_GLU]))
    zglu = _dot(xn, w_ref[:, R_GLU:R_GB])
    u_ref[...] = zglu[:, :CONV_C] * _sigmoid(zglu[:, CONV_C:])
    sgb_ref[...] = _silu(_dot(xn, w_ref[:, R_GB:R_MRG]))
    zm = _dot(xn, w_ref[:, R_MRG:R_END])
    sma_ref[...] = _sigmoid(zm[:, :D_MODEL])
    smb_ref[...] = _sigmoid(zm[:, D_MODEL:])


def _proj(x2d, ln_g, w_all, n_batch, t_len, row_ids, attn_ops):
    m = x2d.shape[0]
    tm = min(TM_PROJ, t_len)
    tpb = t_len // tm
    grid = (m // tm,)

    def row_spec(width):
        return pl.BlockSpec((tm, width), lambda i: (i, 0))

    def bt_spec(lead, width):
        return pl.BlockSpec((None, lead, tm, width), lambda i: (i // tpb, 0, i % tpb, 0))

    t_spec = pl.BlockSpec((None, KV_W, tm), lambda i: (i // tpb, 0, i % tpb))
    if attn_ops:
        out_shape = [jax.ShapeDtypeStruct((n_batch, N_HEADS, LANES, t_len), BF16)]
        out_specs = [pl.BlockSpec((None, N_HEADS, LANES, tm), lambda i: (i // tpb, 0, 0, i % tpb))]
    else:
        out_shape = [jax.ShapeDtypeStruct((n_batch, N_HEADS, t_len, LANES), BF16)]
        out_specs = [bt_spec(N_HEADS, LANES)]
    out_shape += [jax.ShapeDtypeStruct((n_batch, KV_W, t_len), F32)] * 6
    out_specs += [t_spec] * 6
    if attn_ops:
        out_shape.append(jax.ShapeDtypeStruct((n_batch, LANES, t_len), F32))
        out_specs.append(t_spec)
    else:
        out_shape.append(jax.ShapeDtypeStruct((m, LANES), F32))
        out_specs.append(row_spec(LANES))
    for width in (NSA_W, CONV_C, CONV_C, D_MODEL, D_MODEL):
        out_shape.append(jax.ShapeDtypeStruct((m, width), F32))
        out_specs.append(row_spec(width))
    out_shape += [jax.ShapeDtypeStruct((m, KV_W), F32)] * len(row_ids)
    out_specs += [row_spec(KV_W)] * len(row_ids)
    if attn_ops:
        cpt = tm // KEY_CHUNK
        k_shape = jax.ShapeDtypeStruct((n_batch, N_KV, t_len, LANES), BF16)
        vt_shape = jax.ShapeDtypeStruct((n_batch, N_KV, t_len // KEY_CHUNK, V_ROWS, KEY_CHUNK), BF16)
        vt_spec = pl.BlockSpec((None, N_KV, cpt, V_ROWS, KEY_CHUNK), lambda i: (i // tpb, 0, i % tpb, 0, 0))
        out_shape += [k_shape, vt_shape, k_shape, vt_shape]
        out_specs += [bt_spec(N_KV, LANES), vt_spec, bt_spec(N_KV, LANES), vt_spec]

    return pl.pallas_call(
        functools.partial(_proj_kernel, tiles_per_batch=tpb, row_ids=row_ids, attn_ops=attn_ops),
        grid=grid,
        in_specs=[row_spec(D_MODEL), _const_spec((1, D_MODEL))] + [_const_spec(w.shape) for w in w_all],
        out_specs=out_specs,
        out_shape=out_shape,
        compiler_params=pltpu.CompilerParams(dimension_semantics=("arbitrary",),
                                             vmem_limit_bytes=VMEM_LIMIT),
        name="proj",
    )(x2d, ln_g, *w_all)


def _compress_from_partials(per, pe_ref, w1_ref, w2_ref):
    n = per.shape[0]
    hp = _dot(pe_ref[...], w1_ref[...])
    h0 = hp[0:1, :KV_W] + hp[1:2, KV_W:]
    h = h0 + per[:, :KV_W] + pltpu.roll(per[:, KV_W:], n - 1, axis=0)
    return _dot(_silu(h).astype(BF16), w2_ref[...])


def _chunk_rows(ref, n_chunks):
    return jnp.concatenate([ref[pl.ds(j, n_chunks, stride=CMP_STRIDE), :] for j in range(CMP_STRIDE)],
                           axis=-1)


def _cmp_prompt_kernel(kc_ref, vc_ref, pek_ref, w1k_ref, w2k_ref, pev_ref, w1v_ref, w2v_ref,
                       ok_ref, ov_ref):
    n_chunks = kc_ref.shape[0] // CMP_STRIDE
    for x_ref, pe_ref, w1_ref, w2_ref, o_ref, transposed in (
            (kc_ref, pek_ref, w1k_ref, w2k_ref, ok_ref, False), (vc_ref, pev_ref, w1v_ref, w2v_ref, ov_ref, True)):
        per = _dot(_chunk_rows(x_ref, n_chunks).astype(BF16), w1_ref[...])
        c = _compress_from_partials(per, pe_ref, w1_ref, w2_ref)
        for g, cg in enumerate((c, pltpu.roll(c, HEAD_DIM, axis=1))):
            o_ref[g] = (cg.T if transposed else cg).astype(BF16)


def _cmp_prompt(k_c, v_c, cmp_w):
    n_batch, t_len, _ = k_c.shape
    n_chunks = t_len // CMP_STRIDE
    seq = pl.BlockSpec((None, t_len, KV_W), lambda b: (b, 0, 0))
    out = pl.BlockSpec((None, N_KV, n_chunks, KV_W), lambda b: (b, 0, 0, 0))
    w_specs = [_const_spec(w.shape) for w in cmp_w]
    return pl.pallas_call(
        _cmp_prompt_kernel,
        grid=(n_batch,),
        in_specs=[seq, seq] + w_specs,
        out_specs=[out, out],
        out_shape=[jax.ShapeDtypeStruct((n_batch, N_KV, n_chunks, KV_W), BF16)] * 2,
        compiler_params=pltpu.CompilerParams(dimension_semantics=("arbitrary",),
                                             vmem_limit_bytes=VMEM_LIMIT),
        name="cmp_prompt",
    )(k_c, v_c, *cmp_w)


def _masked_softmax(s, mask):
    s = jnp.where(mask, s, NEG_INF)
    e = jnp.where(mask, jnp.exp(s - jnp.max(s, axis=-1, keepdims=True)), 0.0)
    return e / jnp.maximum(jnp.sum(e, axis=-1, keepdims=True), 1e-30)


def _softmax_rows(s, mask):
    s = jnp.where(mask, s, NEG_INF)
    e = jnp.where(mask, jnp.exp2(s - jnp.max(s, axis=0, keepdims=True)), 0.0)
    return e / jnp.maximum(jnp.sum(e, axis=0, keepdims=True), 1e-30)


def _block_scores(imp, j, qpos, n_blocks):
    cur = qpos >> 6
    forced = (j == 0) | (j == cur) | (j == cur - 1)
    score = jnp.where(forced, FORCED_SCORE, imp)
    score = jnp.where(j * SLC_LEN <= qpos, score, -1.0)
    return jnp.where((j >= 0) & (j < n_blocks), score, -2.0)


def _pair_heads(o_even, o_odd):
    lane = _iota(o_even.shape, 1)
    return jnp.where(lane < HEAD_DIM, o_even, pltpu.roll(o_odd, HEAD_DIM, axis=1))


KEY_CHUNK = 128
V_ROWS = LANES
LOG2_E = 1.4426950408889634
SEL_ROWS = 32


def _attn_prompt_kernel(qt_ref, gnt_ref, kcm_ref, vcmt_ref, ksa_ref, vst_ref, kwp_ref, vwt_ref, aggt_ref,
                        o_ref, oacc_ref, qsel_ref, m_ref, acc_ref, *, t_len):
    tq = gnt_ref.shape[1]
    n_blocks = t_len // SLC_LEN
    i = pl.program_id(1)
    t0 = i * tq
    chunks_tile = tq // KEY_CHUNK

    gates = _sigmoid(gnt_ref[...])
    qpos_c = t0 + _iota((KV_W, tq), 1)
    cvalid = (_iota((KV_W, tq), 0) * CMP_STRIDE + (CMP_LEN - 1)) <= qpos_c
    jblk = _iota((SEL_ROWS, tq), 0)
    qpos_j = t0 + _iota((SEL_ROWS, tq), 1)
    row_minus_lane = _iota((KEY_CHUNK, tq), 0) - _iota((KEY_CHUNK, tq), 1)

    def gate(hh, branch):
        r = hh * 3 + branch
        return gates[r:r + 1, :]

    flash_ops = ((qsel_ref, ksa_ref, vst_ref, 0, False),
                 (qt_ref, kwp_ref, vwt_ref, jnp.maximum(i - WINDOW // tq, 0), True))

    def step(br, c, hh, diagonal):
        q_ref, k_ref, vt_ref, _, banded = flash_ops[br]
        g, st = hh // HPG, br * N_HEADS + hh
        kc = k_ref[g, pl.ds(pl.multiple_of(c * KEY_CHUNK, KEY_CHUNK), KEY_CHUNK), :]
        s = _dot(kc, q_ref[hh])
        off = c * KEY_CHUNK - t0
        if diagonal:
            s = jnp.where(row_minus_lane + off <= 0, s, NEG_INF)
        elif banded:
            s = jnp.where(row_minus_lane + off >= -WINDOW, s, NEG_INF)
        m = m_ref[st]
        m_new = jnp.maximum(m, jnp.max(s, axis=0, keepdims=True))
        p = jnp.exp2(s - m_new).astype(BF16)
        acc_ref[st] = jnp.exp2(m - m_new) * acc_ref[st] + _dot(vt_ref[g, c], p)
        m_ref[st] = m_new

    def earlier_tiles(br):
        t_lo = flash_ops[br][3]

        def tile_steps(t):
            for cd in range(chunks_tile):
                for hh in range(N_HEADS):
                    step(br, t * chunks_tile + cd, hh, False)

        n_pairs = lax.shift_right_logical(i - t_lo, 1)

        def body(u, carry):
            tile_steps(t_lo + 2 * u)
            tile_steps(t_lo + 2 * u + 1)
            return carry

        lax.fori_loop(0, n_pairs, body, 0)

        @pl.when(t_lo + 2 * n_pairs < i)
        def _():
            tile_steps(i - 1)

    for g in range(N_KV):
        psum = None
        for h in range(HPG):
            hh = g * HPG + h
            p_c = _softmax_rows(_dot(kcm_ref[g], qt_ref[hh]), cvalid)
            oacc_ref[hh] = gate(hh, 0) * _dot(vcmt_ref[g, 0:HEAD_DIM, :], p_c.astype(BF16))
            psum = p_c if psum is None else psum + p_c
        imp = _dot(aggt_ref[...], psum.astype(BF16))[:SEL_ROWS]
        score = _block_scores(imp, jblk, qpos_j, n_blocks)
        rank = jnp.zeros((SEL_ROWS, tq), jnp.int32)
        for jp in range(n_blocks):
            other = score[jp:jp + 1, :]
            ahead = (other > score) | ((other == score) & (jblk > jp))
            rank = rank + jnp.where(ahead, 1, 0)
        allowed = (rank < TOP_N) & (jblk < n_blocks) & (jblk * SLC_LEN <= qpos_j)
        bias = jnp.where(allowed, 0.0, MASK_BIAS).astype(BF16)
        pad = jnp.zeros((LANES - HEAD_DIM - SEL_ROWS, tq), BF16)
        for h in range(HPG):
            qsel_ref[g * HPG + h] = jnp.concatenate([qt_ref[g * HPG + h, 0:HEAD_DIM, :], bias, pad], axis=0)

    m_ref[...] = jnp.full(m_ref.shape, NEG_INF, F32)
    acc_ref[...] = jnp.zeros(acc_ref.shape, F32)
    earlier_tiles(0)
    earlier_tiles(1)
    for cd in range(chunks_tile):
        for hh in range(N_HEADS):
            for br in range(2):
                step(br, i * chunks_tile + cd, hh, True)

    def head_out(hh):
        o = oacc_ref[hh]
        for br in range(2):
            acc = acc_ref[br * N_HEADS + hh]
            o = o + gate(hh, br + 1) * (acc[0:HEAD_DIM, :] / acc[HEAD_DIM:HEAD_DIM + 1, :])
        return o

    for pr in range(N_HEADS // 2):
        o_ref[:, pr * LANES:(pr + 1) * LANES] = jnp.concatenate([head_out(2 * pr), head_out(2 * pr + 1)], axis=0).T


def _attn_prompt(qt, gnt, kcm, vcmt, ksa, vst, kwp, vwt, aggt):
    n_batch, _, _, t_len = qt.shape
    tq = TQ
    n_cmp = kcm.shape[2]
    assert tq % KEY_CHUNK == 0 and WINDOW % KEY_CHUNK == 0 and t_len // SLC_LEN <= SEL_ROWS
    assert n_cmp == KV_W, "one lane tile of compressed blocks"
    k_spec = pl.BlockSpec((None, N_KV, t_len, LANES), lambda b, i: (b, 0, 0, 0))
    vt_spec = pl.BlockSpec((None, N_KV, t_len // KEY_CHUNK, V_ROWS, KEY_CHUNK), lambda b, i: (b, 0, 0, 0, 0))
    cmp_spec = pl.BlockSpec((None, N_KV, n_cmp, KV_W), lambda b, i: (b, 0, 0, 0))
    return pl.pallas_call(
        functools.partial(_attn_prompt_kernel, t_len=t_len),
        grid=(n_batch, t_len // tq),
        in_specs=[pl.BlockSpec((None, N_HEADS, LANES, tq), lambda b, i: (b, 0, 0, i)),
                  pl.BlockSpec((None, LANES, tq), lambda b, i: (b, 0, i)),
                  cmp_spec, cmp_spec, k_spec, vt_spec, k_spec, vt_spec, _const_spec(aggt.shape)],
        out_specs=pl.BlockSpec((None, tq, NSA_W), lambda b, i: (b, i, 0)),
        out_shape=jax.ShapeDtypeStruct((n_batch, t_len, NSA_W), F32),
        scratch_shapes=[pltpu.VMEM((N_HEADS, HEAD_DIM, tq), F32), pltpu.VMEM((N_HEADS, LANES, tq), BF16),
                        pltpu.VMEM((2 * N_HEADS, 1, tq), F32), pltpu.VMEM((2 * N_HEADS, V_ROWS, tq), F32)],
        compiler_params=pltpu.CompilerParams(dimension_semantics=("arbitrary", "arbitrary"),
                                             vmem_limit_bytes=VMEM_LIMIT),
        name="attn_prompt",
    )(qt, gnt, kcm, vcmt, ksa, vst, kwp, vwt, aggt)


def _tail_math(x, o_a, sga, c_pre, sgb, sma, smb, w):
    (dwb_ref, clg_ref, clb_ref, pww_ref, pwb_ref, wpa_ref, wpb_ref, wo_ref, fg_ref) = w
    c = c_pre + dwb_ref[...]
    mu = jnp.mean(c, axis=-1, keepdims=True)
    var = jnp.mean(jnp.square(c - mu), axis=-1, keepdims=True)
    cn = (c - mu) * lax.rsqrt(var + EPS) * clg_ref[...] + clb_ref[...]
    cp = _dot(_silu(cn).astype(BF16), pww_ref[...]) + pwb_ref[...]
    br_a = _dot((o_a * sga).astype(BF16), wpa_ref[...])
    br_b = _dot((cp * sgb).astype(BF16), wpb_ref[...])
    h = sma * br_a + smb * br_b
    xo = x + _dot(h.astype(BF16), wo_ref[...])
    return xo * lax.rsqrt(jnp.mean(xo * xo, axis=-1, keepdims=True) + EPS) * fg_ref[...]


HIST = 32


def _tail_prompt_kernel(x_ref, oa_ref, sga_ref, u_ref, sgb_ref, sma_ref, smb_ref, dw_ref, *rest):
    w, (y_ref, up_ref, sh_ref) = rest[:9], rest[9:]
    tm = x_ref.shape[0]

    @pl.when(pl.program_id(1) == 0)
    def _():
        up_ref[0:HIST, :] = jnp.zeros((HIST, CONV_C), F32)

    up_ref[HIST:HIST + tm, :] = u_ref[...]
    first = HIST - (CONV_W - 1)
    n_sh = HIST + tm - SUBLANES
    acc = jnp.zeros((tm, CONV_C), F32)
    for phase in range(SUBLANES):
        tiles = [(first + k) // SUBLANES for k in range(CONV_W) if (first + k) % SUBLANES == phase]
        if not tiles:
            continue
        if phase:
            sh_ref[phase, 0:n_sh, :] = up_ref[pl.ds(phase, n_sh), :]
        for a in tiles:
            k = a * SUBLANES + phase - first
            rows = up_ref[pl.ds(a * SUBLANES, tm), :] if phase == 0 else sh_ref[phase, pl.ds(a * SUBLANES, tm), :]
            acc = acc + rows * dw_ref[k:k + 1, :]
    up_ref[0:HIST, :] = up_ref[tm:tm + HIST, :]
    y_ref[...] = _tail_math(x_ref[...], oa_ref[...], sga_ref[...], acc, sgb_ref[...], sma_ref[...],
                            smb_ref[...], w)


def _tail_prompt(x, o_a, sga, u, sgb, sma, smb, dw, tail_w):
    n_batch, t_len, _ = x.shape
    tm = TM_TAIL

    def bt(width):
        return pl.BlockSpec((None, tm, width), lambda b, t: (b, t, 0))

    ins = [x, o_a, sga, u, sgb, sma, smb]
    return pl.pallas_call(
        _tail_prompt_kernel,
        grid=(n_batch, t_len // tm),
        in_specs=[bt(a.shape[-1]) for a in ins] + [_const_spec(dw.shape)] + [_const_spec(a.shape) for a in tail_w],
        out_specs=bt(D_MODEL),
        out_shape=jax.ShapeDtypeStruct((n_batch, t_len, D_MODEL), F32),
        scratch_shapes=[pltpu.VMEM((HIST + tm, CONV_C), F32), pltpu.VMEM((SUBLANES, HIST + tm, CONV_C), F32)],
        compiler_params=pltpu.CompilerParams(dimension_semantics=("arbitrary", "arbitrary"),
                                             vmem_limit_bytes=VMEM_LIMIT),
        name="tail_prompt",
    )(*ins, dw, *tail_w)


S_ROWS = 8
UP_ROWS = 40


def _tail_sample_kernel(x_ref, oa_ref, sga_ref, ups_ref, sgb_ref, sma_ref, smb_ref, dws_ref, *rest,
                        n_new):
    w, (y_ref, c_ref) = rest[:9], rest[9:]
    n_batch = ups_ref.shape[0]
    c_ref[...] = jnp.zeros(c_ref.shape, F32)

    def body(b, carry):
        up = ups_ref[b]
        for t in range(n_new):
            c_ref[pl.ds(b * S_ROWS + t, 1), :] = jnp.sum(up * dws_ref[t], axis=0, keepdims=True)
        return carry

    lax.fori_loop(0, n_batch, body, 0)
    y_ref[...] = _tail_math(x_ref[...], oa_ref[...], sga_ref[...], c_ref[...], sgb_ref[...], sma_ref[...],
                            smb_ref[...], w)


def _tail_sample(x, o_a, sga, up_s, sgb, sma, smb, dw_shift, tail_w, n_new):
    ins = [x, o_a, sga, up_s, sgb, sma, smb, dw_shift] + list(tail_w)
    m = x.shape[0]
    return pl.pallas_call(
        functools.partial(_tail_sample_kernel, n_new=n_new),
        grid=(1,),
        in_specs=[_const_spec(a.shape) for a in ins],
        out_specs=_const_spec((m, D_MODEL)),
        out_shape=jax.ShapeDtypeStruct((m, D_MODEL), F32),
        scratch_shapes=[pltpu.VMEM((m, CONV_C), F32)],
        compiler_params=pltpu.CompilerParams(dimension_semantics=("arbitrary",),
                                             vmem_limit_bytes=VMEM_LIMIT),
        name="tail_sample",
    )(*ins)


def _group_q(q_ref, g):
    q = q_ref[g * HPG:(g + 1) * HPG].reshape(HPG * S_ROWS, LANES).astype(F32)
    return q if g == 0 else pltpu.roll(q, HEAD_DIM, axis=1)


def _group_out(o, g):
    return o if g == 0 else pltpu.roll(o, HEAD_DIM, axis=1)


def _scmp_kernel(pt_ref, kpool_ref, vpool_ref, q_ref, perm_ref, pek_ref, w1k_ref, w2k_ref, pev_ref, w1v_ref,
                 w2v_ref, agg_ref, oc_ref, score_ref, ak_ref, av_ref, kbuf_ref, vbuf_ref, sem_ref,
                 *, n_pages_step, n_steps, past_len):
    p = n_pages_step
    b, pg = pl.program_id(0), pl.program_id(1)
    step = b * n_steps + pg
    n_total = pl.num_programs(0) * n_steps
    chunks_page = PAGE_SIZE // CMP_STRIDE
    rows = chunks_page * p

    def page_copies(step_idx, slot, lookup):
        if lookup:
            bb = step_idx // n_steps
            first = (step_idx - bb * n_steps) * p
        for i in range(p):
            page = pt_ref[bb, first + i] if lookup else 0
            for pool_ref, buf_ref in ((kpool_ref, kbuf_ref), (vpool_ref, vbuf_ref)):
                yield pltpu.make_async_copy(pool_ref.at[page], buf_ref.at[slot, i], sem_ref.at[slot])

    @pl.when(step == 0)
    def _():
        for cp in page_copies(step, 0, True):
            cp.start()

    @pl.when(step + 1 < n_total)
    def _():
        for cp in page_copies(step + 1, (step + 1) % 2, True):
            cp.start()

    slot = step % 2
    for cp in page_copies(step, slot, False):
        cp.wait()

    for buf_ref, w1_ref, a_ref in ((kbuf_ref, w1k_ref, ak_ref), (vbuf_ref, w1v_ref, av_ref)):
        blocks = []
        for i in range(p):
            y = _dot_nt(perm_ref[...], buf_ref[slot, i].astype(BF16))
            blocks.append(jnp.concatenate(
                [y[j * chunks_page:(j + 1) * chunks_page] for j in range(CMP_STRIDE)], axis=-1))
        xc = jnp.concatenate(blocks, axis=0).astype(BF16)
        a_ref[pl.ds(pl.multiple_of(pg * rows, rows), rows), :] = _dot(xc, w1_ref[...])

    @pl.when(pg == n_steps - 1)
    def _():
        kc = _compress_from_partials(ak_ref[...], pek_ref, w1k_ref, w2k_ref).astype(BF16)
        vc = _compress_from_partials(av_ref[...], pev_ref, w1v_ref, w2v_ref).astype(BF16)
        n_chunks = kc.shape[0]
        n_blocks = past_len // SLC_LEN + 1
        r32 = HPG * S_ROWS
        qpos32 = past_len + (_iota((r32, n_chunks), 0) & (S_ROWS - 1))
        cvalid = (_iota((r32, n_chunks), 1) * CMP_STRIDE + (CMP_LEN - 1)) <= qpos32
        nb_pad = agg_ref.shape[1]
        jl = _iota((S_ROWS, nb_pad), 1)
        qpos8 = past_len + _iota((S_ROWS, nb_pad), 0)
        for g in range(N_KV):
            qg = _group_q(q_ref, g).astype(BF16)
            p_c = _masked_softmax(_dot_nt(qg, kc), cvalid)
            oc_ref[g] = _group_out(_dot(p_c.astype(BF16), vc), g)
            psum = (p_c[0:S_ROWS] + p_c[S_ROWS:2 * S_ROWS] + p_c[2 * S_ROWS:3 * S_ROWS]
                    + p_c[3 * S_ROWS:4 * S_ROWS])
            imp = _dot(psum.astype(BF16), agg_ref[...])
            score_ref[g] = _block_scores(imp, jl, qpos8, n_blocks)


def _scmp(page_table, kpool, vpool, qp_s, cmp_w, agg_s, past_len):
    n_batch, n_pages = page_table.shape
    p = PAGES_PER_STEP
    n_steps = n_pages // p
    n_chunks = n_pages * (PAGE_SIZE // CMP_STRIDE)

    def cspec(shape):
        nd = len(shape)
        return pl.BlockSpec(shape, lambda b, s, pt: (0,) * nd)

    r_out = jnp.arange(PAGE_SIZE)
    chunks_page = PAGE_SIZE // CMP_STRIDE
    src = (r_out % chunks_page) * CMP_STRIDE + r_out // chunks_page
    perm = (src[:, None] == jnp.arange(PAGE_SIZE)[None, :]).astype(BF16)

    in_specs = ([pl.BlockSpec(memory_space=pl.ANY), pl.BlockSpec(memory_space=pl.ANY),
                 pl.BlockSpec((None, N_HEADS, S_ROWS, LANES), lambda b, s, pt: (0, 0, b, 0))]
                + [cspec(perm.shape)] + [cspec(w.shape) for w in cmp_w] + [cspec(agg_s.shape)])
    page_buf = pltpu.VMEM((2, p, KV_W, PAGE_SIZE), F32)
    nb_pad = agg_s.shape[1]
    out_specs = [pl.BlockSpec((None, N_KV, HPG * S_ROWS, LANES), lambda b, s, pt: (b, 0, 0, 0)),
                 pl.BlockSpec((None, N_KV, S_ROWS, nb_pad), lambda b, s, pt: (b, 0, 0, 0))]
    out_shape = [jax.ShapeDtypeStruct((n_batch, N_KV, HPG * S_ROWS, LANES), F32),
                 jax.ShapeDtypeStruct((n_batch, N_KV, S_ROWS, nb_pad), F32)]
    oc, score = pl.pallas_call(
        functools.partial(_scmp_kernel, n_pages_step=p, n_steps=n_steps, past_len=past_len),
        grid_spec=pltpu.PrefetchScalarGridSpec(
            num_scalar_prefetch=1, grid=(n_batch, n_steps), in_specs=in_specs, out_specs=out_specs,
            scratch_shapes=[pltpu.VMEM((n_chunks, 2 * KV_W), F32)] * 2
            + [page_buf, page_buf, pltpu.SemaphoreType.DMA((2,))]),
        out_shape=out_shape,
        compiler_params=pltpu.CompilerParams(dimension_semantics=("arbitrary", "arbitrary"),
                                             vmem_limit_bytes=VMEM_LIMIT),
        name="sample_cmp",
    )(page_table, kpool, vpool, qp_s, perm, *cmp_w, agg_s)
    return oc, _top_blocks(score.reshape(n_batch * N_KV * S_ROWS, nb_pad))


def _top_blocks_kernel(score_ref, idx_ref):
    score = score_ref[...]
    rows, nb_pad = score.shape
    jlf = _iota((rows, nb_pad), 1).astype(F32)
    lane = _iota((rows, LANES), 1)
    idx = jnp.zeros((rows, LANES), F32)
    for k in range(TOP_N):
        mx = jnp.max(score, axis=-1, keepdims=True)
        am = jnp.min(jnp.where(score == mx, jlf, float(nb_pad)), axis=-1, keepdims=True)
        idx = jnp.where(lane == k, am, idx)
        score = jnp.where(jlf == am, -3.0, score)
    idx_ref[...] = idx.astype(jnp.int32)


def _top_blocks(score):
    rows = score.shape[0]
    return pl.pallas_call(
        _top_blocks_kernel,
        grid=(1,),
        in_specs=[_const_spec(score.shape)],
        out_specs=_const_spec((rows, LANES)),
        out_shape=jax.ShapeDtypeStruct((rows, LANES), jnp.int32),
        compiler_params=pltpu.CompilerParams(dimension_semantics=("arbitrary",), vmem_limit_bytes=VMEM_LIMIT),
        name="sample_topk",
    )(score)


def _ssel_kernel(pt_ref, idx_ref, kpool_ref, vpool_ref, q_ref, ksn_ref, vsn_ref, kwn_ref, vwn_ref,
                 kwb_ref, vwb_ref, gn_ref, oc_ref, o_ref, kbuf_ref, vbuf_ref, sem_ref,
                 *, n_new, past_len):
    b = pl.program_id(0)
    n_pool_blocks = past_len // SLC_LEN
    blocks_page = PAGE_SIZE // SLC_LEN

    page_shift = blocks_page.bit_length() - 1
    assert blocks_page == 1 << page_shift

    def block_copies(bb, ring, lookup):
        for g in range(N_KV):
            for qi in range(n_new):
                for k in range(TOP_N):
                    page = 0
                    if lookup:
                        j = jnp.minimum(idx_ref[bb, (g * n_new + qi) * TOP_N + k], n_pool_blocks - 1)
                        page = pt_ref[bb, lax.shift_right_logical(j, page_shift)]
                    for pool_ref, buf_ref in ((kpool_ref, kbuf_ref), (vpool_ref, vbuf_ref)):
                        yield pltpu.make_async_copy(
                            pool_ref.at[page, pl.ds(g * HEAD_DIM, HEAD_DIM), :],
                            buf_ref.at[ring, g, :, pl.ds((qi * TOP_N + k) * PAGE_SIZE, PAGE_SIZE)],
                            sem_ref.at[ring])

    @pl.when(b == 0)
    def _():
        for cp in block_copies(b, 0, True):
            cp.start()

    @pl.when(b + 1 < pl.num_programs(0))
    def _():
        for cp in block_copies(b + 1, (b + 1) % 2, True):
            cp.start()

    ring = b % 2

    r32 = HPG * S_ROWS
    n_sel = n_new * TOP_N * PAGE_SIZE
    new_rows = LANES
    newcol = _iota((r32, new_rows), 1)
    qrow_new = _iota((r32, new_rows), 0) & (S_ROWS - 1)

    def pad_new(ref):
        return jnp.concatenate([ref[...], jnp.zeros((new_rows - S_ROWS, KV_W), F32)], axis=0).astype(BF16)

    ks_new, vs_new, kw_new, vw_new = pad_new(ksn_ref), pad_new(vsn_ref), pad_new(kwn_ref), pad_new(vwn_ref)

    w_buf = kwb_ref.shape[1]
    wpos = jnp.concatenate([past_len - w_buf + _iota((r32, w_buf), 1), past_len + newcol], axis=1)
    qpos_w = past_len + (_iota((r32, w_buf + new_rows), 0) & (S_ROWS - 1))
    wmask = (wpos <= qpos_w) & (wpos >= qpos_w - WINDOW)
    wmask = wmask & jnp.concatenate([jnp.full((r32, w_buf), True), newcol < n_new], axis=1)
    kw_t = kwb_ref[...].astype(BF16)
    vw_t = vwb_ref[...].astype(BF16)
    gates = _sigmoid(gn_ref[...])

    o_w = []
    q_g = []
    for g in range(N_KV):
        qg = _group_q(q_ref, g).astype(BF16)
        q_g.append(qg)
        s_w = jnp.concatenate([_dot(qg, kw_t), _dot_nt(qg, kw_new)], axis=1)
        p_w = _masked_softmax(s_w, wmask).astype(BF16)
        o_w.append(_group_out(_dot_nt(p_w[:, :w_buf], vw_t) + _dot(p_w[:, w_buf:], vw_new), g))

    for cp in block_copies(b, ring, False):
        cp.wait()

    pad64 = jnp.zeros((r32, LANES - HEAD_DIM), F32)
    row_qi = _iota((r32, PAGE_SIZE), 0) & (S_ROWS - 1)
    lane_part = lax.shift_right_logical(_iota((r32, PAGE_SIZE), 1), SLC_LEN.bit_length() - 1)
    for g in range(N_KV):
        q64 = q_ref[g * HPG:(g + 1) * HPG].reshape(r32, LANES)[:, :HEAD_DIM]
        pieces = []
        new_limit = jnp.zeros((r32, new_rows), jnp.int32)
        for qi in range(n_new):
            n_hit = jnp.int32(0)
            for k in range(TOP_N):
                j = idx_ref[b, (g * n_new + qi) * TOP_N + k]
                part = jnp.where(j < n_pool_blocks, j & (blocks_page - 1), -1)
                pieces.append((row_qi == qi) & (lane_part == part))
                n_hit = n_hit + jnp.where(j == n_pool_blocks, 1, 0)
            new_limit = jnp.where(qrow_new == qi, jnp.where(n_hit > 0, n_new, 0), new_limit)
        nmask = (newcol <= qrow_new) & (newcol < new_limit)
        s_s = jnp.concatenate([_dot(q64, kbuf_ref[ring, g].astype(BF16)), _dot_nt(q_g[g], ks_new)], axis=1)
        p_s = _masked_softmax(s_s, jnp.concatenate(pieces + [nmask], axis=1)).astype(BF16)
        o_pool = _dot_nt(p_s[:, :n_sel], vbuf_ref[ring, g].astype(BF16))
        o_s = jnp.concatenate([o_pool, pad64], axis=1) + _group_out(_dot(p_s[:, n_sel:], vs_new), g)
        o_c = oc_ref[g]
        heads = []
        for h in range(HPG):
            col = (g * HPG + h) * 3
            rs = slice(h * S_ROWS, (h + 1) * S_ROWS)
            heads.append(gates[:, col:col + 1] * o_c[rs] + gates[:, col + 1:col + 2] * o_s[rs]
                         + gates[:, col + 2:col + 3] * o_w[g][rs])
        for pr in range(HPG // 2):
            c0 = (g * HPG + 2 * pr) * HEAD_DIM
            o_ref[:, c0:c0 + LANES] = _pair_heads(heads[2 * pr], heads[2 * pr + 1])


def _ssel(page_table, idx, kpool, vpool, qp_s, ks_n, vs_n, kw_n, vw_n, kw_buf, vw_buf, gn_s, oc,
          n_new, past_len):
    n_batch = page_table.shape[0]
    w_buf = kw_buf.shape[2]
    rows8 = pl.BlockSpec((S_ROWS, KV_W), lambda b, pt, ix: (b, 0))
    wbuf_spec = pl.BlockSpec((None, KV_W, w_buf), lambda b, pt, ix: (b, 0, 0))
    in_specs = [pl.BlockSpec(memory_space=pl.ANY), pl.BlockSpec(memory_space=pl.ANY),
                pl.BlockSpec((None, N_HEADS, S_ROWS, LANES), lambda b, pt, ix: (0, 0, b, 0)),
                rows8, rows8, rows8, rows8, wbuf_spec, wbuf_spec,
                pl.BlockSpec((S_ROWS, LANES), lambda b, pt, ix: (b, 0)),
                pl.BlockSpec((None, N_KV, HPG * S_ROWS, LANES), lambda b, pt, ix: (b, 0, 0, 0))]
    return pl.pallas_call(
        functools.partial(_ssel_kernel, n_new=n_new, past_len=past_len),
        grid_spec=pltpu.PrefetchScalarGridSpec(
            num_scalar_prefetch=2, grid=(n_batch,), in_specs=in_specs,
            out_specs=pl.BlockSpec((S_ROWS, NSA_W), lambda b, pt, ix: (b, 0)),
            scratch_shapes=[pltpu.VMEM((2, N_KV, HEAD_DIM, n_new * TOP_N * PAGE_SIZE), F32)] * 2
            + [pltpu.SemaphoreType.DMA((2,))]),
        out_shape=jax.ShapeDtypeStruct((n_batch * S_ROWS, NSA_W), F32),
        compiler_params=pltpu.CompilerParams(dimension_semantics=("arbitrary",),
                                             vmem_limit_bytes=VMEM_LIMIT),
        name="sample_sel",
    )(page_table, idx, kpool, vpool, qp_s, ks_n, vs_n, kw_n, vw_n, kw_buf, vw_buf, gn_s, oc)


def _prep_w_in(w_in):
    n_gate = 3 * N_HEADS
    w_a = w_in[:, :A_END].astype(BF16)
    w_gn = jnp.pad(w_in[:, A_END:A_END + n_gate], ((0, 0), (0, LANES - n_gate))).astype(BF16)
    w_rest = w_in[:, A_END + n_gate:].astype(BF16)
    assert w_rest.shape[1] == R_END
    return w_a, w_gn, w_rest


def _prep_cmp(pe, w1, w2):
    w1r = w1.reshape(CMP_R, CMP_STRIDE, HEAD_DIM, HEAD_DIM)
    eye = jnp.eye(N_KV, dtype=w1.dtype)
    w1big = jnp.einsum('rjdh,ge->jgdreh', w1r, eye).reshape(CMP_STRIDE * KV_W, CMP_R * KV_W)
    pe_rows = jnp.broadcast_to(pe.reshape(CMP_R, CMP_STRIDE, 1, HEAD_DIM),
                               (CMP_R, CMP_STRIDE, N_KV, HEAD_DIM)).reshape(CMP_R, CMP_STRIDE * KV_W)
    pe_rows = jnp.pad(pe_rows, ((0, SUBLANES - CMP_R), (0, 0)))
    w2big = jnp.einsum('hd,ge->ghed', w2, eye).reshape(KV_W, KV_W)
    return pe_rows.astype(BF16), w1big.astype(BF16), w2big.astype(BF16)


def _agg_matrix(n_rows, n_blocks, n_cols, col0):
    i = jnp.arange(n_rows)[:, None] * CMP_STRIDE
    jj = jnp.arange(n_cols)[None, :] - col0
    hit = (i < jj * SLC_LEN + SLC_LEN) & (i + CMP_LEN > jj * SLC_LEN) & (jj >= 0) & (jj < n_blocks)
    return hit.astype(BF16)


def kernel(x_prompt, x_sample, cache_k_cmp, cache_v_cmp, cache_k_slc, cache_v_slc, cache_k_win, cache_v_win, state_conv, page_table, ln_g, w_in, pe_k, w1_k, w2_k, pe_v, w1_v, w2_v, dw_k, dw_b, cln_g, cln_b, pw_w, pw_b, w_pa, w_pb, w_o, final_g):
    assert w_in.shape[0] == 1, "single-layer stack"
    n_b, t_len, _ = x_prompt.shape
    n_db, n_new, _ = x_sample.shape
    n_pages = page_table.shape[1]
    past_len = n_pages * PAGE_SIZE
    w_buf = cache_k_win.shape[2]
    assert n_new <= S_ROWS and t_len % TQ == 0 and t_len >= WINDOW + TQ
    assert n_new < CMP_STRIDE, "the new rows must not complete a compression chunk"
    assert t_len // SLC_LEN <= LANES - HEAD_DIM and past_len % PAGE_SIZE == 0

    w_all = _prep_w_in(w_in[0])
    lng = ln_g[0].reshape(1, D_MODEL)
    cmp_w = _prep_cmp(pe_k[0], w1_k[0], w2_k[0]) + _prep_cmp(pe_v[0], w1_v[0], w2_v[0])
    tail_w = (dw_b[0].reshape(1, CONV_C), cln_g[0].reshape(1, CONV_C), cln_b[0].reshape(1, CONV_C),
              pw_w[0].astype(BF16), pw_b[0].reshape(1, CONV_C), w_pa[0].astype(BF16), w_pb[0].astype(BF16),
              w_o[0].astype(BF16), final_g.reshape(1, D_MODEL))
    dw = jnp.pad(dw_k[0], ((0, 1), (0, 0)))

    (qt, kct, vct, kst, vst, kwt, vwt, gnt, sga, u, sgb, sma, smb, k_c, v_c, ksa, vst_g, kwp, vwt_g) = _proj(
        x_prompt.reshape(n_b * t_len, D_MODEL), lng, w_all, n_b, t_len, (0, 1), True)
    seq = lambda a: a.reshape(n_b, t_len, a.shape[-1])
    kcm, vcmt = _cmp_prompt(seq(k_c), seq(v_c), cmp_w)
    n_cmp_rows = t_len // CMP_STRIDE
    aggt_p = _agg_matrix(n_cmp_rows, t_len // SLC_LEN, LANES, 0).T
    o_a = _attn_prompt(qt, gnt, kcm, vcmt, ksa, vst_g, kwp, vwt_g, aggt_p)
    y_prompt = _tail_prompt(x_prompt, o_a, seq(sga), seq(u), seq(sgb), seq(sma), seq(smb), dw, tail_w)

    xs = jnp.pad(x_sample, ((0, 0), (0, S_ROWS - n_new), (0, 0))).reshape(n_db * S_ROWS, D_MODEL)
    (qp_s, kct_s, vct_s, kst_s, vst_s, kwt_s, vwt_s, gn_s, sga_s, u_s, sgb_s, sma_s, smb_s,
     k_s_s, v_s_s, k_w_s, v_w_s) = _proj(xs, lng, w_all, 1, n_db * S_ROWS, (2, 3, 4, 5), False)
    pool = lambda c: jnp.transpose(c[0], (0, 2, 3, 1)).reshape(c.shape[1], KV_W, PAGE_SIZE)
    n_blocks_s = past_len // SLC_LEN + 1
    nb_pad = -(-n_blocks_s // LANES) * LANES
    agg_s = _agg_matrix(past_len // CMP_STRIDE, n_blocks_s, nb_pad, 0)
    agg_s = agg_s.at[past_len // CMP_STRIDE - 1].set(0)
    oc_s, idx_s = _scmp(page_table, pool(cache_k_cmp), pool(cache_v_cmp), qp_s, cmp_w, agg_s, past_len)
    idx_flat = idx_s.reshape(n_db, N_KV, S_ROWS, LANES)[:, :, :n_new, :TOP_N].reshape(n_db, N_KV * n_new * TOP_N)
    win = lambda c: jnp.transpose(c[0], (0, 2, 3, 1)).reshape(n_db, KV_W, w_buf)
    o_a_s = _ssel(page_table, idx_flat, pool(cache_k_slc), pool(cache_v_slc), qp_s, k_s_s, v_s_s, k_w_s,
                  v_w_s, win(cache_k_win), win(cache_v_win), gn_s, oc_s, n_new, past_len)
    u_new = u_s.reshape(n_db, S_ROWS, CONV_C)[:, :n_new]
    up_s = jnp.concatenate([state_conv[0], u_new], axis=1)
    up_pad = jnp.pad(up_s, ((0, 0), (0, UP_ROWS - up_s.shape[1]), (0, 0)))
    dw_shift = jnp.stack([jnp.pad(dw_k[0], ((t, UP_ROWS - CONV_W - t), (0, 0))) for t in range(n_new)])
    y_s = _tail_sample(xs, o_a_s, sga_s, up_pad, sgb_s, sma_s, smb_s, dw_shift, tail_w, n_new)
    y_sample = y_s.reshape(n_db, S_ROWS, D_MODEL)[:, :n_new]

    def p_rows(at):
        return jnp.transpose(at.reshape(n_b, N_KV, HEAD_DIM, at.shape[-1]), (0, 3, 1, 2))[None]

    def s_rows(at):
        a = at.reshape(N_KV, HEAD_DIM, n_db, S_ROWS)[:, :, :, :n_new]
        return jnp.transpose(a, (2, 3, 0, 1))[None]

    win_p = min(WINDOW, t_len)
    p_conv = seq(u)[:, -(CONV_W - 1):][None]
    s_k_win = jnp.concatenate([cache_k_win[0], s_rows(kwt_s)[0]], axis=1)[:, -w_buf:][None]
    s_v_win = jnp.concatenate([cache_v_win[0], s_rows(vwt_s)[0]], axis=1)[:, -w_buf:][None]
    s_conv = up_s[:, -(CONV_W - 1):][None]
    return (y_prompt, y_sample, p_rows(kct), p_rows(vct), p_rows(kst), p_rows(vst),
            p_rows(kwt[:, :, -win_p:]), p_rows(vwt[:, :, -win_p:]), p_conv,
            s_rows(kct_s), s_rows(vct_s), s_rows(kst_s), s_rows(vst_s), s_k_win, s_v_win, s_conv)
```

```python
import functools

import jax
import jax.numpy as jnp
from jax import lax
from jax.experimental import pallas as pl
from jax.experimental.pallas import tpu as pltpu

F32 = jnp.float32
BF16 = jnp.bfloat16

D_MODEL = 1024
N_HEADS = 8
HEAD_DIM = 64
N_KV = 2
HPG = N_HEADS // N_KV
NSA_W = N_HEADS * HEAD_DIM
KV_W = N_KV * HEAD_DIM
CMP_LEN = 32
CMP_STRIDE = 16
CMP_R = CMP_LEN // CMP_STRIDE
SLC_LEN = 64
TOP_N = 16
WINDOW = 512
CONV_C = 512
CONV_W = 31
PAGE_SIZE = 128
EPS = 1e-6
FORCED_SCORE = 1e4
NEG_INF = -1e30
MASK_BIAS = -1e4

LANES = 128
SUBLANES = 8
VMEM_LIMIT = 56 * 1024 * 1024

A_KV = NSA_W
A_END = A_KV + 6 * KV_W
R_GLU = NSA_W
R_GB = R_GLU + 2 * CONV_C
R_MRG = R_GB + CONV_C
R_END = R_MRG + 2 * D_MODEL

TM_PROJ = 256
TQ = 256
TM_TAIL = 256
PAGES_PER_STEP = 64


def _sigmoid(x):
    return jax.nn.sigmoid(x)


def _silu(x):
    return jax.nn.silu(x)


def _dot(a, b):
    return jnp.dot(a, b, preferred_element_type=F32)


def _dot_nt(a, b):
    return lax.dot_general(a, b, (((1,), (1,)), ((), ())), preferred_element_type=F32)


def _iota(shape, dim):
    return lax.broadcasted_iota(jnp.int32, shape, dim)


def _const_spec(shape):
    nd = len(shape)
    return pl.BlockSpec(shape, lambda *_: (0,) * nd)


HIST = 32


def _causal_dw_conv(up_ref, sh_ref, u, dw_ref, first_tile):
    tm = u.shape[0]

    @pl.when(first_tile)
    def _():
        up_ref[0:HIST, :] = jnp.zeros((HIST, CONV_C), F32)

    up_ref[HIST:HIST + tm, :] = u
    first = HIST - (CONV_W - 1)
    n_sh = HIST + tm - SUBLANES
    acc = jnp.zeros((tm, CONV_C), F32)
    for phase in range(SUBLANES):
        tiles = [(first + k) // SUBLANES for k in range(CONV_W) if (first + k) % SUBLANES == phase]
        if tiles and phase:
            sh_ref[phase, 0:n_sh, :] = up_ref[pl.ds(phase, n_sh), :]
        for a in tiles:
            k = a * SUBLANES + phase - first
            rows = up_ref[pl.ds(a * SUBLANES, tm), :] if phase == 0 else sh_ref[phase, pl.ds(a * SUBLANES, tm), :]
            acc = acc + rows * dw_ref[k:k + 1, :]
    up_ref[0:HIST, :] = up_ref[tm:tm + HIST, :]
    return acc


def _proj_kernel(x_ref, lng_ref, wa_ref, wgn_ref, w_ref, qp_ref, kct_ref, vct_ref, kst_ref, vst_ref,
                 kwt_ref, vwt_ref, gn_ref, sga_ref, u_ref, sgb_ref, sma_ref, smb_ref, *more_refs,
                 tiles_per_batch, row_ids, attn_ops):
    x = x_ref[...]
    tm = x.shape[0]
    xn = x * lax.rsqrt(jnp.mean(x * x, axis=-1, keepdims=True) + EPS) * lng_ref[...]
    xn = xn.astype(BF16)

    zq = _dot(xn, wa_ref[:, :A_KV]) * (HEAD_DIM ** -0.5 * (LOG2_E if attn_ops else 1.0))
    zgn = _dot(xn, wgn_ref[...])
    lane = _iota((tm, LANES), 1)
    for pr in range(N_HEADS // 2):
        two = zq[:, pr * LANES:(pr + 1) * LANES]
        if attn_ops:
            two_t = two.T
            pad = jnp.zeros((LANES - HEAD_DIM, tm), F32)
            for o in range(2):
                qp_ref[2 * pr + o] = jnp.concatenate(
                    [two_t[o * HEAD_DIM:(o + 1) * HEAD_DIM], pad], axis=0).astype(BF16)
        else:
            qp_ref[2 * pr] = jnp.where(lane < HEAD_DIM, two, 0.0).astype(BF16)
            qp_ref[2 * pr + 1] = jnp.where(lane < HEAD_DIM, pltpu.roll(two, HEAD_DIM, axis=1), 0.0).astype(BF16)
    gn_ref[...] = zgn.T if attn_ops else zgn

    zkv = _dot(xn, wa_ref[:, A_KV:A_END])
    kv = [zkv[:, i * KV_W:(i + 1) * KV_W] for i in range(6)]
    kvt = [v.T for v in kv]
    for r, v in zip((kct_ref, vct_ref, kst_ref, vst_ref, kwt_ref, vwt_ref), kvt):
        r[...] = v
    for r, i in zip(more_refs[:len(row_ids)], row_ids):
        r[...] = kv[i]

    if attn_ops:
        ksa_ref, vst_g_ref, kwp_ref, vwt_g_ref = more_refs[len(row_ids):]
        t0 = (pl.program_id(0) % tiles_per_batch) * tm
        blk = (t0 + _iota((tm, LANES), 0)) >> 6
        onehot = jnp.where((lane - HEAD_DIM) == blk, 1.0, 0.0)
        ones_row = jnp.where(_iota((V_ROWS - HEAD_DIM, tm), 0) == 0, 1.0, 0.0)
        for g in range(N_KV):
            rows_g = slice(g * HEAD_DIM, (g + 1) * HEAD_DIM)

            def grp(v):
                return v if g == 0 else pltpu.roll(v, HEAD_DIM, axis=1)

            ksa_ref[g] = jnp.where(lane < HEAD_DIM, grp(kv[2]), onehot).astype(BF16)
            kwp_ref[g] = grp(kv[4]).astype(BF16)
            for ref, vt in ((vst_g_ref, kvt[3]), (vwt_g_ref, kvt[5])):
                v_aug = jnp.concatenate([vt[rows_g], ones_row], axis=0).astype(BF16)
                for j in range(tm // KEY_CHUNK):
                    ref[g, j] = v_aug[:, j * KEY_CHUNK:(j + 1) * KEY_CHUNK]

    zglu = _dot(xn, w_ref[:, R_GLU:R_GB])
    u = zglu[:, :CONV_C] * _sigmoid(zglu[:, CONV_C:])
    u_ref[...] = u
    sga_ref[...] = _silu(_dot(xn, w_ref[:, :R_GLU]))
    sgb_ref[...] = _silu(_dot(xn, w_ref[:, R_GB:R_MRG]))
    zm = _dot(xn, w_ref[:, R_MRG:R_END])
    sma_ref[...] = _sigmoid(zm[:, :D_MODEL])
    smb_ref[...] = _sigmoid(zm[:, D_MODEL:])


def _proj(x2d, ln_g, w_all, n_batch, t_len, row_ids, attn_ops):
    m = x2d.shape[0]
    tm = min(TM_PROJ, t_len)
    tpb = t_len // tm
    grid = (m // tm,)

    def row_spec(width):
        return pl.BlockSpec((tm, width), lambda i: (i, 0))

    def bt_spec(lead, width):
        return pl.BlockSpec((None, lead, tm, width), lambda i: (i // tpb, 0, i % tpb, 0))

    t_spec = pl.BlockSpec((None, KV_W, tm), lambda i: (i // tpb, 0, i % tpb))
    if attn_ops:
        out_shape = [jax.ShapeDtypeStruct((n_batch, N_HEADS, LANES, t_len), BF16)]
        out_specs = [pl.BlockSpec((None, N_HEADS, LANES, tm), lambda i: (i // tpb, 0, 0, i % tpb))]
    else:
        out_shape = [jax.ShapeDtypeStruct((n_batch, N_HEADS, t_len, LANES), BF16)]
        out_specs = [bt_spec(N_HEADS, LANES)]
    out_shape += [jax.ShapeDtypeStruct((n_batch, KV_W, t_len), F32)] * 6
    out_specs += [t_spec] * 6
    if attn_ops:
        out_shape.append(jax.ShapeDtypeStruct((n_batch, LANES, t_len), F32))
        out_specs.append(t_spec)
    else:
        out_shape.append(jax.ShapeDtypeStruct((m, LANES), F32))
        out_specs.append(row_spec(LANES))
    for width in (NSA_W, CONV_C, CONV_C, D_MODEL, D_MODEL):
        out_shape.append(jax.ShapeDtypeStruct((m, width), F32))
        out_specs.append(row_spec(width))
    out_shape += [jax.ShapeDtypeStruct((m, KV_W), F32)] * len(row_ids)
    out_specs += [row_spec(KV_W)] * len(row_ids)
    if attn_ops:
        cpt = tm // KEY_CHUNK
        k_shape = jax.ShapeDtypeStruct((n_batch, N_KV, t_len, LANES), BF16)
        vt_shape = jax.ShapeDtypeStruct((n_batch, N_KV, t_len // KEY_CHUNK, V_ROWS, KEY_CHUNK), BF16)
        vt_spec = pl.BlockSpec((None, N_KV, cpt, V_ROWS, KEY_CHUNK), lambda i: (i // tpb, 0, i % tpb, 0, 0))
        out_shape += [k_shape, vt_shape, k_shape, vt_shape]
        out_specs += [bt_spec(N_KV, LANES), vt_spec, bt_spec(N_KV, LANES), vt_spec]

    return pl.pallas_call(
        functools.partial(_proj_kernel, tiles_per_batch=tpb, row_ids=row_ids, attn_ops=attn_ops),
        grid=grid,
        in_specs=[row_spec(D_MODEL), _const_spec((1, D_MODEL))] + [_const_spec(w.shape) for w in w_all],
        out_specs=out_specs,
        out_shape=out_shape,
        compiler_params=pltpu.CompilerParams(dimension_semantics=("arbitrary",),
                                             vmem_limit_bytes=VMEM_LIMIT),
        name="proj",
    )(x2d, ln_g, *w_all)


def _compress_from_partials(per, pe_ref, w1_ref, w2_ref):
    n = per.shape[0]
    hp = _dot(pe_ref[...], w1_ref[...])
    h0 = hp[0:1, :KV_W] + hp[1:2, KV_W:]
    h = h0 + per[:, :KV_W] + pltpu.roll(per[:, KV_W:], n - 1, axis=0)
    return _dot(_silu(h).astype(BF16), w2_ref[...])


def _chunk_rows(ref, n_chunks):
    return jnp.concatenate([ref[pl.ds(j, n_chunks, stride=CMP_STRIDE), :] for j in range(CMP_STRIDE)],
                           axis=-1)


def _cmp_prompt_kernel(kc_ref, vc_ref, pek_ref, w1k_ref, w2k_ref, pev_ref, w1v_ref, w2v_ref,
                       ok_ref, ov_ref):
    n_chunks = kc_ref.shape[0] // CMP_STRIDE
    for x_ref, pe_ref, w1_ref, w2_ref, o_ref, transposed in (
            (kc_ref, pek_ref, w1k_ref, w2k_ref, ok_ref, False), (vc_ref, pev_ref, w1v_ref, w2v_ref, ov_ref, True)):
        per = _dot(_chunk_rows(x_ref, n_chunks).astype(BF16), w1_ref[...])
        c = _compress_from_partials(per, pe_ref, w1_ref, w2_ref)
        for g, cg in enumerate((c, pltpu.roll(c, HEAD_DIM, axis=1))):
            o_ref[g] = (cg.T if transposed else cg).astype(BF16)


def _cmp_prompt(k_c, v_c, cmp_w):
    n_batch, t_len, _ = k_c.shape
    n_chunks = t_len // CMP_STRIDE
    seq = pl.BlockSpec((None, t_len, KV_W), lambda b: (b, 0, 0))
    out = pl.BlockSpec((None, N_KV, n_chunks, KV_W), lambda b: (b, 0, 0, 0))
    w_specs = [_const_spec(w.shape) for w in cmp_w]
    return pl.pallas_call(
        _cmp_prompt_kernel,
        grid=(n_batch,),
        in_specs=[seq, seq] + w_specs,
        out_specs=[out, out],
        out_shape=[jax.ShapeDtypeStruct((n_batch, N_KV, n_chunks, KV_W), BF16)] * 2,
        compiler_params=pltpu.CompilerParams(dimension_semantics=("arbitrary",),
                                             vmem_limit_bytes=VMEM_LIMIT),
        name="cmp_prompt",
    )(k_c, v_c, *cmp_w)


def _masked_softmax(s, mask):
    s = jnp.where(mask, s, NEG_INF)
    e = jnp.where(mask, jnp.exp(s - jnp.max(s, axis=-1, keepdims=True)), 0.0)
    return e / jnp.maximum(jnp.sum(e, axis=-1, keepdims=True), 1e-30)


def _softmax_rows(s, mask):
    s = jnp.where(mask, s, NEG_INF)
    e = jnp.where(mask, jnp.exp2(s - jnp.max(s, axis=0, keepdims=True)), 0.0)
    return e / jnp.maximum(jnp.sum(e, axis=0, keepdims=True), 1e-30)


def _block_scores(imp, j, qpos, n_blocks):
    cur = qpos >> 6
    forced = (j == 0) | (j == cur) | (j == cur - 1)
    score = jnp.where(forced, FORCED_SCORE, imp)
    score = jnp.where(j * SLC_LEN <= qpos, score, -1.0)
    return jnp.where((j >= 0) & (j < n_blocks), score, -2.0)


def _pair_heads(o_even, o_odd):
    lane = _iota(o_even.shape, 1)
    return jnp.where(lane < HEAD_DIM, o_even, pltpu.roll(o_odd, HEAD_DIM, axis=1))


KEY_CHUNK = 128
V_ROWS = LANES
LOG2_E = 1.4426950408889634
SEL_ROWS = 32


def _attn_prompt_kernel(qt_ref, gnt_ref, kcm_ref, vcmt_ref, ksa_ref, vst_ref, kwp_ref, vwt_ref, aggt_ref,
                        o_ref, oacc_ref, qsel_ref, m_ref, acc_ref, *, t_len):
    tq = gnt_ref.shape[1]
    n_blocks = t_len // SLC_LEN
    i = pl.program_id(1)
    t0 = i * tq
    chunks_tile = tq // KEY_CHUNK

    gates = _sigmoid(gnt_ref[...])
    qpos_c = t0 + _iota((KV_W, tq), 1)
    cvalid = (_iota((KV_W, tq), 0) * CMP_STRIDE + (CMP_LEN - 1)) <= qpos_c
    jblk = _iota((SEL_ROWS, tq), 0)
    qpos_j = t0 + _iota((SEL_ROWS, tq), 1)
    row_minus_lane = _iota((KEY_CHUNK, tq), 0) - _iota((KEY_CHUNK, tq), 1)

    def gate(hh, branch):
        r = hh * 3 + branch
        return gates[r:r + 1, :]

    flash_ops = ((qsel_ref, ksa_ref, vst_ref, 0, False),
                 (qt_ref, kwp_ref, vwt_ref, jnp.maximum(i - WINDOW // tq, 0), True))

    def step(br, c, hh, diagonal):
        q_ref, k_ref, vt_ref, _, banded = flash_ops[br]
        g, st = hh // HPG, br * N_HEADS + hh
        kc = k_ref[g, pl.ds(pl.multiple_of(c * KEY_CHUNK, KEY_CHUNK), KEY_CHUNK), :]
        s = _dot(kc, q_ref[hh])
        off = c * KEY_CHUNK - t0
        if diagonal:
            s = jnp.where(row_minus_lane + off <= 0, s, NEG_INF)
        elif banded:
            s = jnp.where(row_minus_lane + off >= -WINDOW, s, NEG_INF)
        m = m_ref[st]
        m_new = jnp.maximum(m, jnp.max(s, axis=0, keepdims=True))
        p = jnp.exp2(s - m_new).astype(BF16)
        acc_ref[st] = jnp.exp2(m - m_new) * acc_ref[st] + _dot(vt_ref[g, c], p)
        m_ref[st] = m_new

    def earlier_tiles(br):
        t_lo = flash_ops[br][3]

        def tile_steps(t):
            for cd in range(chunks_tile):
                for hh in range(N_HEADS):
                    step(br, t * chunks_tile + cd, hh, False)

        n_pairs = lax.shift_right_logical(i - t_lo, 1)

        def body(u, carry):
            tile_steps(t_lo + 2 * u)
            tile_steps(t_lo + 2 * u + 1)
            return carry

        lax.fori_loop(0, n_pairs, body, 0)

        @pl.when(t_lo + 2 * n_pairs < i)
        def _():
            tile_steps(i - 1)

    for g in range(N_KV):
        psum = None
        for h in range(HPG):
            hh = g * HPG + h
            p_c = _softmax_rows(_dot(kcm_ref[g], qt_ref[hh]), cvalid)
            oacc_ref[hh] = gate(hh, 0) * _dot(vcmt_ref[g, 0:HEAD_DIM, :], p_c.astype(BF16))
            psum = p_c if psum is None else psum + p_c
        imp = _dot(aggt_ref[...], psum.astype(BF16))[:SEL_ROWS]
        score = _block_scores(imp, jblk, qpos_j, n_blocks)
        rank = jnp.zeros((SEL_ROWS, tq), jnp.int32)
        for jp in range(n_blocks):
            other = score[jp:jp + 1, :]
            ahead = (other > score) | ((other == score) & (jblk > jp))
            rank = rank + jnp.where(ahead, 1, 0)
        allowed = (rank < TOP_N) & (jblk < n_blocks) & (jblk * SLC_LEN <= qpos_j)
        bias = jnp.where(allowed, 0.0, MASK_BIAS).astype(BF16)
        pad = jnp.zeros((LANES - HEAD_DIM - SEL_ROWS, tq), BF16)
        for h in range(HPG):
            qsel_ref[g * HPG + h] = jnp.concatenate([qt_ref[g * HPG + h, 0:HEAD_DIM, :], bias, pad], axis=0)

    m_ref[...] = jnp.full(m_ref.shape, NEG_INF, F32)
    acc_ref[...] = jnp.zeros(acc_ref.shape, F32)
    earlier_tiles(0)
    earlier_tiles(1)
    for cd in range(chunks_tile):
        for hh in range(N_HEADS):
            for br in range(2):
                step(br, i * chunks_tile + cd, hh, True)

    def head_out(hh):
        o = oacc_ref[hh]
        for br in range(2):
            acc = acc_ref[br * N_HEADS + hh]
            o = o + gate(hh, br + 1) * (acc[0:HEAD_DIM, :] / acc[HEAD_DIM:HEAD_DIM + 1, :])
        return o

    for pr in range(N_HEADS // 2):
        o_ref[:, pr * LANES:(pr + 1) * LANES] = jnp.concatenate([head_out(2 * pr), head_out(2 * pr + 1)], axis=0).T


def _attn_prompt(qt, gnt, kcm, vcmt, ksa, vst, kwp, vwt, aggt):
    n_batch, _, _, t_len = qt.shape
    tq = TQ
    n_cmp = kcm.shape[2]
    assert tq % KEY_CHUNK == 0 and WINDOW % KEY_CHUNK == 0 and t_len // SLC_LEN <= SEL_ROWS
    assert n_cmp == KV_W, "one lane tile of compressed blocks"
    k_spec = pl.BlockSpec((None, N_KV, t_len, LANES), lambda b, i: (b, 0, 0, 0))
    vt_spec = pl.BlockSpec((None, N_KV, t_len // KEY_CHUNK, V_ROWS, KEY_CHUNK), lambda b, i: (b, 0, 0, 0, 0))
    cmp_spec = pl.BlockSpec((None, N_KV, n_cmp, KV_W), lambda b, i: (b, 0, 0, 0))
    return pl.pallas_call(
        functools.partial(_attn_prompt_kernel, t_len=t_len),
        grid=(n_batch, t_len // tq),
        in_specs=[pl.BlockSpec((None, N_HEADS, LANES, tq), lambda b, i: (b, 0, 0, i)),
                  pl.BlockSpec((None, LANES, tq), lambda b, i: (b, 0, i)),
                  cmp_spec, cmp_spec, k_spec, vt_spec, k_spec, vt_spec, _const_spec(aggt.shape)],
        out_specs=pl.BlockSpec((None, tq, NSA_W), lambda b, i: (b, i, 0)),
        out_shape=jax.ShapeDtypeStruct((n_batch, t_len, NSA_W), F32),
        scratch_shapes=[pltpu.VMEM((N_HEADS, HEAD_DIM, tq), F32), pltpu.VMEM((N_HEADS, LANES, tq), BF16),
                        pltpu.VMEM((2 * N_HEADS, 1, tq), F32), pltpu.VMEM((2 * N_HEADS, V_ROWS, tq), F32)],
        compiler_params=pltpu.CompilerParams(dimension_semantics=("arbitrary", "arbitrary"),
                                             vmem_limit_bytes=VMEM_LIMIT),
        name="attn_prompt",
    )(qt, gnt, kcm, vcmt, ksa, vst, kwp, vwt, aggt)


def _tail_math(x, o_a, sga, c_pre, sgb, sma, smb, w):
    (dwb_ref, clg_ref, clb_ref, pww_ref, pwb_ref, wpa_ref, wpb_ref, wo_ref, fg_ref) = w
    c = c_pre + dwb_ref[...]
    mu = jnp.mean(c, axis=-1, keepdims=True)
    var = jnp.mean(jnp.square(c - mu), axis=-1, keepdims=True)
    cn = (c - mu) * lax.rsqrt(var + EPS) * clg_ref[...] + clb_ref[...]
    cp = _dot(_silu(cn).astype(BF16), pww_ref[...]) + pwb_ref[...]
    br_a = _dot((o_a * sga).astype(BF16), wpa_ref[...])
    br_b = _dot((cp * sgb).astype(BF16), wpb_ref[...])
    h = sma * br_a + smb * br_b
    xo = x + _dot(h.astype(BF16), wo_ref[...])
    return xo * lax.rsqrt(jnp.mean(xo * xo, axis=-1, keepdims=True) + EPS) * fg_ref[...]


def _tail_prompt_kernel(x_ref, oa_ref, sga_ref, u_ref, sgb_ref, sma_ref, smb_ref, dw_ref, *rest):
    w, (y_ref, up_ref, sh_ref) = rest[:9], rest[9:]
    c_pre = _causal_dw_conv(up_ref, sh_ref, u_ref[...], dw_ref, pl.program_id(1) == 0)
    y_ref[...] = _tail_math(x_ref[...], oa_ref[...], sga_ref[...], c_pre, sgb_ref[...], sma_ref[...],
                            smb_ref[...], w)


def _tail_prompt(x, o_a, sga, u, sgb, sma, smb, dw, tail_w):
    n_batch, t_len, _ = x.shape
    tm = TM_TAIL

    def bt(width):
        return pl.BlockSpec((None, tm, width), lambda b, t: (b, t, 0))

    ins = [x, o_a, sga, u, sgb, sma, smb]
    return pl.pallas_call(
        _tail_prompt_kernel,
        grid=(n_batch, t_len // tm),
        in_specs=[bt(a.shape[-1]) for a in ins] + [_const_spec(dw.shape)] + [_const_spec(a.shape) for a in tail_w],
        out_specs=bt(D_MODEL),
        out_shape=jax.ShapeDtypeStruct((n_batch, t_len, D_MODEL), F32),
        scratch_shapes=[pltpu.VMEM((HIST + tm, CONV_C), F32), pltpu.VMEM((SUBLANES, HIST + tm, CONV_C), F32)],
        compiler_params=pltpu.CompilerParams(dimension_semantics=("arbitrary", "arbitrary"),
                                             vmem_limit_bytes=VMEM_LIMIT),
        name="tail_prompt",
    )(*ins, dw, *tail_w)


S_ROWS = 8
UP_ROWS = 40


def _tail_sample_kernel(x_ref, oa_ref, sga_ref, ups_ref, sgb_ref, sma_ref, smb_ref, dws_ref, *rest,
                        n_new):
    w, (y_ref, c_ref) = rest[:9], rest[9:]
    n_batch = ups_ref.shape[0]
    c_ref[...] = jnp.zeros(c_ref.shape, F32)

    def body(b, carry):
        up = ups_ref[b]
        for t in range(n_new):
            c_ref[pl.ds(b * S_ROWS + t, 1), :] = jnp.sum(up * dws_ref[t], axis=0, keepdims=True)
        return carry

    lax.fori_loop(0, n_batch, body, 0)
    y_ref[...] = _tail_math(x_ref[...], oa_ref[...], sga_ref[...], c_ref[...], sgb_ref[...], sma_ref[...],
                            smb_ref[...], w)


def _tail_sample(x, o_a, sga, up_s, sgb, sma, smb, dw_shift, tail_w, n_new):
    ins = [x, o_a, sga, up_s, sgb, sma, smb, dw_shift] + list(tail_w)
    m = x.shape[0]
    return pl.pallas_call(
        functools.partial(_tail_sample_kernel, n_new=n_new),
        grid=(1,),
        in_specs=[_const_spec(a.shape) for a in ins],
        out_specs=_const_spec((m, D_MODEL)),
        out_shape=jax.ShapeDtypeStruct((m, D_MODEL), F32),
        scratch_shapes=[pltpu.VMEM((m, CONV_C), F32)],
        compiler_params=pltpu.CompilerParams(dimension_semantics=("arbitrary",),
                                             vmem_limit_bytes=VMEM_LIMIT),
        name="tail_sample",
    )(*ins)


def _group_q(q_ref, g):
    q = q_ref[g * HPG:(g + 1) * HPG].reshape(HPG * S_ROWS, LANES).astype(F32)
    return q if g == 0 else pltpu.roll(q, HEAD_DIM, axis=1)


def _group_out(o, g):
    return o if g == 0 else pltpu.roll(o, HEAD_DIM, axis=1)


def _scmp_kernel(pt_ref, kpool_ref, vpool_ref, q_ref, perm_ref, pek_ref, w1k_ref, w2k_ref, pev_ref, w1v_ref,
                 w2v_ref, agg_ref, oc_ref, score_ref, ak_ref, av_ref, kbuf_ref, vbuf_ref, sem_ref,
                 *, n_pages_step, n_steps, past_len):
    p = n_pages_step
    b, pg = pl.program_id(0), pl.program_id(1)
    step = b * n_steps + pg
    n_total = pl.num_programs(0) * n_steps
    chunks_page = PAGE_SIZE // CMP_STRIDE
    rows = chunks_page * p

    def page_copies(step_idx, slot, lookup):
        if lookup:
            bb = step_idx // n_steps
            first = (step_idx - bb * n_steps) * p
        for i in range(p):
            page = pt_ref[bb, first + i] if lookup else 0
            for pool_ref, buf_ref in ((kpool_ref, kbuf_ref), (vpool_ref, vbuf_ref)):
                yield pltpu.make_async_copy(pool_ref.at[page], buf_ref.at[slot, i], sem_ref.at[slot])

    @pl.when(step == 0)
    def _():
        for cp in page_copies(step, 0, True):
            cp.start()

    @pl.when(step + 1 < n_total)
    def _():
        for cp in page_copies(step + 1, (step + 1) % 2, True):
            cp.start()

    slot = step % 2
    for cp in page_copies(step, slot, False):
        cp.wait()

    for buf_ref, w1_ref, a_ref in ((kbuf_ref, w1k_ref, ak_ref), (vbuf_ref, w1v_ref, av_ref)):
        blocks = []
        for i in range(p):
            y = _dot_nt(perm_ref[...], buf_ref[slot, i].astype(BF16))
            blocks.append(jnp.concatenate(
                [y[j * chunks_page:(j + 1) * chunks_page] for j in range(CMP_STRIDE)], axis=-1))
        xc = jnp.concatenate(blocks, axis=0).astype(BF16)
        a_ref[pl.ds(pl.multiple_of(pg * rows, rows), rows), :] = _dot(xc, w1_ref[...])

    @pl.when(pg == n_steps - 1)
    def _():
        kc = _compress_from_partials(ak_ref[...], pek_ref, w1k_ref, w2k_ref).astype(BF16)
        vc = _compress_from_partials(av_ref[...], pev_ref, w1v_ref, w2v_ref).astype(BF16)
        n_chunks = kc.shape[0]
        n_blocks = past_len // SLC_LEN + 1
        r32 = HPG * S_ROWS
        qpos32 = past_len + (_iota((r32, n_chunks), 0) & (S_ROWS - 1))
        cvalid = (_iota((r32, n_chunks), 1) * CMP_STRIDE + (CMP_LEN - 1)) <= qpos32
        nb_pad = agg_ref.shape[1]
        jl = _iota((S_ROWS, nb_pad), 1)
        qpos8 = past_len + _iota((S_ROWS, nb_pad), 0)
        for g in range(N_KV):
            qg = _group_q(q_ref, g).astype(BF16)
            p_c = _masked_softmax(_dot_nt(qg, kc), cvalid)
            oc_ref[g] = _group_out(_dot(p_c.astype(BF16), vc), g)
            psum = (p_c[0:S_ROWS] + p_c[S_ROWS:2 * S_ROWS] + p_c[2 * S_ROWS:3 * S_ROWS]
                    + p_c[3 * S_ROWS:4 * S_ROWS])
            imp = _dot(psum.astype(BF16), agg_ref[...])
            score_ref[g] = _block_scores(imp, jl, qpos8, n_blocks)


def _scmp(page_table, kpool, vpool, qp_s, cmp_w, agg_s, past_len):
    n_batch, n_pages = page_table.shape
    p = PAGES_PER_STEP
    n_steps = n_pages // p
    n_chunks = n_pages * (PAGE_SIZE // CMP_STRIDE)

    def cspec(shape):
        nd = len(shape)
        return pl.BlockSpec(shape, lambda b, s, pt: (0,) * nd)

    r_out = jnp.arange(PAGE_SIZE)
    chunks_page = PAGE_SIZE // CMP_STRIDE
    src = (r_out % chunks_page) * CMP_STRIDE + r_out // chunks_page
    perm = (src[:, None] == jnp.arange(PAGE_SIZE)[None, :]).astype(BF16)

    in_specs = ([pl.BlockSpec(memory_space=pl.ANY), pl.BlockSpec(memory_space=pl.ANY),
                 pl.BlockSpec((None, N_HEADS, S_ROWS, LANES), lambda b, s, pt: (0, 0, b, 0))]
                + [cspec(perm.shape)] + [cspec(w.shape) for w in cmp_w] + [cspec(agg_s.shape)])
    page_buf = pltpu.VMEM((2, p, KV_W, PAGE_SIZE), F32)
    nb_pad = agg_s.shape[1]
    out_specs = [pl.BlockSpec((None, N_KV, HPG * S_ROWS, LANES), lambda b, s, pt: (b, 0, 0, 0)),
                 pl.BlockSpec((None, N_KV, S_ROWS, nb_pad), lambda b, s, pt: (b, 0, 0, 0))]
    out_shape = [jax.ShapeDtypeStruct((n_batch, N_KV, HPG * S_ROWS, LANES), F32),
                 jax.ShapeDtypeStruct((n_batch, N_KV, S_ROWS, nb_pad), F32)]
    oc, score = pl.pallas_call(
        functools.partial(_scmp_kernel, n_pages_step=p, n_steps=n_steps, past_len=past_len),
        grid_spec=pltpu.PrefetchScalarGridSpec(
            num_scalar_prefetch=1, grid=(n_batch, n_steps), in_specs=in_specs, out_specs=out_specs,
            scratch_shapes=[pltpu.VMEM((n_chunks, 2 * KV_W), F32)] * 2
            + [page_buf, page_buf, pltpu.SemaphoreType.DMA((2,))]),
        out_shape=out_shape,
        compiler_params=pltpu.CompilerParams(dimension_semantics=("arbitrary", "arbitrary"),
                                             vmem_limit_bytes=VMEM_LIMIT),
        name="sample_cmp",
    )(page_table, kpool, vpool, qp_s, perm, *cmp_w, agg_s)
    return oc, _top_blocks(score.reshape(n_batch * N_KV * S_ROWS, nb_pad))


def _top_blocks_kernel(score_ref, idx_ref):
    score = score_ref[...]
    rows, nb_pad = score.shape
    jlf = _iota((rows, nb_pad), 1).astype(F32)
    lane = _iota((rows, LANES), 1)
    idx = jnp.zeros((rows, LANES), F32)
    for k in range(TOP_N):
        mx = jnp.max(score, axis=-1, keepdims=True)
        am = jnp.min(jnp.where(score == mx, jlf, float(nb_pad)), axis=-1, keepdims=True)
        idx = jnp.where(lane == k, am, idx)
        score = jnp.where(jlf == am, -3.0, score)
    idx_ref[...] = idx.astype(jnp.int32)


def _top_blocks(score):
    rows = score.shape[0]
    return pl.pallas_call(
        _top_blocks_kernel,
        grid=(1,),
        in_specs=[_const_spec(score.shape)],
        out_specs=_const_spec((rows, LANES)),
        out_shape=jax.ShapeDtypeStruct((rows, LANES), jnp.int32),
        compiler_params=pltpu.CompilerParams(dimension_semantics=("arbitrary",), vmem_limit_bytes=VMEM_LIMIT),
        name="sample_topk",
    )(score)


def _ssel_kernel(pt_ref, idx_ref, kpool_ref, vpool_ref, q_ref, ksn_ref, vsn_ref, kwn_ref, vwn_ref,
                 kwb_ref, vwb_ref, gn_ref, oc_ref, o_ref, kbuf_ref, vbuf_ref, sem_ref,
                 *, n_new, past_len):
    b = pl.program_id(0)
    n_pool_blocks = past_len // SLC_LEN
    blocks_page = PAGE_SIZE // SLC_LEN

    page_shift = blocks_page.bit_length() - 1
    assert blocks_page == 1 << page_shift

    def block_copies(bb, ring, lookup):
        for g in range(N_KV):
            for qi in range(n_new):
                for k in range(TOP_N):
                    page = 0
                    if lookup:
                        j = jnp.minimum(idx_ref[bb, (g * n_new + qi) * TOP_N + k], n_pool_blocks - 1)
                        page = pt_ref[bb, lax.shift_right_logical(j, page_shift)]
                    for pool_ref, buf_ref in ((kpool_ref, kbuf_ref), (vpool_ref, vbuf_ref)):
                        yield pltpu.make_async_copy(
                            pool_ref.at[page, pl.ds(g * HEAD_DIM, HEAD_DIM), :],
                            buf_ref.at[ring, g, :, pl.ds((qi * TOP_N + k) * PAGE_SIZE, PAGE_SIZE)],
                            sem_ref.at[ring])

    @pl.when(b == 0)
    def _():
        for cp in block_copies(b, 0, True):
            cp.start()

    @pl.when(b + 1 < pl.num_programs(0))
    def _():
        for cp in block_copies(b + 1, (b + 1) % 2, True):
            cp.start()

    ring = b % 2

    r32 = HPG * S_ROWS
    n_sel = n_new * TOP_N * PAGE_SIZE
    new_rows = LANES
    newcol = _iota((r32, new_rows), 1)
    qrow_new = _iota((r32, new_rows), 0) & (S_ROWS - 1)

    def pad_new(ref):
        return jnp.concatenate([ref[...], jnp.zeros((new_rows - S_ROWS, KV_W), F32)], axis=0).astype(BF16)

    ks_new, vs_new, kw_new, vw_new = pad_new(ksn_ref), pad_new(vsn_ref), pad_new(kwn_ref), pad_new(vwn_ref)

    w_buf = kwb_ref.shape[1]
    wpos = jnp.concatenate([past_len - w_buf + _iota((r32, w_buf), 1), past_len + newcol], axis=1)
    qpos_w = past_len + (_iota((r32, w_buf + new_rows), 0) & (S_ROWS - 1))
    wmask = (wpos <= qpos_w) & (wpos >= qpos_w - WINDOW)
    wmask = wmask & jnp.concatenate([jnp.full((r32, w_buf), True), newcol < n_new], axis=1)
    kw_t = kwb_ref[...].astype(BF16)
    vw_t = vwb_ref[...].astype(BF16)
    gates = _sigmoid(gn_ref[...])

    o_w = []
    q_g = []
    for g in range(N_KV):
        qg = _group_q(q_ref, g).astype(BF16)
        q_g.append(qg)
        s_w = jnp.concatenate([_dot(qg, kw_t), _dot_nt(qg, kw_new)], axis=1)
        p_w = _masked_softmax(s_w, wmask).astype(BF16)
        o_w.append(_group_out(_dot_nt(p_w[:, :w_buf], vw_t) + _dot(p_w[:, w_buf:], vw_new), g))

    for cp in block_copies(b, ring, False):
        cp.wait()

    pad64 = jnp.zeros((r32, LANES - HEAD_DIM), F32)
    row_qi = _iota((r32, PAGE_SIZE), 0) & (S_ROWS - 1)
    lane_part = lax.shift_right_logical(_iota((r32, PAGE_SIZE), 1), SLC_LEN.bit_length() - 1)
    for g in range(N_KV):
        q64 = q_ref[g * HPG:(g + 1) * HPG].reshape(r32, LANES)[:, :HEAD_DIM]
        pieces = []
        new_limit = jnp.zeros((r32, new_rows), jnp.int32)
        for qi in range(n_new):
            n_hit = jnp.int32(0)
            for k in range(TOP_N):
                j = idx_ref[b, (g * n_new + qi) * TOP_N + k]
                part = jnp.where(j < n_pool_blocks, j & (blocks_page - 1), -1)
                pieces.append((row_qi == qi) & (lane_part == part))
                n_hit = n_hit + jnp.where(j == n_pool_blocks, 1, 0)
            new_limit = jnp.where(qrow_new == qi, jnp.where(n_hit > 0, n_new, 0), new_limit)
        nmask = (newcol <= qrow_new) & (newcol < new_limit)
        s_s = jnp.concatenate([_dot(q64, kbuf_ref[ring, g].astype(BF16)), _dot_nt(q_g[g], ks_new)], axis=1)
        p_s = _masked_softmax(s_s, jnp.concatenate(pieces + [nmask], axis=1)).astype(BF16)
        o_pool = _dot_nt(p_s[:, :n_sel], vbuf_ref[ring, g].astype(BF16))
        o_s = jnp.concatenate([o_pool, pad64], axis=1) + _group_out(_dot(p_s[:, n_sel:], vs_new), g)
        o_c = oc_ref[g]
        heads = []
        for h in range(HPG):
            col = (g * HPG + h) * 3
            rs = slice(h * S_ROWS, (h + 1) * S_ROWS)
            heads.append(gates[:, col:col + 1] * o_c[rs] + gates[:, col + 1:col + 2] * o_s[rs]
                         + gates[:, col + 2:col + 3] * o_w[g][rs])
        for pr in range(HPG // 2):
            c0 = (g * HPG + 2 * pr) * HEAD_DIM
            o_ref[:, c0:c0 + LANES] = _pair_heads(heads[2 * pr], heads[2 * pr + 1])


def _ssel(page_table, idx, kpool, vpool, qp_s, ks_n, vs_n, kw_n, vw_n, kw_buf, vw_buf, gn_s, oc,
          n_new, past_len):
    n_batch = page_table.shape[0]
    w_buf = kw_buf.shape[2]
    rows8 = pl.BlockSpec((S_ROWS, KV_W), lambda b, pt, ix: (b, 0))
    wbuf_spec = pl.BlockSpec((None, KV_W, w_buf), lambda b, pt, ix: (b, 0, 0))
    in_specs = [pl.BlockSpec(memory_space=pl.ANY), pl.BlockSpec(memory_space=pl.ANY),
                pl.BlockSpec((None, N_HEADS, S_ROWS, LANES), lambda b, pt, ix: (0, 0, b, 0)),
                rows8, rows8, rows8, rows8, wbuf_spec, wbuf_spec,
                pl.BlockSpec((S_ROWS, LANES), lambda b, pt, ix: (b, 0)),
                pl.BlockSpec((None, N_KV, HPG * S_ROWS, LANES), lambda b, pt, ix: (b, 0, 0, 0))]
    return pl.pallas_call(
        functools.partial(_ssel_kernel, n_new=n_new, past_len=past_len),
        grid_spec=pltpu.PrefetchScalarGridSpec(
            num_scalar_prefetch=2, grid=(n_batch,), in_specs=in_specs,
            out_specs=pl.BlockSpec((S_ROWS, NSA_W), lambda b, pt, ix: (b, 0)),
            scratch_shapes=[pltpu.VMEM((2, N_KV, HEAD_DIM, n_new * TOP_N * PAGE_SIZE), F32)] * 2
            + [pltpu.SemaphoreType.DMA((2,))]),
        out_shape=jax.ShapeDtypeStruct((n_batch * S_ROWS, NSA_W), F32),
        compiler_params=pltpu.CompilerParams(dimension_semantics=("arbitrary",),
                                             vmem_limit_bytes=VMEM_LIMIT),
        name="sample_sel",
    )(page_table, idx, kpool, vpool, qp_s, ks_n, vs_n, kw_n, vw_n, kw_buf, vw_buf, gn_s, oc)


def _w_in_kernel(w_ref, wa_ref, wgn_ref, wr_ref):
    n_gate = 3 * N_HEADS
    w = w_ref[...]
    wa_ref[...] = w[:, :A_END].astype(BF16)
    lane = _iota((w.shape[0], LANES), 1)
    wgn_ref[...] = jnp.where(lane < n_gate, w[:, A_END:A_END + LANES], 0.0).astype(BF16)
    wr_ref[...] = w[:, A_END + n_gate:].astype(BF16)


def _prep_w_in(w_in):
    rows, cols = w_in.shape
    assert cols == A_END + 3 * N_HEADS + R_END
    tr = 128
    widths = (A_END, LANES, R_END)
    return pl.pallas_call(
        _w_in_kernel,
        grid=(rows // tr,),
        in_specs=[pl.BlockSpec((tr, cols), lambda r: (r, 0))],
        out_specs=[pl.BlockSpec((tr, w), lambda r: (r, 0)) for w in widths],
        out_shape=[jax.ShapeDtypeStruct((rows, w), BF16) for w in widths],
        compiler_params=pltpu.CompilerParams(dimension_semantics=("arbitrary",), vmem_limit_bytes=VMEM_LIMIT),
        name="w_in_groups",
    )(w_in)


def _prep_cmp(pe, w1, w2):
    w1r = w1.reshape(CMP_R, CMP_STRIDE, HEAD_DIM, HEAD_DIM)
    eye = jnp.eye(N_KV, dtype=w1.dtype)
    w1big = jnp.einsum('rjdh,ge->jgdreh', w1r, eye).reshape(CMP_STRIDE * KV_W, CMP_R * KV_W)
    pe_rows = jnp.broadcast_to(pe.reshape(CMP_R, CMP_STRIDE, 1, HEAD_DIM),
                               (CMP_R, CMP_STRIDE, N_KV, HEAD_DIM)).reshape(CMP_R, CMP_STRIDE * KV_W)
    pe_rows = jnp.pad(pe_rows, ((0, SUBLANES - CMP_R), (0, 0)))
    w2big = jnp.einsum('hd,ge->ghed', w2, eye).reshape(KV_W, KV_W)
    return pe_rows.astype(BF16), w1big.astype(BF16), w2big.astype(BF16)


def _agg_matrix(n_rows, n_blocks, n_cols, col0):
    i = jnp.arange(n_rows)[:, None] * CMP_STRIDE
    jj = jnp.arange(n_cols)[None, :] - col0
    hit = (i < jj * SLC_LEN + SLC_LEN) & (i + CMP_LEN > jj * SLC_LEN) & (jj >= 0) & (jj < n_blocks)
    return hit.astype(BF16)


def kernel(x_prompt, x_sample, cache_k_cmp, cache_v_cmp, cache_k_slc, cache_v_slc, cache_k_win, cache_v_win, state_conv, page_table, ln_g, w_in, pe_k, w1_k, w2_k, pe_v, w1_v, w2_v, dw_k, dw_b, cln_g, cln_b, pw_w, pw_b, w_pa, w_pb, w_o, final_g):
    assert w_in.shape[0] == 1, "single-layer stack"
    n_b, t_len, _ = x_prompt.shape
    n_db, n_new, _ = x_sample.shape
    n_pages = page_table.shape[1]
    past_len = n_pages * PAGE_SIZE
    w_buf = cache_k_win.shape[2]
    assert n_new <= S_ROWS and t_len % TQ == 0 and t_len >= WINDOW + TQ
    assert n_new < CMP_STRIDE, "the new rows must not complete a compression chunk"
    assert t_len // SLC_LEN <= LANES - HEAD_DIM and past_len % PAGE_SIZE == 0

    w_all = _prep_w_in(w_in[0])
    lng = ln_g[0].reshape(1, D_MODEL)
    cmp_w = _prep_cmp(pe_k[0], w1_k[0], w2_k[0]) + _prep_cmp(pe_v[0], w1_v[0], w2_v[0])
    tail_w = (dw_b[0].reshape(1, CONV_C), cln_g[0].reshape(1, CONV_C), cln_b[0].reshape(1, CONV_C),
              pw_w[0].astype(BF16), pw_b[0].reshape(1, CONV_C), w_pa[0].astype(BF16), w_pb[0].astype(BF16),
              w_o[0].astype(BF16), final_g.reshape(1, D_MODEL))
    dw = jnp.pad(dw_k[0], ((0, 1), (0, 0)))

    (qt, kct, vct, kst, vst, kwt, vwt, gnt, sga, u, sgb, sma, smb, k_c, v_c, ksa, vst_g, kwp, vwt_g) = _proj(
        x_prompt.reshape(n_b * t_len, D_MODEL), lng, w_all, n_b, t_len, (0, 1), True)
    seq = lambda a: a.reshape(n_b, t_len, a.shape[-1])
    kcm, vcmt = _cmp_prompt(seq(k_c), seq(v_c), cmp_w)
    n_cmp_rows = t_len // CMP_STRIDE
    aggt_p = _agg_matrix(n_cmp_rows, t_len // SLC_LEN, LANES, 0).T
    o_a = _attn_prompt(qt, gnt, kcm, vcmt, ksa, vst_g, kwp, vwt_g, aggt_p)
    y_prompt = _tail_prompt(x_prompt, o_a, seq(sga), seq(u), seq(sgb), seq(sma), seq(smb), dw, tail_w)

    xs = jnp.pad(x_sample, ((0, 0), (0, S_ROWS - n_new), (0, 0))).reshape(n_db * S_ROWS, D_MODEL)
    (qp_s, kct_s, vct_s, kst_s, vst_s, kwt_s, vwt_s, gn_s, sga_s, u_s, sgb_s, sma_s, smb_s,
     k_s_s, v_s_s, k_w_s, v_w_s) = _proj(xs, lng, w_all, 1, n_db * S_ROWS, (2, 3, 4, 5), False)
    pool = lambda c: jnp.transpose(c[0], (0, 2, 3, 1)).reshape(c.shape[1], KV_W, PAGE_SIZE)
    n_blocks_s = past_len // SLC_LEN + 1
    nb_pad = -(-n_blocks_s // LANES) * LANES
    agg_s = _agg_matrix(past_len // CMP_STRIDE, n_blocks_s, nb_pad, 0)
    agg_s = agg_s.at[past_len // CMP_STRIDE - 1].set(0)
    oc_s, idx_s = _scmp(page_table, pool(cache_k_cmp), pool(cache_v_cmp), qp_s, cmp_w, agg_s, past_len)
    idx_flat = idx_s.reshape(n_db, N_KV, S_ROWS, LANES)[:, :, :n_new, :TOP_N].reshape(n_db, N_KV * n_new * TOP_N)
    win = lambda c: jnp.transpose(c[0], (0, 2, 3, 1)).reshape(n_db, KV_W, w_buf)
    o_a_s = _ssel(page_table, idx_flat, pool(cache_k_slc), pool(cache_v_slc), qp_s, k_s_s, v_s_s, k_w_s,
                  v_w_s, win(cache_k_win), win(cache_v_win), gn_s, oc_s, n_new, past_len)
    u_new = u_s.reshape(n_db, S_ROWS, CONV_C)[:, :n_new]
    up_s = jnp.concatenate([state_conv[0], u_new], axis=1)
    up_pad = jnp.pad(up_s, ((0, 0), (0, UP_ROWS - up_s.shape[1]), (0, 0)))
    dw_shift = jnp.stack([jnp.pad(dw_k[0], ((t, UP_ROWS - CONV_W - t), (0, 0))) for t in range(n_new)])
    y_s = _tail_sample(xs, o_a_s, sga_s, up_pad, sgb_s, sma_s, smb_s, dw_shift, tail_w, n_new)
    y_sample = y_s.reshape(n_db, S_ROWS, D_MODEL)[:, :n_new]

    def p_rows(at):
        return jnp.transpose(at.reshape(n_b, N_KV, HEAD_DIM, at.shape[-1]), (0, 3, 1, 2))[None]

    def s_rows(at):
        a = at.reshape(N_KV, HEAD_DIM, n_db, S_ROWS)[:, :, :, :n_new]
        return jnp.transpose(a, (2, 3, 0, 1))[None]

    win_p = min(WINDOW, t_len)
    p_conv = seq(u)[:, -(CONV_W - 1):][None]
    s_k_win = jnp.concatenate([cache_k_win[0], s_rows(kwt_s)[0]], axis=1)[:, -w_buf:][None]
    s_v_win = jnp.concatenate([cache_v_win[0], s_rows(vwt_s)[0]], axis=1)[:, -w_buf:][None]
    s_conv = up_s[:, -(CONV_W - 1):][None]
    return (y_prompt, y_sample, p_rows(kct), p_rows(vct), p_rows(kst), p_rows(vst),
            p_rows(kwt[:, :, -win_p:]), p_rows(vwt[:, :, -win_p:]), p_conv,
            s_rows(kct_s), s_rows(vct_s), s_rows(kst_s), s_rows(vst_s), s_k_win, s_v_win, s_conv)
```

```python
import functools

import jax
import jax.numpy as jnp
from jax import lax
from jax.experimental import pallas as pl
from jax.experimental.pallas import tpu as pltpu

F32 = jnp.float32
BF16 = jnp.bfloat16

D_MODEL = 1024
N_HEADS = 8
HEAD_DIM = 64
N_KV = 2
HPG = N_HEADS // N_KV
NSA_W = N_HEADS * HEAD_DIM
KV_W = N_KV * HEAD_DIM
CMP_LEN = 32
CMP_STRIDE = 16
CMP_R = CMP_LEN // CMP_STRIDE
SLC_LEN = 64
TOP_N = 16
WINDOW = 512
CONV_C = 512
CONV_W = 31
PAGE_SIZE = 128
EPS = 1e-6
FORCED_SCORE = 1e4
NEG_INF = -1e30
MASK_BIAS = -1e4

LANES = 128
SUBLANES = 8
VMEM_LIMIT = 56 * 1024 * 1024

A_KV = NSA_W
A_END = A_KV + 6 * KV_W
R_GLU = NSA_W
R_GB = R_GLU + 2 * CONV_C
R_MRG = R_GB + CONV_C
R_END = R_MRG + 2 * D_MODEL
W_REST = A_END + LANES

TM_PROJ = 256
TQ = 256
TM_TAIL = 256
PAGES_PER_STEP = 64


def _sigmoid(x):
    return jax.nn.sigmoid(x)


def _silu(x):
    return jax.nn.silu(x)


def _dot(a, b):
    return jnp.dot(a, b, preferred_element_type=F32)


def _dot_nt(a, b):
    return lax.dot_general(a, b, (((1,), (1,)), ((), ())), preferred_element_type=F32)


def _iota(shape, dim):
    return lax.broadcasted_iota(jnp.int32, shape, dim)


def _const_spec(shape):
    nd = len(shape)
    return pl.BlockSpec(shape, lambda *_: (0,) * nd)


HIST = 32


def _causal_dw_conv(up_ref, sh_ref, u, dw_ref, first_tile):
    tm = u.shape[0]

    @pl.when(first_tile)
    def _():
        up_ref[0:HIST, :] = jnp.zeros((HIST, CONV_C), F32)

    up_ref[HIST:HIST + tm, :] = u
    first = HIST - (CONV_W - 1)
    n_sh = HIST + tm - SUBLANES
    acc = jnp.zeros((tm, CONV_C), F32)
    for phase in range(SUBLANES):
        tiles = [(first + k) // SUBLANES for k in range(CONV_W) if (first + k) % SUBLANES == phase]
        if tiles and phase:
            sh_ref[phase, 0:n_sh, :] = up_ref[pl.ds(phase, n_sh), :]
        for a in tiles:
            k = a * SUBLANES + phase - first
            rows = up_ref[pl.ds(a * SUBLANES, tm), :] if phase == 0 else sh_ref[phase, pl.ds(a * SUBLANES, tm), :]
            acc = acc + rows * dw_ref[k:k + 1, :]
    up_ref[0:HIST, :] = up_ref[tm:tm + HIST, :]
    return acc


def _proj_kernel(x_ref, lng_ref, w_ref, qp_ref, kct_ref, vct_ref, kst_ref, vst_ref,
                 kwt_ref, vwt_ref, gn_ref, sga_ref, u_ref, sgb_ref, sma_ref, smb_ref, *more_refs,
                 tiles_per_batch, row_ids, attn_ops):
    x = x_ref[...]
    tm = x.shape[0]
    xn = x * lax.rsqrt(jnp.mean(x * x, axis=-1, keepdims=True) + EPS) * lng_ref[...]
    xn = xn.astype(BF16)

    zq = _dot(xn, w_ref[:, :A_KV]) * (HEAD_DIM ** -0.5 * (LOG2_E if attn_ops else 1.0))
    zgn = _dot(xn, w_ref[:, A_END:W_REST])
    lane = _iota((tm, LANES), 1)
    for pr in range(N_HEADS // 2):
        two = zq[:, pr * LANES:(pr + 1) * LANES]
        if attn_ops:
            two_t = two.T
            pad = jnp.zeros((LANES - HEAD_DIM, tm), F32)
            for o in range(2):
                qp_ref[2 * pr + o] = jnp.concatenate(
                    [two_t[o * HEAD_DIM:(o + 1) * HEAD_DIM], pad], axis=0).astype(BF16)
        else:
            qp_ref[2 * pr] = jnp.where(lane < HEAD_DIM, two, 0.0).astype(BF16)
            qp_ref[2 * pr + 1] = jnp.where(lane < HEAD_DIM, pltpu.roll(two, HEAD_DIM, axis=1), 0.0).astype(BF16)
    gn_ref[...] = zgn.T if attn_ops else zgn

    zkv = _dot(xn, w_ref[:, A_KV:A_END])
    kv = [zkv[:, i * KV_W:(i + 1) * KV_W] for i in range(6)]
    kvt = [v.T for v in kv]
    for r, v in zip((kct_ref, vct_ref, kst_ref, vst_ref, kwt_ref, vwt_ref), kvt):
        r[...] = v
    for r, i in zip(more_refs[:len(row_ids)], row_ids):
        r[...] = kv[i]

    if attn_ops:
        ksa_ref, vst_g_ref, kwp_ref, vwt_g_ref = more_refs[len(row_ids):]
        t0 = (pl.program_id(0) % tiles_per_batch) * tm
        blk = (t0 + _iota((tm, LANES), 0)) >> 6
        onehot = jnp.where((lane - HEAD_DIM) == blk, 1.0, 0.0)
        ones_row = jnp.where(_iota((V_ROWS - HEAD_DIM, tm), 0) == 0, 1.0, 0.0)
        for g in range(N_KV):
            rows_g = slice(g * HEAD_DIM, (g + 1) * HEAD_DIM)

            def grp(v):
                return v if g == 0 else pltpu.roll(v, HEAD_DIM, axis=1)

            ksa_ref[g] = jnp.where(lane < HEAD_DIM, grp(kv[2]), onehot).astype(BF16)
            kwp_ref[g] = grp(kv[4]).astype(BF16)
            for ref, vt in ((vst_g_ref, kvt[3]), (vwt_g_ref, kvt[5])):
                v_aug = jnp.concatenate([vt[rows_g], ones_row], axis=0).astype(BF16)
                for j in range(tm // KEY_CHUNK):
                    ref[g, j] = v_aug[:, j * KEY_CHUNK:(j + 1) * KEY_CHUNK]

    zglu = _dot(xn, w_ref[:, W_REST + R_GLU:W_REST + R_GB])
    u = zglu[:, :CONV_C] * _sigmoid(zglu[:, CONV_C:])
    u_ref[...] = u
    sga_ref[...] = _silu(_dot(xn, w_ref[:, W_REST:W_REST + R_GLU]))
    sgb_ref[...] = _silu(_dot(xn, w_ref[:, W_REST + R_GB:W_REST + R_MRG]))
    zm = _dot(xn, w_ref[:, W_REST + R_MRG:W_REST + R_END])
    sma_ref[...] = _sigmoid(zm[:, :D_MODEL])
    smb_ref[...] = _sigmoid(zm[:, D_MODEL:])


def _proj(x2d, ln_g, w_all, n_batch, t_len, row_ids, attn_ops):
    m = x2d.shape[0]
    tm = min(TM_PROJ, t_len)
    tpb = t_len // tm
    grid = (m // tm,)

    def row_spec(width):
        return pl.BlockSpec((tm, width), lambda i: (i, 0))

    def bt_spec(lead, width):
        return pl.BlockSpec((None, lead, tm, width), lambda i: (i // tpb, 0, i % tpb, 0))

    t_spec = pl.BlockSpec((None, KV_W, tm), lambda i: (i // tpb, 0, i % tpb))
    if attn_ops:
        out_shape = [jax.ShapeDtypeStruct((n_batch, N_HEADS, LANES, t_len), BF16)]
        out_specs = [pl.BlockSpec((None, N_HEADS, LANES, tm), lambda i: (i // tpb, 0, 0, i % tpb))]
    else:
        out_shape = [jax.ShapeDtypeStruct((n_batch, N_HEADS, t_len, LANES), BF16)]
        out_specs = [bt_spec(N_HEADS, LANES)]
    out_shape += [jax.ShapeDtypeStruct((n_batch, KV_W, t_len), F32)] * 6
    out_specs += [t_spec] * 6
    if attn_ops:
        out_shape.append(jax.ShapeDtypeStruct((n_batch, LANES, t_len), F32))
        out_specs.append(t_spec)
    else:
        out_shape.append(jax.ShapeDtypeStruct((m, LANES), F32))
        out_specs.append(row_spec(LANES))
    for width in (NSA_W, CONV_C, CONV_C, D_MODEL, D_MODEL):
        out_shape.append(jax.ShapeDtypeStruct((m, width), F32))
        out_specs.append(row_spec(width))
    out_shape += [jax.ShapeDtypeStruct((m, KV_W), F32)] * len(row_ids)
    out_specs += [row_spec(KV_W)] * len(row_ids)
    if attn_ops:
        cpt = tm // KEY_CHUNK
        k_shape = jax.ShapeDtypeStruct((n_batch, N_KV, t_len, LANES), BF16)
        vt_shape = jax.ShapeDtypeStruct((n_batch, N_KV, t_len // KEY_CHUNK, V_ROWS, KEY_CHUNK), BF16)
        vt_spec = pl.BlockSpec((None, N_KV, cpt, V_ROWS, KEY_CHUNK), lambda i: (i // tpb, 0, i % tpb, 0, 0))
        out_shape += [k_shape, vt_shape, k_shape, vt_shape]
        out_specs += [bt_spec(N_KV, LANES), vt_spec, bt_spec(N_KV, LANES), vt_spec]

    return pl.pallas_call(
        functools.partial(_proj_kernel, tiles_per_batch=tpb, row_ids=row_ids, attn_ops=attn_ops),
        grid=grid,
        in_specs=[row_spec(D_MODEL), _const_spec((1, D_MODEL)), _const_spec(w_all.shape)],
        out_specs=out_specs,
        out_shape=out_shape,
        compiler_params=pltpu.CompilerParams(dimension_semantics=("arbitrary",),
                                             vmem_limit_bytes=VMEM_LIMIT),
        name="proj",
    )(x2d, ln_g, w_all)


def _compress_from_partials(per, pe_ref, w1_ref, w2_ref):
    n = per.shape[0]
    hp = _dot(pe_ref[...], w1_ref[...])
    h0 = hp[0:1, :KV_W] + hp[1:2, KV_W:]
    h = h0 + per[:, :KV_W] + pltpu.roll(per[:, KV_W:], n - 1, axis=0)
    return _dot(_silu(h).astype(BF16), w2_ref[...])


def _chunk_rows(ref, n_chunks):
    return jnp.concatenate([ref[pl.ds(j, n_chunks, stride=CMP_STRIDE), :] for j in range(CMP_STRIDE)],
                           axis=-1)


def _cmp_prompt_kernel(kc_ref, vc_ref, pek_ref, w1k_ref, w2k_ref, pev_ref, w1v_ref, w2v_ref,
                       ok_ref, ov_ref):
    n_chunks = kc_ref.shape[0] // CMP_STRIDE
    for x_ref, pe_ref, w1_ref, w2_ref, o_ref, transposed in (
            (kc_ref, pek_ref, w1k_ref, w2k_ref, ok_ref, False), (vc_ref, pev_ref, w1v_ref, w2v_ref, ov_ref, True)):
        per = _dot(_chunk_rows(x_ref, n_chunks).astype(BF16), w1_ref[...])
        c = _compress_from_partials(per, pe_ref, w1_ref, w2_ref)
        for g, cg in enumerate((c, pltpu.roll(c, HEAD_DIM, axis=1))):
            o_ref[g] = (cg.T if transposed else cg).astype(BF16)


def _cmp_prompt(k_c, v_c, cmp_w):
    n_batch, t_len, _ = k_c.shape
    n_chunks = t_len // CMP_STRIDE
    seq = pl.BlockSpec((None, t_len, KV_W), lambda b: (b, 0, 0))
    out = pl.BlockSpec((None, N_KV, n_chunks, KV_W), lambda b: (b, 0, 0, 0))
    w_specs = [_const_spec(w.shape) for w in cmp_w]
    return pl.pallas_call(
        _cmp_prompt_kernel,
        grid=(n_batch,),
        in_specs=[seq, seq] + w_specs,
        out_specs=[out, out],
        out_shape=[jax.ShapeDtypeStruct((n_batch, N_KV, n_chunks, KV_W), BF16)] * 2,
        compiler_params=pltpu.CompilerParams(dimension_semantics=("arbitrary",),
                                             vmem_limit_bytes=VMEM_LIMIT),
        name="cmp_prompt",
    )(k_c, v_c, *cmp_w)


def _masked_softmax(s, mask):
    s = jnp.where(mask, s, NEG_INF)
    e = jnp.where(mask, jnp.exp(s - jnp.max(s, axis=-1, keepdims=True)), 0.0)
    return e / jnp.maximum(jnp.sum(e, axis=-1, keepdims=True), 1e-30)


def _softmax_rows(s, mask):
    s = jnp.where(mask, s, NEG_INF)
    e = jnp.where(mask, jnp.exp2(s - jnp.max(s, axis=0, keepdims=True)), 0.0)
    return e / jnp.maximum(jnp.sum(e, axis=0, keepdims=True), 1e-30)


def _block_scores(imp, j, qpos, n_blocks):
    cur = qpos >> 6
    forced = (j == 0) | (j == cur) | (j == cur - 1)
    score = jnp.where(forced, FORCED_SCORE, imp)
    score = jnp.where(j * SLC_LEN <= qpos, score, -1.0)
    return jnp.where((j >= 0) & (j < n_blocks), score, -2.0)


def _pair_heads(o_even, o_odd):
    lane = _iota(o_even.shape, 1)
    return jnp.where(lane < HEAD_DIM, o_even, pltpu.roll(o_odd, HEAD_DIM, axis=1))


KEY_CHUNK = 128
V_ROWS = LANES
LOG2_E = 1.4426950408889634
SEL_ROWS = 32


def _attn_prompt_kernel(qt_ref, gnt_ref, kcm_ref, vcmt_ref, ksa_ref, vst_ref, kwp_ref, vwt_ref, aggt_ref,
                        o_ref, oacc_ref, qsel_ref, m_ref, acc_ref, *, t_len):
    tq = gnt_ref.shape[1]
    n_blocks = t_len // SLC_LEN
    i = pl.program_id(1)
    t0 = i * tq
    chunks_tile = tq // KEY_CHUNK

    gates = _sigmoid(gnt_ref[...])
    qpos_c = t0 + _iota((KV_W, tq), 1)
    cvalid = (_iota((KV_W, tq), 0) * CMP_STRIDE + (CMP_LEN - 1)) <= qpos_c
    jblk = _iota((SEL_ROWS, tq), 0)
    qpos_j = t0 + _iota((SEL_ROWS, tq), 1)
    row_minus_lane = _iota((KEY_CHUNK, tq), 0) - _iota((KEY_CHUNK, tq), 1)

    def gate(hh, branch):
        r = hh * 3 + branch
        return gates[r:r + 1, :]

    flash_ops = ((qsel_ref, ksa_ref, vst_ref, 0, False),
                 (qt_ref, kwp_ref, vwt_ref, jnp.maximum(i - WINDOW // tq, 0), True))

    def step(br, c, hh, diagonal):
        q_ref, k_ref, vt_ref, _, banded = flash_ops[br]
        g, st = hh // HPG, br * N_HEADS + hh
        kc = k_ref[g, pl.ds(pl.multiple_of(c * KEY_CHUNK, KEY_CHUNK), KEY_CHUNK), :]
        s = _dot(kc, q_ref[hh])
        off = c * KEY_CHUNK - t0
        if diagonal:
            s = jnp.where(row_minus_lane + off <= 0, s, NEG_INF)
        elif banded:
            s = jnp.where(row_minus_lane + off >= -WINDOW, s, NEG_INF)
        m = m_ref[st]
        m_new = jnp.maximum(m, jnp.max(s, axis=0, keepdims=True))
        p = jnp.exp2(s - m_new).astype(BF16)
        acc_ref[st] = jnp.exp2(m - m_new) * acc_ref[st] + _dot(vt_ref[g, c], p)
        m_ref[st] = m_new

    def earlier_tiles(br):
        t_lo = flash_ops[br][3]

        def tile_steps(t):
            for cd in range(chunks_tile):
                for hh in range(N_HEADS):
                    step(br, t * chunks_tile + cd, hh, False)

        n_pairs = lax.shift_right_logical(i - t_lo, 1)

        def body(u, carry):
            tile_steps(t_lo + 2 * u)
            tile_steps(t_lo + 2 * u + 1)
            return carry

        lax.fori_loop(0, n_pairs, body, 0)

        @pl.when(t_lo + 2 * n_pairs < i)
        def _():
            tile_steps(i - 1)

    for g in range(N_KV):
        psum = None
        for h in range(HPG):
            hh = g * HPG + h
            p_c = _softmax_rows(_dot(kcm_ref[g], qt_ref[hh]), cvalid)
            oacc_ref[hh] = gate(hh, 0) * _dot(vcmt_ref[g, 0:HEAD_DIM, :], p_c.astype(BF16))
            psum = p_c if psum is None else psum + p_c
        imp = _dot(aggt_ref[...], psum.astype(BF16))[:SEL_ROWS]
        score = _block_scores(imp, jblk, qpos_j, n_blocks)
        rank = jnp.zeros((SEL_ROWS, tq), jnp.int32)
        for jp in range(n_blocks):
            other = score[jp:jp + 1, :]
            ahead = (other > score) | ((other == score) & (jblk > jp))
            rank = rank + jnp.where(ahead, 1, 0)
        allowed = (rank < TOP_N) & (jblk < n_blocks) & (jblk * SLC_LEN <= qpos_j)
        bias = jnp.where(allowed, 0.0, MASK_BIAS).astype(BF16)
        pad = jnp.zeros((LANES - HEAD_DIM - SEL_ROWS, tq), BF16)
        for h in range(HPG):
            qsel_ref[g * HPG + h] = jnp.concatenate([qt_ref[g * HPG + h, 0:HEAD_DIM, :], bias, pad], axis=0)

    m_ref[...] = jnp.full(m_ref.shape, NEG_INF, F32)
    acc_ref[...] = jnp.zeros(acc_ref.shape, F32)
    earlier_tiles(0)
    earlier_tiles(1)
    for cd in range(chunks_tile):
        for hh in range(N_HEADS):
            for br in range(2):
                step(br, i * chunks_tile + cd, hh, True)

    def head_out(hh):
        o = oacc_ref[hh]
        for br in range(2):
            acc = acc_ref[br * N_HEADS + hh]
            o = o + gate(hh, br + 1) * (acc[0:HEAD_DIM, :] / acc[HEAD_DIM:HEAD_DIM + 1, :])
        return o

    for pr in range(N_HEADS // 2):
        o_ref[:, pr * LANES:(pr + 1) * LANES] = jnp.concatenate([head_out(2 * pr), head_out(2 * pr + 1)], axis=0).T


def _attn_prompt(qt, gnt, kcm, vcmt, ksa, vst, kwp, vwt, aggt):
    n_batch, _, _, t_len = qt.shape
    tq = TQ
    n_cmp = kcm.shape[2]
    assert tq % KEY_CHUNK == 0 and WINDOW % KEY_CHUNK == 0 and t_len // SLC_LEN <= SEL_ROWS
    assert n_cmp == KV_W, "one lane tile of compressed blocks"
    k_spec = pl.BlockSpec((None, N_KV, t_len, LANES), lambda b, i: (b, 0, 0, 0))
    vt_spec = pl.BlockSpec((None, N_KV, t_len // KEY_CHUNK, V_ROWS, KEY_CHUNK), lambda b, i: (b, 0, 0, 0, 0))
    cmp_spec = pl.BlockSpec((None, N_KV, n_cmp, KV_W), lambda b, i: (b, 0, 0, 0))
    return pl.pallas_call(
        functools.partial(_attn_prompt_kernel, t_len=t_len),
        grid=(n_batch, t_len // tq),
        in_specs=[pl.BlockSpec((None, N_HEADS, LANES, tq), lambda b, i: (b, 0, 0, i)),
                  pl.BlockSpec((None, LANES, tq), lambda b, i: (b, 0, i)),
                  cmp_spec, cmp_spec, k_spec, vt_spec, k_spec, vt_spec, _const_spec(aggt.shape)],
        out_specs=pl.BlockSpec((None, tq, NSA_W), lambda b, i: (b, i, 0)),
        out_shape=jax.ShapeDtypeStruct((n_batch, t_len, NSA_W), F32),
        scratch_shapes=[pltpu.VMEM((N_HEADS, HEAD_DIM, tq), F32), pltpu.VMEM((N_HEADS, LANES, tq), BF16),
                        pltpu.VMEM((2 * N_HEADS, 1, tq), F32), pltpu.VMEM((2 * N_HEADS, V_ROWS, tq), F32)],
        compiler_params=pltpu.CompilerParams(dimension_semantics=("arbitrary", "arbitrary"),
                                             vmem_limit_bytes=VMEM_LIMIT),
        name="attn_prompt",
    )(qt, gnt, kcm, vcmt, ksa, vst, kwp, vwt, aggt)


def _tail_math(x, o_a, sga, c_pre, sgb, sma, smb, w):
    (dwb_ref, clg_ref, clb_ref, pww_ref, pwb_ref, wpa_ref, wpb_ref, wo_ref, fg_ref) = w
    c = c_pre + dwb_ref[...]
    mu = jnp.mean(c, axis=-1, keepdims=True)
    var = jnp.mean(jnp.square(c - mu), axis=-1, keepdims=True)
    cn = (c - mu) * lax.rsqrt(var + EPS) * clg_ref[...] + clb_ref[...]
    cp = _dot(_silu(cn).astype(BF16), pww_ref[...]) + pwb_ref[...]
    br_a = _dot((o_a * sga).astype(BF16), wpa_ref[...])
    br_b = _dot((cp * sgb).astype(BF16), wpb_ref[...])
    h = sma * br_a + smb * br_b
    xo = x + _dot(h.astype(BF16), wo_ref[...])
    return xo * lax.rsqrt(jnp.mean(xo * xo, axis=-1, keepdims=True) + EPS) * fg_ref[...]


def _tail_prompt_kernel(x_ref, oa_ref, sga_ref, u_ref, sgb_ref, sma_ref, smb_ref, dw_ref, *rest):
    w, (y_ref, up_ref, sh_ref) = rest[:9], rest[9:]
    c_pre = _causal_dw_conv(up_ref, sh_ref, u_ref[...], dw_ref, pl.program_id(1) == 0)
    y_ref[...] = _tail_math(x_ref[...], oa_ref[...], sga_ref[...], c_pre, sgb_ref[...], sma_ref[...],
                            smb_ref[...], w)


def _tail_prompt(x, o_a, sga, u, sgb, sma, smb, dw, tail_w):
    n_batch, t_len, _ = x.shape
    tm = TM_TAIL

    def bt(width):
        return pl.BlockSpec((None, tm, width), lambda b, t: (b, t, 0))

    ins = [x, o_a, sga, u, sgb, sma, smb]
    return pl.pallas_call(
        _tail_prompt_kernel,
        grid=(n_batch, t_len // tm),
        in_specs=[bt(a.shape[-1]) for a in ins] + [_const_spec(dw.shape)] + [_const_spec(a.shape) for a in tail_w],
        out_specs=bt(D_MODEL),
        out_shape=jax.ShapeDtypeStruct((n_batch, t_len, D_MODEL), F32),
        scratch_shapes=[pltpu.VMEM((HIST + tm, CONV_C), F32), pltpu.VMEM((SUBLANES, HIST + tm, CONV_C), F32)],
        compiler_params=pltpu.CompilerParams(dimension_semantics=("arbitrary", "arbitrary"),
                                             vmem_limit_bytes=VMEM_LIMIT),
        name="tail_prompt",
    )(*ins, dw, *tail_w)


S_ROWS = 8
UP_ROWS = 40


def _tail_sample_kernel(x_ref, oa_ref, sga_ref, ups_ref, sgb_ref, sma_ref, smb_ref, dws_ref, *rest,
                        n_new):
    w, (y_ref, c_ref) = rest[:9], rest[9:]
    n_batch = ups_ref.shape[0]
    c_ref[...] = jnp.zeros(c_ref.shape, F32)

    def body(b, carry):
        up = ups_ref[b]
        for t in range(n_new):
            c_ref[pl.ds(b * S_ROWS + t, 1), :] = jnp.sum(up * dws_ref[t], axis=0, keepdims=True)
        return carry

    lax.fori_loop(0, n_batch, body, 0)
    y_ref[...] = _tail_math(x_ref[...], oa_ref[...], sga_ref[...], c_ref[...], sgb_ref[...], sma_ref[...],
                            smb_ref[...], w)


def _tail_sample(x, o_a, sga, up_s, sgb, sma, smb, dw_shift, tail_w, n_new):
    ins = [x, o_a, sga, up_s, sgb, sma, smb, dw_shift] + list(tail_w)
    m = x.shape[0]
    return pl.pallas_call(
        functools.partial(_tail_sample_kernel, n_new=n_new),
        grid=(1,),
        in_specs=[_const_spec(a.shape) for a in ins],
        out_specs=_const_spec((m, D_MODEL)),
        out_shape=jax.ShapeDtypeStruct((m, D_MODEL), F32),
        scratch_shapes=[pltpu.VMEM((m, CONV_C), F32)],
        compiler_params=pltpu.CompilerParams(dimension_semantics=("arbitrary",),
                                             vmem_limit_bytes=VMEM_LIMIT),
        name="tail_sample",
    )(*ins)


def _group_q(q_ref, g):
    q = q_ref[g * HPG:(g + 1) * HPG].reshape(HPG * S_ROWS, LANES).astype(F32)
    return q if g == 0 else pltpu.roll(q, HEAD_DIM, axis=1)


def _group_out(o, g):
    return o if g == 0 else pltpu.roll(o, HEAD_DIM, axis=1)


def _scmp_kernel(pt_ref, kpool_ref, vpool_ref, q_ref, perm_ref, pek_ref, w1k_ref, w2k_ref, pev_ref, w1v_ref,
                 w2v_ref, agg_ref, oc_ref, score_ref, ak_ref, av_ref, kbuf_ref, vbuf_ref, sem_ref,
                 *, n_pages_step, n_steps, past_len):
    p = n_pages_step
    b, pg = pl.program_id(0), pl.program_id(1)
    step = b * n_steps + pg
    n_total = pl.num_programs(0) * n_steps
    chunks_page = PAGE_SIZE // CMP_STRIDE
    rows = chunks_page * p

    def page_copies(step_idx, slot, lookup):
        if lookup:
            bb = step_idx // n_steps
            first = (step_idx - bb * n_steps) * p
        for i in range(p):
            page = pt_ref[bb, first + i] if lookup else 0
            for pool_ref, buf_ref in ((kpool_ref, kbuf_ref), (vpool_ref, vbuf_ref)):
                yield pltpu.make_async_copy(pool_ref.at[page], buf_ref.at[slot, i], sem_ref.at[slot])

    @pl.when(step == 0)
    def _():
        for cp in page_copies(step, 0, True):
            cp.start()

    @pl.when(step + 1 < n_total)
    def _():
        for cp in page_copies(step + 1, (step + 1) % 2, True):
            cp.start()

    slot = step % 2
    for cp in page_copies(step, slot, False):
        cp.wait()

    for buf_ref, w1_ref, a_ref in ((kbuf_ref, w1k_ref, ak_ref), (vbuf_ref, w1v_ref, av_ref)):
        blocks = []
        for i in range(p):
            y = _dot_nt(perm_ref[...], buf_ref[slot, i].astype(BF16))
            blocks.append(jnp.concatenate(
                [y[j * chunks_page:(j + 1) * chunks_page] for j in range(CMP_STRIDE)], axis=-1))
        xc = jnp.concatenate(blocks, axis=0).astype(BF16)
        a_ref[pl.ds(pl.multiple_of(pg * rows, rows), rows), :] = _dot(xc, w1_ref[...])

    @pl.when(pg == n_steps - 1)
    def _():
        kc = _compress_from_partials(ak_ref[...], pek_ref, w1k_ref, w2k_ref).astype(BF16)
        vc = _compress_from_partials(av_ref[...], pev_ref, w1v_ref, w2v_ref).astype(BF16)
        n_chunks = kc.shape[0]
        n_blocks = past_len // SLC_LEN + 1
        r32 = HPG * S_ROWS
        qpos32 = past_len + (_iota((r32, n_chunks), 0) & (S_ROWS - 1))
        cvalid = (_iota((r32, n_chunks), 1) * CMP_STRIDE + (CMP_LEN - 1)) <= qpos32
        nb_pad = agg_ref.shape[1]
        jl = _iota((S_ROWS, nb_pad), 1)
        qpos8 = past_len + _iota((S_ROWS, nb_pad), 0)
        for g in range(N_KV):
            qg = _group_q(q_ref, g).astype(BF16)
            p_c = _masked_softmax(_dot_nt(qg, kc), cvalid)
            oc_ref[g] = _group_out(_dot(p_c.astype(BF16), vc), g)
            psum = (p_c[0:S_ROWS] + p_c[S_ROWS:2 * S_ROWS] + p_c[2 * S_ROWS:3 * S_ROWS]
                    + p_c[3 * S_ROWS:4 * S_ROWS])
            imp = _dot(psum.astype(BF16), agg_ref[...])
            score_ref[g] = _block_scores(imp, jl, qpos8, n_blocks)


def _scmp(page_table, kpool, vpool, qp_s, cmp_w, agg_s, past_len):
    n_batch, n_pages = page_table.shape
    p = PAGES_PER_STEP
    n_steps = n_pages // p
    n_chunks = n_pages * (PAGE_SIZE // CMP_STRIDE)

    def cspec(shape):
        nd = len(shape)
        return pl.BlockSpec(shape, lambda b, s, pt: (0,) * nd)

    r_out = jnp.arange(PAGE_SIZE)
    chunks_page = PAGE_SIZE // CMP_STRIDE
    src = (r_out % chunks_page) * CMP_STRIDE + r_out // chunks_page
    perm = (src[:, None] == jnp.arange(PAGE_SIZE)[None, :]).astype(BF16)

    in_specs = ([pl.BlockSpec(memory_space=pl.ANY), pl.BlockSpec(memory_space=pl.ANY),
                 pl.BlockSpec((None, N_HEADS, S_ROWS, LANES), lambda b, s, pt: (0, 0, b, 0))]
                + [cspec(perm.shape)] + [cspec(w.shape) for w in cmp_w] + [cspec(agg_s.shape)])
    page_buf = pltpu.VMEM((2, p, KV_W, PAGE_SIZE), F32)
    nb_pad = agg_s.shape[1]
    out_specs = [pl.BlockSpec((None, N_KV, HPG * S_ROWS, LANES), lambda b, s, pt: (b, 0, 0, 0)),
                 pl.BlockSpec((None, N_KV, S_ROWS, nb_pad), lambda b, s, pt: (b, 0, 0, 0))]
    out_shape = [jax.ShapeDtypeStruct((n_batch, N_KV, HPG * S_ROWS, LANES), F32),
                 jax.ShapeDtypeStruct((n_batch, N_KV, S_ROWS, nb_pad), F32)]
    oc, score = pl.pallas_call(
        functools.partial(_scmp_kernel, n_pages_step=p, n_steps=n_steps, past_len=past_len),
        grid_spec=pltpu.PrefetchScalarGridSpec(
            num_scalar_prefetch=1, grid=(n_batch, n_steps), in_specs=in_specs, out_specs=out_specs,
            scratch_shapes=[pltpu.VMEM((n_chunks, 2 * KV_W), F32)] * 2
            + [page_buf, page_buf, pltpu.SemaphoreType.DMA((2,))]),
        out_shape=out_shape,
        compiler_params=pltpu.CompilerParams(dimension_semantics=("arbitrary", "arbitrary"),
                                             vmem_limit_bytes=VMEM_LIMIT),
        name="sample_cmp",
    )(page_table, kpool, vpool, qp_s, perm, *cmp_w, agg_s)
    return oc, _top_blocks(score.reshape(n_batch * N_KV * S_ROWS, nb_pad))


def _top_blocks_kernel(score_ref, idx_ref):
    score = score_ref[...]
    rows, nb_pad = score.shape
    jlf = _iota((rows, nb_pad), 1).astype(F32)
    lane = _iota((rows, LANES), 1)
    idx = jnp.zeros((rows, LANES), F32)
    for k in range(TOP_N):
        mx = jnp.max(score, axis=-1, keepdims=True)
        am = jnp.min(jnp.where(score == mx, jlf, float(nb_pad)), axis=-1, keepdims=True)
        idx = jnp.where(lane == k, am, idx)
        score = jnp.where(jlf == am, -3.0, score)
    idx_ref[...] = idx.astype(jnp.int32)


def _top_blocks(score):
    rows = score.shape[0]
    return pl.pallas_call(
        _top_blocks_kernel,
        grid=(1,),
        in_specs=[_const_spec(score.shape)],
        out_specs=_const_spec((rows, LANES)),
        out_shape=jax.ShapeDtypeStruct((rows, LANES), jnp.int32),
        compiler_params=pltpu.CompilerParams(dimension_semantics=("arbitrary",), vmem_limit_bytes=VMEM_LIMIT),
        name="sample_topk",
    )(score)


def _ssel_kernel(pt_ref, idx_ref, kpool_ref, vpool_ref, q_ref, ksn_ref, vsn_ref, kwn_ref, vwn_ref,
                 kwb_ref, vwb_ref, gn_ref, oc_ref, o_ref, kbuf_ref, vbuf_ref, sem_ref,
                 *, n_new, past_len):
    b = pl.program_id(0)
    n_pool_blocks = past_len // SLC_LEN
    blocks_page = PAGE_SIZE // SLC_LEN

    page_shift = blocks_page.bit_length() - 1
    assert blocks_page == 1 << page_shift

    def block_copies(bb, ring, lookup):
        for g in range(N_KV):
            for qi in range(n_new):
                for k in range(TOP_N):
                    page = 0
                    if lookup:
                        j = jnp.minimum(idx_ref[bb, (g * n_new + qi) * TOP_N + k], n_pool_blocks - 1)
                        page = pt_ref[bb, lax.shift_right_logical(j, page_shift)]
                    for pool_ref, buf_ref in ((kpool_ref, kbuf_ref), (vpool_ref, vbuf_ref)):
                        yield pltpu.make_async_copy(
                            pool_ref.at[page, pl.ds(g * HEAD_DIM, HEAD_DIM), :],
                            buf_ref.at[ring, g, :, pl.ds((qi * TOP_N + k) * PAGE_SIZE, PAGE_SIZE)],
                            sem_ref.at[ring])

    @pl.when(b == 0)
    def _():
        for cp in block_copies(b, 0, True):
            cp.start()

    @pl.when(b + 1 < pl.num_programs(0))
    def _():
        for cp in block_copies(b + 1, (b + 1) % 2, True):
            cp.start()

    ring = b % 2

    r32 = HPG * S_ROWS
    n_sel = n_new * TOP_N * PAGE_SIZE
    new_rows = LANES
    newcol = _iota((r32, new_rows), 1)
    qrow_new = _iota((r32, new_rows), 0) & (S_ROWS - 1)

    def pad_new(ref):
        return jnp.concatenate([ref[...], jnp.zeros((new_rows - S_ROWS, KV_W), F32)], axis=0).astype(BF16)

    ks_new, vs_new, kw_new, vw_new = pad_new(ksn_ref), pad_new(vsn_ref), pad_new(kwn_ref), pad_new(vwn_ref)

    w_buf = kwb_ref.shape[1]
    wpos = jnp.concatenate([past_len - w_buf + _iota((r32, w_buf), 1), past_len + newcol], axis=1)
    qpos_w = past_len + (_iota((r32, w_buf + new_rows), 0) & (S_ROWS - 1))
    wmask = (wpos <= qpos_w) & (wpos >= qpos_w - WINDOW)
    wmask = wmask & jnp.concatenate([jnp.full((r32, w_buf), True), newcol < n_new], axis=1)
    kw_t = kwb_ref[...].astype(BF16)
    vw_t = vwb_ref[...].astype(BF16)
    gates = _sigmoid(gn_ref[...])

    o_w = []
    q_g = []
    for g in range(N_KV):
        qg = _group_q(q_ref, g).astype(BF16)
        q_g.append(qg)
        s_w = jnp.concatenate([_dot(qg, kw_t), _dot_nt(qg, kw_new)], axis=1)
        p_w = _masked_softmax(s_w, wmask).astype(BF16)
        o_w.append(_group_out(_dot_nt(p_w[:, :w_buf], vw_t) + _dot(p_w[:, w_buf:], vw_new), g))

    for cp in block_copies(b, ring, False):
        cp.wait()

    pad64 = jnp.zeros((r32, LANES - HEAD_DIM), F32)
    row_qi = _iota((r32, PAGE_SIZE), 0) & (S_ROWS - 1)
    lane_part = lax.shift_right_logical(_iota((r32, PAGE_SIZE), 1), SLC_LEN.bit_length() - 1)
    for g in range(N_KV):
        q64 = q_ref[g * HPG:(g + 1) * HPG].reshape(r32, LANES)[:, :HEAD_DIM]
        pieces = []
        new_limit = jnp.zeros((r32, new_rows), jnp.int32)
        for qi in range(n_new):
            n_hit = jnp.int32(0)
            for k in range(TOP_N):
                j = idx_ref[b, (g * n_new + qi) * TOP_N + k]
                part = jnp.where(j < n_pool_blocks, j & (blocks_page - 1), -1)
                pieces.append((row_qi == qi) & (lane_part == part))
                n_hit = n_hit + jnp.where(j == n_pool_blocks, 1, 0)
            new_limit = jnp.where(qrow_new == qi, jnp.where(n_hit > 0, n_new, 0), new_limit)
        nmask = (newcol <= qrow_new) & (newcol < new_limit)
        s_s = jnp.concatenate([_dot(q64, kbuf_ref[ring, g].astype(BF16)), _dot_nt(q_g[g], ks_new)], axis=1)
        p_s = _masked_softmax(s_s, jnp.concatenate(pieces + [nmask], axis=1)).astype(BF16)
        o_pool = _dot_nt(p_s[:, :n_sel], vbuf_ref[ring, g].astype(BF16))
        o_s = jnp.concatenate([o_pool, pad64], axis=1) + _group_out(_dot(p_s[:, n_sel:], vs_new), g)
        o_c = oc_ref[g]
        heads = []
        for h in range(HPG):
            col = (g * HPG + h) * 3
            rs = slice(h * S_ROWS, (h + 1) * S_ROWS)
            heads.append(gates[:, col:col + 1] * o_c[rs] + gates[:, col + 1:col + 2] * o_s[rs]
                         + gates[:, col + 2:col + 3] * o_w[g][rs])
        for pr in range(HPG // 2):
            c0 = (g * HPG + 2 * pr) * HEAD_DIM
            o_ref[:, c0:c0 + LANES] = _pair_heads(heads[2 * pr], heads[2 * pr + 1])


def _ssel(page_table, idx, kpool, vpool, qp_s, ks_n, vs_n, kw_n, vw_n, kw_buf, vw_buf, gn_s, oc,
          n_new, past_len):
    n_batch = page_table.shape[0]
    w_buf = kw_buf.shape[2]
    rows8 = pl.BlockSpec((S_ROWS, KV_W), lambda b, pt, ix: (b, 0))
    wbuf_spec = pl.BlockSpec((None, KV_W, w_buf), lambda b, pt, ix: (b, 0, 0))
    in_specs = [pl.BlockSpec(memory_space=pl.ANY), pl.BlockSpec(memory_space=pl.ANY),
                pl.BlockSpec((None, N_HEADS, S_ROWS, LANES), lambda b, pt, ix: (0, 0, b, 0)),
                rows8, rows8, rows8, rows8, wbuf_spec, wbuf_spec,
                pl.BlockSpec((S_ROWS, LANES), lambda b, pt, ix: (b, 0)),
                pl.BlockSpec((None, N_KV, HPG * S_ROWS, LANES), lambda b, pt, ix: (b, 0, 0, 0))]
    return pl.pallas_call(
        functools.partial(_ssel_kernel, n_new=n_new, past_len=past_len),
        grid_spec=pltpu.PrefetchScalarGridSpec(
            num_scalar_prefetch=2, grid=(n_batch,), in_specs=in_specs,
            out_specs=pl.BlockSpec((S_ROWS, NSA_W), lambda b, pt, ix: (b, 0)),
            scratch_shapes=[pltpu.VMEM((2, N_KV, HEAD_DIM, n_new * TOP_N * PAGE_SIZE), F32)] * 2
            + [pltpu.SemaphoreType.DMA((2,))]),
        out_shape=jax.ShapeDtypeStruct((n_batch * S_ROWS, NSA_W), F32),
        compiler_params=pltpu.CompilerParams(dimension_semantics=("arbitrary",),
                                             vmem_limit_bytes=VMEM_LIMIT),
        name="sample_sel",
    )(page_table, idx, kpool, vpool, qp_s, ks_n, vs_n, kw_n, vw_n, kw_buf, vw_buf, gn_s, oc)


N_GATE = 3 * N_HEADS
GATE_BLOCK = A_END // LANES


def _w_in_kernel(wt_ref, o_ref):
    blk = wt_ref[...].T
    lane = _iota(blk.shape, 1)
    keep = (pl.program_id(0) != GATE_BLOCK) | (lane < N_GATE)
    o_ref[...] = jnp.where(keep, blk, 0.0).astype(BF16)


def _prep_w_in(w_in_t):
    cols, rows = w_in_t.shape
    assert cols == A_END + N_GATE + R_END and A_END % LANES == 0 and R_END % LANES == 0
    n_blocks = (W_REST + R_END) // LANES

    assert N_GATE % SUBLANES == 0

    def first_row(j):
        return pl.multiple_of(j * LANES - jnp.where(j > GATE_BLOCK, LANES - N_GATE, 0), SUBLANES)

    return pl.pallas_call(
        _w_in_kernel,
        grid=(n_blocks,),
        in_specs=[pl.BlockSpec((pl.Element(LANES), pl.Element(rows)), lambda j: (first_row(j), 0))],
        out_specs=pl.BlockSpec((rows, LANES), lambda j: (0, j)),
        out_shape=jax.ShapeDtypeStruct((rows, n_blocks * LANES), BF16),
        compiler_params=pltpu.CompilerParams(dimension_semantics=("arbitrary",), vmem_limit_bytes=VMEM_LIMIT),
        name="w_in_groups",
    )(w_in_t)


def _prep_cmp(pe, w1, w2):
    w1r = w1.reshape(CMP_R, CMP_STRIDE, HEAD_DIM, HEAD_DIM)
    eye = jnp.eye(N_KV, dtype=w1.dtype)
    w1big = jnp.einsum('rjdh,ge->jgdreh', w1r, eye).reshape(CMP_STRIDE * KV_W, CMP_R * KV_W)
    pe_rows = jnp.broadcast_to(pe.reshape(CMP_R, CMP_STRIDE, 1, HEAD_DIM),
                               (CMP_R, CMP_STRIDE, N_KV, HEAD_DIM)).reshape(CMP_R, CMP_STRIDE * KV_W)
    pe_rows = jnp.pad(pe_rows, ((0, SUBLANES - CMP_R), (0, 0)))
    w2big = jnp.einsum('hd,ge->ghed', w2, eye).reshape(KV_W, KV_W)
    return pe_rows.astype(BF16), w1big.astype(BF16), w2big.astype(BF16)


def _agg_matrix(n_rows, n_blocks, n_cols, col0):
    i = jnp.arange(n_rows)[:, None] * CMP_STRIDE
    jj = jnp.arange(n_cols)[None, :] - col0
    hit = (i < jj * SLC_LEN + SLC_LEN) & (i + CMP_LEN > jj * SLC_LEN) & (jj >= 0) & (jj < n_blocks)
    return hit.astype(BF16)


def kernel(x_prompt, x_sample, cache_k_cmp, cache_v_cmp, cache_k_slc, cache_v_slc, cache_k_win, cache_v_win, state_conv, page_table, ln_g, w_in, pe_k, w1_k, w2_k, pe_v, w1_v, w2_v, dw_k, dw_b, cln_g, cln_b, pw_w, pw_b, w_pa, w_pb, w_o, final_g):
    assert w_in.shape[0] == 1, "single-layer stack"
    n_b, t_len, _ = x_prompt.shape
    n_db, n_new, _ = x_sample.shape
    n_pages = page_table.shape[1]
    past_len = n_pages * PAGE_SIZE
    w_buf = cache_k_win.shape[2]
    assert n_new <= S_ROWS and t_len % TQ == 0 and t_len >= WINDOW + TQ
    assert n_new < CMP_STRIDE, "the new rows must not complete a compression chunk"
    assert t_len // SLC_LEN <= LANES - HEAD_DIM and past_len % PAGE_SIZE == 0

    w_all = _prep_w_in(jnp.transpose(w_in[0]))
    lng = ln_g[0].reshape(1, D_MODEL)
    cmp_w = _prep_cmp(pe_k[0], w1_k[0], w2_k[0]) + _prep_cmp(pe_v[0], w1_v[0], w2_v[0])
    tail_w = (dw_b[0].reshape(1, CONV_C), cln_g[0].reshape(1, CONV_C), cln_b[0].reshape(1, CONV_C),
              pw_w[0].astype(BF16), pw_b[0].reshape(1, CONV_C), w_pa[0].astype(BF16), w_pb[0].astype(BF16),
              w_o[0].astype(BF16), final_g.reshape(1, D_MODEL))
    dw = jnp.pad(dw_k[0], ((0, 1), (0, 0)))

    (qt, kct, vct, kst, vst, kwt, vwt, gnt, sga, u, sgb, sma, smb, k_c, v_c, ksa, vst_g, kwp, vwt_g) = _proj(
        x_prompt.reshape(n_b * t_len, D_MODEL), lng, w_all, n_b, t_len, (0, 1), True)
    seq = lambda a: a.reshape(n_b, t_len, a.shape[-1])
    kcm, vcmt = _cmp_prompt(seq(k_c), seq(v_c), cmp_w)
    n_cmp_rows = t_len // CMP_STRIDE
    aggt_p = _agg_matrix(n_cmp_rows, t_len // SLC_LEN, LANES, 0).T
    o_a = _attn_prompt(qt, gnt, kcm, vcmt, ksa, vst_g, kwp, vwt_g, aggt_p)
    y_prompt = _tail_prompt(x_prompt, o_a, seq(sga), seq(u), seq(sgb), seq(sma), seq(smb), dw, tail_w)

    xs = jnp.pad(x_sample, ((0, 0), (0, S_ROWS - n_new), (0, 0))).reshape(n_db * S_ROWS, D_MODEL)
    (qp_s, kct_s, vct_s, kst_s, vst_s, kwt_s, vwt_s, gn_s, sga_s, u_s, sgb_s, sma_s, smb_s,
     k_s_s, v_s_s, k_w_s, v_w_s) = _proj(xs, lng, w_all, 1, n_db * S_ROWS, (2, 3, 4, 5), False)
    pool = lambda c: jnp.transpose(c[0], (0, 2, 3, 1)).reshape(c.shape[1], KV_W, PAGE_SIZE)
    n_blocks_s = past_len // SLC_LEN + 1
    nb_pad = -(-n_blocks_s // LANES) * LANES
    agg_s = _agg_matrix(past_len // CMP_STRIDE, n_blocks_s, nb_pad, 0)
    agg_s = agg_s.at[past_len // CMP_STRIDE - 1].set(0)
    oc_s, idx_s = _scmp(page_table, pool(cache_k_cmp), pool(cache_v_cmp), qp_s, cmp_w, agg_s, past_len)
    idx_flat = idx_s.reshape(n_db, N_KV, S_ROWS, LANES)[:, :, :n_new, :TOP_N].reshape(n_db, N_KV * n_new * TOP_N)
    win = lambda c: jnp.transpose(c[0], (0, 2, 3, 1)).reshape(n_db, KV_W, w_buf)
    o_a_s = _ssel(page_table, idx_flat, pool(cache_k_slc), pool(cache_v_slc), qp_s, k_s_s, v_s_s, k_w_s,
                  v_w_s, win(cache_k_win), win(cache_v_win), gn_s, oc_s, n_new, past_len)
    u_new = u_s.reshape(n_db, S_ROWS, CONV_C)[:, :n_new]
    up_s = jnp.concatenate([state_conv[0], u_new], axis=1)
    up_pad = jnp.pad(up_s, ((0, 0), (0, UP_ROWS - up_s.shape[1]), (0, 0)))
    dw_shift = jnp.stack([jnp.pad(dw_k[0], ((t, UP_ROWS - CONV_W - t), (0, 0))) for t in range(n_new)])
    y_s = _tail_sample(xs, o_a_s, sga_s, up_pad, sgb_s, sma_s, smb_s, dw_shift, tail_w, n_new)
    y_sample = y_s.reshape(n_db, S_ROWS, D_MODEL)[:, :n_new]

    def p_rows(at):
        return jnp.transpose(at.reshape(n_b, N_KV, HEAD_DIM, at.shape[-1]), (0, 3, 1, 2))[None]

    def s_rows(at):
        a = at.reshape(N_KV, HEAD_DIM, n_db, S_ROWS)[:, :, :, :n_new]
        return jnp.transpose(a, (2, 3, 0, 1))[None]

    win_p = min(WINDOW, t_len)
    p_conv = seq(u)[:, -(CONV_W - 1):][None]
    s_k_win = jnp.concatenate([cache_k_win[0], s_rows(kwt_s)[0]], axis=1)[:, -w_buf:][None]
    s_v_win = jnp.concatenate([cache_v_win[0], s_rows(vwt_s)[0]], axis=1)[:, -w_buf:][None]
    s_conv = up_s[:, -(CONV_W - 1):][None]
    return (y_prompt, y_sample, p_rows(kct), p_rows(vct), p_rows(kst), p_rows(vst),
            p_rows(kwt[:, :, -win_p:]), p_rows(vwt[:, :, -win_p:]), p_conv,
            s_rows(kct_s), s_rows(vct_s), s_rows(kst_s), s_rows(vst_s), s_k_win, s_v_win, s_conv)
```

```python
import functools

import jax
import jax.numpy as jnp
from jax import lax
from jax.experimental import pallas as pl
from jax.experimental.pallas import tpu as pltpu

F32 = jnp.float32
BF16 = jnp.bfloat16

D_MODEL = 1024
N_HEADS = 8
HEAD_DIM = 64
N_KV = 2
HPG = N_HEADS // N_KV
NSA_W = N_HEADS * HEAD_DIM
KV_W = N_KV * HEAD_DIM
CMP_LEN = 32
CMP_STRIDE = 16
CMP_R = CMP_LEN // CMP_STRIDE
SLC_LEN = 64
TOP_N = 16
WINDOW = 512
CONV_C = 512
CONV_W = 31
PAGE_SIZE = 128
EPS = 1e-6
FORCED_SCORE = 1e4
NEG_INF = -1e30
MASK_BIAS = -1e4

LANES = 128
SUBLANES = 8
VMEM_LIMIT = 56 * 1024 * 1024

A_KV = NSA_W
A_END = A_KV + 6 * KV_W
R_GLU = NSA_W
R_GB = R_GLU + 2 * CONV_C
R_MRG = R_GB + CONV_C
R_END = R_MRG + 2 * D_MODEL
W_BLOCK = 2 * LANES
W_REST = A_END
W_GN = W_REST + R_END

TM_PROJ = 256
TQ = 256
TM_TAIL = 256
PAGES_PER_STEP = 64


def _sigmoid(x):
    return jax.nn.sigmoid(x)


def _silu(x):
    return jax.nn.silu(x)


def _dot(a, b):
    return jnp.dot(a, b, preferred_element_type=F32)


def _dot_nt(a, b):
    return lax.dot_general(a, b, (((1,), (1,)), ((), ())), preferred_element_type=F32)


def _iota(shape, dim):
    return lax.broadcasted_iota(jnp.int32, shape, dim)


def _const_spec(shape):
    nd = len(shape)
    return pl.BlockSpec(shape, lambda *_: (0,) * nd)


HIST = 32


def _causal_dw_conv(up_ref, sh_ref, u, dw_ref, first_tile):
    tm = u.shape[0]

    @pl.when(first_tile)
    def _():
        up_ref[0:HIST, :] = jnp.zeros((HIST, CONV_C), F32)

    up_ref[HIST:HIST + tm, :] = u
    first = HIST - (CONV_W - 1)
    n_sh = HIST + tm - SUBLANES
    acc = jnp.zeros((tm, CONV_C), F32)
    for phase in range(SUBLANES):
        tiles = [(first + k) // SUBLANES for k in range(CONV_W) if (first + k) % SUBLANES == phase]
        if tiles and phase:
            sh_ref[phase, 0:n_sh, :] = up_ref[pl.ds(phase, n_sh), :]
        for a in tiles:
            k = a * SUBLANES + phase - first
            rows = up_ref[pl.ds(a * SUBLANES, tm), :] if phase == 0 else sh_ref[phase, pl.ds(a * SUBLANES, tm), :]
            acc = acc + rows * dw_ref[k:k + 1, :]
    up_ref[0:HIST, :] = up_ref[tm:tm + HIST, :]
    return acc


def _proj_kernel(x_ref, lng_ref, w_ref, qp_ref, kct_ref, vct_ref, kst_ref, vst_ref,
                 kwt_ref, vwt_ref, gn_ref, sga_ref, u_ref, sgb_ref, sma_ref, smb_ref, *more_refs,
                 tiles_per_batch, row_ids, attn_ops):
    x = x_ref[...]
    tm = x.shape[0]
    xn = x * lax.rsqrt(jnp.mean(x * x, axis=-1, keepdims=True) + EPS) * lng_ref[...]
    xn = xn.astype(BF16)

    zq = _dot(xn, w_ref[:, :A_KV]) * (HEAD_DIM ** -0.5 * (LOG2_E if attn_ops else 1.0))
    zgn = _dot(xn, w_ref[:, W_GN:W_GN + LANES])
    lane = _iota((tm, LANES), 1)
    for pr in range(N_HEADS // 2):
        two = zq[:, pr * LANES:(pr + 1) * LANES]
        if attn_ops:
            two_t = two.T
            pad = jnp.zeros((LANES - HEAD_DIM, tm), F32)
            for o in range(2):
                qp_ref[2 * pr + o] = jnp.concatenate(
                    [two_t[o * HEAD_DIM:(o + 1) * HEAD_DIM], pad], axis=0).astype(BF16)
        else:
            qp_ref[2 * pr] = jnp.where(lane < HEAD_DIM, two, 0.0).astype(BF16)
            qp_ref[2 * pr + 1] = jnp.where(lane < HEAD_DIM, pltpu.roll(two, HEAD_DIM, axis=1), 0.0).astype(BF16)
    gn_ref[...] = zgn.T if attn_ops else zgn

    zkv = _dot(xn, w_ref[:, A_KV:A_END])
    kv = [zkv[:, i * KV_W:(i + 1) * KV_W] for i in range(6)]
    kvt = [v.T for v in kv]
    for r, v in zip((kct_ref, vct_ref, kst_ref, vst_ref, kwt_ref, vwt_ref), kvt):
        r[...] = v
    for r, i in zip(more_refs[:len(row_ids)], row_ids):
        r[...] = kv[i]

    if attn_ops:
        ksa_ref, vst_g_ref, kwp_ref, vwt_g_ref = more_refs[len(row_ids):]
        t0 = (pl.program_id(0) % tiles_per_batch) * tm
        blk = (t0 + _iota((tm, LANES), 0)) >> 6
        onehot = jnp.where((lane - HEAD_DIM) == blk, 1.0, 0.0)
        ones_row = jnp.where(_iota((V_ROWS - HEAD_DIM, tm), 0) == 0, 1.0, 0.0)
        for g in range(N_KV):
            rows_g = slice(g * HEAD_DIM, (g + 1) * HEAD_DIM)

            def grp(v):
                return v if g == 0 else pltpu.roll(v, HEAD_DIM, axis=1)

            ksa_ref[g] = jnp.where(lane < HEAD_DIM, grp(kv[2]), onehot).astype(BF16)
            kwp_ref[g] = grp(kv[4]).astype(BF16)
            for ref, vt in ((vst_g_ref, kvt[3]), (vwt_g_ref, kvt[5])):
                v_aug = jnp.concatenate([vt[rows_g], ones_row], axis=0).astype(BF16)
                for j in range(tm // KEY_CHUNK):
                    ref[g, j] = v_aug[:, j * KEY_CHUNK:(j + 1) * KEY_CHUNK]

    zglu = _dot(xn, w_ref[:, W_REST + R_GLU:W_REST + R_GB])
    u = zglu[:, :CONV_C] * _sigmoid(zglu[:, CONV_C:])
    u_ref[...] = u
    sga_ref[...] = _silu(_dot(xn, w_ref[:, W_REST:W_REST + R_GLU]))
    sgb_ref[...] = _silu(_dot(xn, w_ref[:, W_REST + R_GB:W_REST + R_MRG]))
    zm = _dot(xn, w_ref[:, W_REST + R_MRG:W_REST + R_END])
    sma_ref[...] = _sigmoid(zm[:, :D_MODEL])
    smb_ref[...] = _sigmoid(zm[:, D_MODEL:])


def _proj(x2d, ln_g, w_all, n_batch, t_len, row_ids, attn_ops):
    m = x2d.shape[0]
    tm = min(TM_PROJ, t_len)
    tpb = t_len // tm
    grid = (m // tm,)

    def row_spec(width):
        return pl.BlockSpec((tm, width), lambda i: (i, 0))

    def bt_spec(lead, width):
        return pl.BlockSpec((None, lead, tm, width), lambda i: (i // tpb, 0, i % tpb, 0))

    t_spec = pl.BlockSpec((None, KV_W, tm), lambda i: (i // tpb, 0, i % tpb))
    if attn_ops:
        out_shape = [jax.ShapeDtypeStruct((n_batch, N_HEADS, LANES, t_len), BF16)]
        out_specs = [pl.BlockSpec((None, N_HEADS, LANES, tm), lambda i: (i // tpb, 0, 0, i % tpb))]
    else:
        out_shape = [jax.ShapeDtypeStruct((n_batch, N_HEADS, t_len, LANES), BF16)]
        out_specs = [bt_spec(N_HEADS, LANES)]
    out_shape += [jax.ShapeDtypeStruct((n_batch, KV_W, t_len), F32)] * 6
    out_specs += [t_spec] * 6
    if attn_ops:
        out_shape.append(jax.ShapeDtypeStruct((n_batch, LANES, t_len), F32))
        out_specs.append(t_spec)
    else:
        out_shape.append(jax.ShapeDtypeStruct((m, LANES), F32))
        out_specs.append(row_spec(LANES))
    for width in (NSA_W, CONV_C, CONV_C, D_MODEL, D_MODEL):
        out_shape.append(jax.ShapeDtypeStruct((m, width), F32))
        out_specs.append(row_spec(width))
    out_shape += [jax.ShapeDtypeStruct((m, KV_W), F32)] * len(row_ids)
    out_specs += [row_spec(KV_W)] * len(row_ids)
    if attn_ops:
        cpt = tm // KEY_CHUNK
        k_shape = jax.ShapeDtypeStruct((n_batch, N_KV, t_len, LANES), BF16)
        vt_shape = jax.ShapeDtypeStruct((n_batch, N_KV, t_len // KEY_CHUNK, V_ROWS, KEY_CHUNK), BF16)
        vt_spec = pl.BlockSpec((None, N_KV, cpt, V_ROWS, KEY_CHUNK), lambda i: (i // tpb, 0, i % tpb, 0, 0))
        out_shape += [k_shape, vt_shape, k_shape, vt_shape]
        out_specs += [bt_spec(N_KV, LANES), vt_spec, bt_spec(N_KV, LANES), vt_spec]

    return pl.pallas_call(
        functools.partial(_proj_kernel, tiles_per_batch=tpb, row_ids=row_ids, attn_ops=attn_ops),
        grid=grid,
        in_specs=[row_spec(D_MODEL), _const_spec((1, D_MODEL)), _const_spec(w_all.shape)],
        out_specs=out_specs,
        out_shape=out_shape,
        compiler_params=pltpu.CompilerParams(dimension_semantics=("arbitrary",),
                                             vmem_limit_bytes=VMEM_LIMIT),
        name="proj",
    )(x2d, ln_g, w_all)


def _compress_from_partials(per, pe_ref, w1_ref, w2_ref):
    n = per.shape[0]
    hp = _dot(pe_ref[...], w1_ref[...])
    h0 = hp[0:1, :KV_W] + hp[1:2, KV_W:]
    h = h0 + per[:, :KV_W] + pltpu.roll(per[:, KV_W:], n - 1, axis=0)
    return _dot(_silu(h).astype(BF16), w2_ref[...])


def _chunk_rows(ref, n_chunks):
    return jnp.concatenate([ref[pl.ds(j, n_chunks, stride=CMP_STRIDE), :] for j in range(CMP_STRIDE)],
                           axis=-1)


def _cmp_prompt_kernel(kc_ref, vc_ref, pek_ref, w1k_ref, w2k_ref, pev_ref, w1v_ref, w2v_ref,
                       ok_ref, ov_ref):
    n_chunks = kc_ref.shape[0] // CMP_STRIDE
    for x_ref, pe_ref, w1_ref, w2_ref, o_ref, transposed in (
            (kc_ref, pek_ref, w1k_ref, w2k_ref, ok_ref, False), (vc_ref, pev_ref, w1v_ref, w2v_ref, ov_ref, True)):
        per = _dot(_chunk_rows(x_ref, n_chunks).astype(BF16), w1_ref[...])
        c = _compress_from_partials(per, pe_ref, w1_ref, w2_ref)
        for g, cg in enumerate((c, pltpu.roll(c, HEAD_DIM, axis=1))):
            o_ref[g] = (cg.T if transposed else cg).astype(BF16)


def _cmp_prompt(k_c, v_c, cmp_w):
    n_batch, t_len, _ = k_c.shape
    n_chunks = t_len // CMP_STRIDE
    seq = pl.BlockSpec((None, t_len, KV_W), lambda b: (b, 0, 0))
    out = pl.BlockSpec((None, N_KV, n_chunks, KV_W), lambda b: (b, 0, 0, 0))
    w_specs = [_const_spec(w.shape) for w in cmp_w]
    return pl.pallas_call(
        _cmp_prompt_kernel,
        grid=(n_batch,),
        in_specs=[seq, seq] + w_specs,
        out_specs=[out, out],
        out_shape=[jax.ShapeDtypeStruct((n_batch, N_KV, n_chunks, KV_W), BF16)] * 2,
        compiler_params=pltpu.CompilerParams(dimension_semantics=("arbitrary",),
                                             vmem_limit_bytes=VMEM_LIMIT),
        name="cmp_prompt",
    )(k_c, v_c, *cmp_w)


def _masked_softmax(s, mask):
    s = jnp.where(mask, s, NEG_INF)
    e = jnp.where(mask, jnp.exp(s - jnp.max(s, axis=-1, keepdims=True)), 0.0)
    return e / jnp.maximum(jnp.sum(e, axis=-1, keepdims=True), 1e-30)


def _softmax_rows(s, mask):
    s = jnp.where(mask, s, NEG_INF)
    e = jnp.where(mask, jnp.exp2(s - jnp.max(s, axis=0, keepdims=True)), 0.0)
    return e / jnp.maximum(jnp.sum(e, axis=0, keepdims=True), 1e-30)


def _block_scores(imp, j, qpos, n_blocks):
    cur = qpos >> 6
    forced = (j == 0) | (j == cur) | (j == cur - 1)
    score = jnp.where(forced, FORCED_SCORE, imp)
    score = jnp.where(j * SLC_LEN <= qpos, score, -1.0)
    return jnp.where((j >= 0) & (j < n_blocks), score, -2.0)


def _pair_heads(o_even, o_odd):
    lane = _iota(o_even.shape, 1)
    return jnp.where(lane < HEAD_DIM, o_even, pltpu.roll(o_odd, HEAD_DIM, axis=1))


KEY_CHUNK = 128
V_ROWS = LANES
LOG2_E = 1.4426950408889634
SEL_ROWS = 32


def _attn_prompt_kernel(qt_ref, gnt_ref, kcm_ref, vcmt_ref, ksa_ref, vst_ref, kwp_ref, vwt_ref, aggt_ref,
                        o_ref, oacc_ref, qsel_ref, m_ref, acc_ref, *, t_len):
    tq = gnt_ref.shape[1]
    n_blocks = t_len // SLC_LEN
    i = pl.program_id(1)
    t0 = i * tq
    chunks_tile = tq // KEY_CHUNK

    gates = _sigmoid(gnt_ref[...])
    qpos_c = t0 + _iota((KV_W, tq), 1)
    cvalid = (_iota((KV_W, tq), 0) * CMP_STRIDE + (CMP_LEN - 1)) <= qpos_c
    jblk = _iota((SEL_ROWS, tq), 0)
    qpos_j = t0 + _iota((SEL_ROWS, tq), 1)
    row_minus_lane = _iota((KEY_CHUNK, tq), 0) - _iota((KEY_CHUNK, tq), 1)

    def gate(hh, branch):
        r = hh * 3 + branch
        return gates[r:r + 1, :]

    flash_ops = ((qsel_ref, ksa_ref, vst_ref, 0, False),
                 (qt_ref, kwp_ref, vwt_ref, jnp.maximum(i - WINDOW // tq, 0), True))

    def step(br, c, hh, diagonal):
        q_ref, k_ref, vt_ref, _, banded = flash_ops[br]
        g, st = hh // HPG, br * N_HEADS + hh
        kc = k_ref[g, pl.ds(pl.multiple_of(c * KEY_CHUNK, KEY_CHUNK), KEY_CHUNK), :]
        s = _dot(kc, q_ref[hh])
        off = c * KEY_CHUNK - t0
        if diagonal:
            s = jnp.where(row_minus_lane + off <= 0, s, NEG_INF)
        elif banded:
            s = jnp.where(row_minus_lane + off >= -WINDOW, s, NEG_INF)
        m = m_ref[st]
        m_new = jnp.maximum(m, jnp.max(s, axis=0, keepdims=True))
        p = jnp.exp2(s - m_new).astype(BF16)
        acc_ref[st] = jnp.exp2(m - m_new) * acc_ref[st] + _dot(vt_ref[g, c], p)
        m_ref[st] = m_new

    def earlier_tiles(br):
        t_lo = flash_ops[br][3]

        def tile_steps(t):
            for cd in range(chunks_tile):
                for hh in range(N_HEADS):
                    step(br, t * chunks_tile + cd, hh, False)

        n_pairs = lax.shift_right_logical(i - t_lo, 1)

        def body(u, carry):
            tile_steps(t_lo + 2 * u)
            tile_steps(t_lo + 2 * u + 1)
            return carry

        lax.fori_loop(0, n_pairs, body, 0)

        @pl.when(t_lo + 2 * n_pairs < i)
        def _():
            tile_steps(i - 1)

    for g in range(N_KV):
        psum = None
        for h in range(HPG):
            hh = g * HPG + h
            p_c = _softmax_rows(_dot(kcm_ref[g], qt_ref[hh]), cvalid)
            oacc_ref[hh] = gate(hh, 0) * _dot(vcmt_ref[g, 0:HEAD_DIM, :], p_c.astype(BF16))
            psum = p_c if psum is None else psum + p_c
        imp = _dot(aggt_ref[...], psum.astype(BF16))[:SEL_ROWS]
        score = _block_scores(imp, jblk, qpos_j, n_blocks)
        rank = jnp.zeros((SEL_ROWS, tq), jnp.int32)
        for jp in range(n_blocks):
            other = score[jp:jp + 1, :]
            ahead = (other > score) | ((other == score) & (jblk > jp))
            rank = rank + jnp.where(ahead, 1, 0)
        allowed = (rank < TOP_N) & (jblk < n_blocks) & (jblk * SLC_LEN <= qpos_j)
        bias = jnp.where(allowed, 0.0, MASK_BIAS).astype(BF16)
        pad = jnp.zeros((LANES - HEAD_DIM - SEL_ROWS, tq), BF16)
        for h in range(HPG):
            qsel_ref[g * HPG + h] = jnp.concatenate([qt_ref[g * HPG + h, 0:HEAD_DIM, :], bias, pad], axis=0)

    m_ref[...] = jnp.full(m_ref.shape, NEG_INF, F32)
    acc_ref[...] = jnp.zeros(acc_ref.shape, F32)
    earlier_tiles(0)
    earlier_tiles(1)
    for cd in range(chunks_tile):
        for hh in range(N_HEADS):
            for br in range(2):
                step(br, i * chunks_tile + cd, hh, True)

    def head_out(hh):
        o = oacc_ref[hh]
        for br in range(2):
            acc = acc_ref[br * N_HEADS + hh]
            o = o + gate(hh, br + 1) * (acc[0:HEAD_DIM, :] / acc[HEAD_DIM:HEAD_DIM + 1, :])
        return o

    for pr in range(N_HEADS // 2):
        o_ref[:, pr * LANES:(pr + 1) * LANES] = jnp.concatenate([head_out(2 * pr), head_out(2 * pr + 1)], axis=0).T


def _attn_prompt(qt, gnt, kcm, vcmt, ksa, vst, kwp, vwt, aggt):
    n_batch, _, _, t_len = qt.shape
    tq = TQ
    n_cmp = kcm.shape[2]
    assert tq % KEY_CHUNK == 0 and WINDOW % KEY_CHUNK == 0 and t_len // SLC_LEN <= SEL_ROWS
    assert n_cmp == KV_W, "one lane tile of compressed blocks"
    k_spec = pl.BlockSpec((None, N_KV, t_len, LANES), lambda b, i: (b, 0, 0, 0))
    vt_spec = pl.BlockSpec((None, N_KV, t_len // KEY_CHUNK, V_ROWS, KEY_CHUNK), lambda b, i: (b, 0, 0, 0, 0))
    cmp_spec = pl.BlockSpec((None, N_KV, n_cmp, KV_W), lambda b, i: (b, 0, 0, 0))
    return pl.pallas_call(
        functools.partial(_attn_prompt_kernel, t_len=t_len),
        grid=(n_batch, t_len // tq),
        in_specs=[pl.BlockSpec((None, N_HEADS, LANES, tq), lambda b, i: (b, 0, 0, i)),
                  pl.BlockSpec((None, LANES, tq), lambda b, i: (b, 0, i)),
                  cmp_spec, cmp_spec, k_spec, vt_spec, k_spec, vt_spec, _const_spec(aggt.shape)],
        out_specs=pl.BlockSpec((None, tq, NSA_W), lambda b, i: (b, i, 0)),
        out_shape=jax.ShapeDtypeStruct((n_batch, t_len, NSA_W), F32),
        scratch_shapes=[pltpu.VMEM((N_HEADS, HEAD_DIM, tq), F32), pltpu.VMEM((N_HEADS, LANES, tq), BF16),
                        pltpu.VMEM((2 * N_HEADS, 1, tq), F32), pltpu.VMEM((2 * N_HEADS, V_ROWS, tq), F32)],
        compiler_params=pltpu.CompilerParams(dimension_semantics=("arbitrary", "arbitrary"),
                                             vmem_limit_bytes=VMEM_LIMIT),
        name="attn_prompt",
    )(qt, gnt, kcm, vcmt, ksa, vst, kwp, vwt, aggt)


def _tail_math(x, o_a, sga, c_pre, sgb, sma, smb, w):
    (dwb_ref, clg_ref, clb_ref, pww_ref, pwb_ref, wpa_ref, wpb_ref, wo_ref, fg_ref) = w
    c = c_pre + dwb_ref[...]
    mu = jnp.mean(c, axis=-1, keepdims=True)
    var = jnp.mean(jnp.square(c - mu), axis=-1, keepdims=True)
    cn = (c - mu) * lax.rsqrt(var + EPS) * clg_ref[...] + clb_ref[...]
    cp = _dot(_silu(cn).astype(BF16), pww_ref[...]) + pwb_ref[...]
    br_a = _dot((o_a * sga).astype(BF16), wpa_ref[...])
    br_b = _dot((cp * sgb).astype(BF16), wpb_ref[...])
    h = sma * br_a + smb * br_b
    xo = x + _dot(h.astype(BF16), wo_ref[...])
    return xo * lax.rsqrt(jnp.mean(xo * xo, axis=-1, keepdims=True) + EPS) * fg_ref[...]


def _tail_prompt_kernel(x_ref, oa_ref, sga_ref, u_ref, sgb_ref, sma_ref, smb_ref, dw_ref, *rest):
    w, (y_ref, up_ref, sh_ref) = rest[:9], rest[9:]
    c_pre = _causal_dw_conv(up_ref, sh_ref, u_ref[...], dw_ref, pl.program_id(1) == 0)
    y_ref[...] = _tail_math(x_ref[...], oa_ref[...], sga_ref[...], c_pre, sgb_ref[...], sma_ref[...],
                            smb_ref[...], w)


def _tail_prompt(x, o_a, sga, u, sgb, sma, smb, dw, tail_w):
    n_batch, t_len, _ = x.shape
    tm = TM_TAIL

    def bt(width):
        return pl.BlockSpec((None, tm, width), lambda b, t: (b, t, 0))

    ins = [x, o_a, sga, u, sgb, sma, smb]
    return pl.pallas_call(
        _tail_prompt_kernel,
        grid=(n_batch, t_len // tm),
        in_specs=[bt(a.shape[-1]) for a in ins] + [_const_spec(dw.shape)] + [_const_spec(a.shape) for a in tail_w],
        out_specs=bt(D_MODEL),
        out_shape=jax.ShapeDtypeStruct((n_batch, t_len, D_MODEL), F32),
        scratch_shapes=[pltpu.VMEM((HIST + tm, CONV_C), F32), pltpu.VMEM((SUBLANES, HIST + tm, CONV_C), F32)],
        compiler_params=pltpu.CompilerParams(dimension_semantics=("arbitrary", "arbitrary"),
                                             vmem_limit_bytes=VMEM_LIMIT),
        name="tail_prompt",
    )(*ins, dw, *tail_w)


S_ROWS = 8
UP_ROWS = 40


def _tail_sample_kernel(x_ref, oa_ref, sga_ref, ups_ref, sgb_ref, sma_ref, smb_ref, dws_ref, *rest,
                        n_new):
    w, (y_ref, c_ref) = rest[:9], rest[9:]
    n_batch = ups_ref.shape[0]
    c_ref[...] = jnp.zeros(c_ref.shape, F32)

    def body(b, carry):
        up = ups_ref[b]
        for t in range(n_new):
            c_ref[pl.ds(b * S_ROWS + t, 1), :] = jnp.sum(up * dws_ref[t], axis=0, keepdims=True)
        return carry

    lax.fori_loop(0, n_batch, body, 0)
    y_ref[...] = _tail_math(x_ref[...], oa_ref[...], sga_ref[...], c_ref[...], sgb_ref[...], sma_ref[...],
                            smb_ref[...], w)


def _tail_sample(x, o_a, sga, up_s, sgb, sma, smb, dw_shift, tail_w, n_new):
    ins = [x, o_a, sga, up_s, sgb, sma, smb, dw_shift] + list(tail_w)
    m = x.shape[0]
    return pl.pallas_call(
        functools.partial(_tail_sample_kernel, n_new=n_new),
        grid=(1,),
        in_specs=[_const_spec(a.shape) for a in ins],
        out_specs=_const_spec((m, D_MODEL)),
        out_shape=jax.ShapeDtypeStruct((m, D_MODEL), F32),
        scratch_shapes=[pltpu.VMEM((m, CONV_C), F32)],
        compiler_params=pltpu.CompilerParams(dimension_semantics=("arbitrary",),
                                             vmem_limit_bytes=VMEM_LIMIT),
        name="tail_sample",
    )(*ins)


def _group_q(q_ref, g):
    q = q_ref[g * HPG:(g + 1) * HPG].reshape(HPG * S_ROWS, LANES).astype(F32)
    return q if g == 0 else pltpu.roll(q, HEAD_DIM, axis=1)


def _group_out(o, g):
    return o if g == 0 else pltpu.roll(o, HEAD_DIM, axis=1)


def _scmp_kernel(pt_ref, kpool_ref, vpool_ref, q_ref, perm_ref, pek_ref, w1k_ref, w2k_ref, pev_ref, w1v_ref,
                 w2v_ref, agg_ref, oc_ref, score_ref, ak_ref, av_ref, kbuf_ref, vbuf_ref, sem_ref,
                 *, n_pages_step, n_steps, past_len):
    p = n_pages_step
    b, pg = pl.program_id(0), pl.program_id(1)
    step = b * n_steps + pg
    n_total = pl.num_programs(0) * n_steps
    chunks_page = PAGE_SIZE // CMP_STRIDE
    rows = chunks_page * p

    def page_copies(step_idx, slot, lookup):
        if lookup:
            bb = step_idx // n_steps
            first = (step_idx - bb * n_steps) * p
        for i in range(p):
            page = pt_ref[bb, first + i] if lookup else 0
            for pool_ref, buf_ref in ((kpool_ref, kbuf_ref), (vpool_ref, vbuf_ref)):
                yield pltpu.make_async_copy(pool_ref.at[page], buf_ref.at[slot, i], sem_ref.at[slot])

    @pl.when(step == 0)
    def _():
        for cp in page_copies(step, 0, True):
            cp.start()

    @pl.when(step + 1 < n_total)
    def _():
        for cp in page_copies(step + 1, (step + 1) % 2, True):
            cp.start()

    slot = step % 2
    for cp in page_copies(step, slot, False):
        cp.wait()

    for buf_ref, w1_ref, a_ref in ((kbuf_ref, w1k_ref, ak_ref), (vbuf_ref, w1v_ref, av_ref)):
        blocks = []
        for i in range(p):
            y = _dot_nt(perm_ref[...], buf_ref[slot, i].astype(BF16))
            blocks.append(jnp.concatenate(
                [y[j * chunks_page:(j + 1) * chunks_page] for j in range(CMP_STRIDE)], axis=-1))
        xc = jnp.concatenate(blocks, axis=0).astype(BF16)
        a_ref[pl.ds(pl.multiple_of(pg * rows, rows), rows), :] = _dot(xc, w1_ref[...])

    @pl.when(pg == n_steps - 1)
    def _():
        kc = _compress_from_partials(ak_ref[...], pek_ref, w1k_ref, w2k_ref).astype(BF16)
        vc = _compress_from_partials(av_ref[...], pev_ref, w1v_ref, w2v_ref).astype(BF16)
        n_chunks = kc.shape[0]
        n_blocks = past_len // SLC_LEN + 1
        r32 = HPG * S_ROWS
        qpos32 = past_len + (_iota((r32, n_chunks), 0) & (S_ROWS - 1))
        cvalid = (_iota((r32, n_chunks), 1) * CMP_STRIDE + (CMP_LEN - 1)) <= qpos32
        nb_pad = agg_ref.shape[1]
        jl = _iota((S_ROWS, nb_pad), 1)
        qpos8 = past_len + _iota((S_ROWS, nb_pad), 0)
        for g in range(N_KV):
            qg = _group_q(q_ref, g).astype(BF16)
            p_c = _masked_softmax(_dot_nt(qg, kc), cvalid)
            oc_ref[g] = _group_out(_dot(p_c.astype(BF16), vc), g)
            psum = (p_c[0:S_ROWS] + p_c[S_ROWS:2 * S_ROWS] + p_c[2 * S_ROWS:3 * S_ROWS]
                    + p_c[3 * S_ROWS:4 * S_ROWS])
            imp = _dot(psum.astype(BF16), agg_ref[...])
            score_ref[g] = _block_scores(imp, jl, qpos8, n_blocks)


def _scmp(page_table, kpool, vpool, qp_s, cmp_w, agg_s, past_len):
    n_batch, n_pages = page_table.shape
    p = PAGES_PER_STEP
    n_steps = n_pages // p
    n_chunks = n_pages * (PAGE_SIZE // CMP_STRIDE)

    def cspec(shape):
        nd = len(shape)
        return pl.BlockSpec(shape, lambda b, s, pt: (0,) * nd)

    r_out = jnp.arange(PAGE_SIZE)
    chunks_page = PAGE_SIZE // CMP_STRIDE
    src = (r_out % chunks_page) * CMP_STRIDE + r_out // chunks_page
    perm = (src[:, None] == jnp.arange(PAGE_SIZE)[None, :]).astype(BF16)

    in_specs = ([pl.BlockSpec(memory_space=pl.ANY), pl.BlockSpec(memory_space=pl.ANY),
                 pl.BlockSpec((None, N_HEADS, S_ROWS, LANES), lambda b, s, pt: (0, 0, b, 0))]
                + [cspec(perm.shape)] + [cspec(w.shape) for w in cmp_w] + [cspec(agg_s.shape)])
    page_buf = pltpu.VMEM((2, p, KV_W, PAGE_SIZE), F32)
    nb_pad = agg_s.shape[1]
    out_specs = [pl.BlockSpec((None, N_KV, HPG * S_ROWS, LANES), lambda b, s, pt: (b, 0, 0, 0)),
                 pl.BlockSpec((None, N_KV, S_ROWS, nb_pad), lambda b, s, pt: (b, 0, 0, 0))]
    out_shape = [jax.ShapeDtypeStruct((n_batch, N_KV, HPG * S_ROWS, LANES), F32),
                 jax.ShapeDtypeStruct((n_batch, N_KV, S_ROWS, nb_pad), F32)]
    oc, score = pl.pallas_call(
        functools.partial(_scmp_kernel, n_pages_step=p, n_steps=n_steps, past_len=past_len),
        grid_spec=pltpu.PrefetchScalarGridSpec(
            num_scalar_prefetch=1, grid=(n_batch, n_steps), in_specs=in_specs, out_specs=out_specs,
            scratch_shapes=[pltpu.VMEM((n_chunks, 2 * KV_W), F32)] * 2
            + [page_buf, page_buf, pltpu.SemaphoreType.DMA((2,))]),
        out_shape=out_shape,
        compiler_params=pltpu.CompilerParams(dimension_semantics=("arbitrary", "arbitrary"),
                                             vmem_limit_bytes=VMEM_LIMIT),
        name="sample_cmp",
    )(page_table, kpool, vpool, qp_s, perm, *cmp_w, agg_s)
    return oc, _top_blocks(score.reshape(n_batch * N_KV * S_ROWS, nb_pad))


def _top_blocks_kernel(score_ref, idx_ref):
    score = score_ref[...]
    rows, nb_pad = score.shape
    jlf = _iota((rows, nb_pad), 1).astype(F32)
    lane = _iota((rows, LANES), 1)
    idx = jnp.zeros((rows, LANES), F32)
    for k in range(TOP_N):
        mx = jnp.max(score, axis=-1, keepdims=True)
        am = jnp.min(jnp.where(score == mx, jlf, float(nb_pad)), axis=-1, keepdims=True)
        idx = jnp.where(lane == k, am, idx)
        score = jnp.where(jlf == am, -3.0, score)
    idx_ref[...] = idx.astype(jnp.int32)


def _top_blocks(score):
    rows = score.shape[0]
    return pl.pallas_call(
        _top_blocks_kernel,
        grid=(1,),
        in_specs=[_const_spec(score.shape)],
        out_specs=_const_spec((rows, LANES)),
        out_shape=jax.ShapeDtypeStruct((rows, LANES), jnp.int32),
        compiler_params=pltpu.CompilerParams(dimension_semantics=("arbitrary",), vmem_limit_bytes=VMEM_LIMIT),
        name="sample_topk",
    )(score)


def _ssel_kernel(pt_ref, idx_ref, kpool_ref, vpool_ref, q_ref, ksn_ref, vsn_ref, kwn_ref, vwn_ref,
                 kwb_ref, vwb_ref, gn_ref, oc_ref, o_ref, kbuf_ref, vbuf_ref, sem_ref,
                 *, n_new, past_len):
    b = pl.program_id(0)
    n_pool_blocks = past_len // SLC_LEN
    blocks_page = PAGE_SIZE // SLC_LEN

    page_shift = blocks_page.bit_length() - 1
    assert blocks_page == 1 << page_shift

    def block_copies(bb, ring, lookup):
        for g in range(N_KV):
            for qi in range(n_new):
                for k in range(TOP_N):
                    page = 0
                    if lookup:
                        j = jnp.minimum(idx_ref[bb, (g * n_new + qi) * TOP_N + k], n_pool_blocks - 1)
                        page = pt_ref[bb, lax.shift_right_logical(j, page_shift)]
                    for pool_ref, buf_ref in ((kpool_ref, kbuf_ref), (vpool_ref, vbuf_ref)):
                        yield pltpu.make_async_copy(
                            pool_ref.at[page, pl.ds(g * HEAD_DIM, HEAD_DIM), :],
                            buf_ref.at[ring, g, :, pl.ds((qi * TOP_N + k) * PAGE_SIZE, PAGE_SIZE)],
                            sem_ref.at[ring])

    @pl.when(b == 0)
    def _():
        for cp in block_copies(b, 0, True):
            cp.start()

    @pl.when(b + 1 < pl.num_programs(0))
    def _():
        for cp in block_copies(b + 1, (b + 1) % 2, True):
            cp.start()

    ring = b % 2

    r32 = HPG * S_ROWS
    n_sel = n_new * TOP_N * PAGE_SIZE
    new_rows = LANES
    newcol = _iota((r32, new_rows), 1)
    qrow_new = _iota((r32, new_rows), 0) & (S_ROWS - 1)

    def pad_new(ref):
        return jnp.concatenate([ref[...], jnp.zeros((new_rows - S_ROWS, KV_W), F32)], axis=0).astype(BF16)

    ks_new, vs_new, kw_new, vw_new = pad_new(ksn_ref), pad_new(vsn_ref), pad_new(kwn_ref), pad_new(vwn_ref)

    w_buf = kwb_ref.shape[1]
    wpos = jnp.concatenate([past_len - w_buf + _iota((r32, w_buf), 1), past_len + newcol], axis=1)
    qpos_w = past_len + (_iota((r32, w_buf + new_rows), 0) & (S_ROWS - 1))
    wmask = (wpos <= qpos_w) & (wpos >= qpos_w - WINDOW)
    wmask = wmask & jnp.concatenate([jnp.full((r32, w_buf), True), newcol < n_new], axis=1)
    kw_t = kwb_ref[...].astype(BF16)
    vw_t = vwb_ref[...].astype(BF16)
    gates = _sigmoid(gn_ref[...])

    o_w = []
    q_g = []
    for g in range(N_KV):
        qg = _group_q(q_ref, g).astype(BF16)
        q_g.append(qg)
        s_w = jnp.concatenate([_dot(qg, kw_t), _dot_nt(qg, kw_new)], axis=1)
        p_w = _masked_softmax(s_w, wmask).astype(BF16)
        o_w.append(_group_out(_dot_nt(p_w[:, :w_buf], vw_t) + _dot(p_w[:, w_buf:], vw_new), g))

    for cp in block_copies(b, ring, False):
        cp.wait()

    pad64 = jnp.zeros((r32, LANES - HEAD_DIM), F32)
    row_qi = _iota((r32, PAGE_SIZE), 0) & (S_ROWS - 1)
    lane_part = lax.shift_right_logical(_iota((r32, PAGE_SIZE), 1), SLC_LEN.bit_length() - 1)
    for g in range(N_KV):
        q64 = q_ref[g * HPG:(g + 1) * HPG].reshape(r32, LANES)[:, :HEAD_DIM]
        pieces = []
        new_limit = jnp.zeros((r32, new_rows), jnp.int32)
        for qi in range(n_new):
            n_hit = jnp.int32(0)
            for k in range(TOP_N):
                j = idx_ref[b, (g * n_new + qi) * TOP_N + k]
                part = jnp.where(j < n_pool_blocks, j & (blocks_page - 1), -1)
                pieces.append((row_qi == qi) & (lane_part == part))
                n_hit = n_hit + jnp.where(j == n_pool_blocks, 1, 0)
            new_limit = jnp.where(qrow_new == qi, jnp.where(n_hit > 0, n_new, 0), new_limit)
        nmask = (newcol <= qrow_new) & (newcol < new_limit)
        s_s = jnp.concatenate([_dot(q64, kbuf_ref[ring, g].astype(BF16)), _dot_nt(q_g[g], ks_new)], axis=1)
        p_s = _masked_softmax(s_s, jnp.concatenate(pieces + [nmask], axis=1)).astype(BF16)
        o_pool = _dot_nt(p_s[:, :n_sel], vbuf_ref[ring, g].astype(BF16))
        o_s = jnp.concatenate([o_pool, pad64], axis=1) + _group_out(_dot(p_s[:, n_sel:], vs_new), g)
        o_c = oc_ref[g]
        heads = []
        for h in range(HPG):
            col = (g * HPG + h) * 3
            rs = slice(h * S_ROWS, (h + 1) * S_ROWS)
            heads.append(gates[:, col:col + 1] * o_c[rs] + gates[:, col + 1:col + 2] * o_s[rs]
                         + gates[:, col + 2:col + 3] * o_w[g][rs])
        for pr in range(HPG // 2):
            c0 = (g * HPG + 2 * pr) * HEAD_DIM
            o_ref[:, c0:c0 + LANES] = _pair_heads(heads[2 * pr], heads[2 * pr + 1])


def _ssel(page_table, idx, kpool, vpool, qp_s, ks_n, vs_n, kw_n, vw_n, kw_buf, vw_buf, gn_s, oc,
          n_new, past_len):
    n_batch = page_table.shape[0]
    w_buf = kw_buf.shape[2]
    rows8 = pl.BlockSpec((S_ROWS, KV_W), lambda b, pt, ix: (b, 0))
    wbuf_spec = pl.BlockSpec((None, KV_W, w_buf), lambda b, pt, ix: (b, 0, 0))
    in_specs = [pl.BlockSpec(memory_space=pl.ANY), pl.BlockSpec(memory_space=pl.ANY),
                pl.BlockSpec((None, N_HEADS, S_ROWS, LANES), lambda b, pt, ix: (0, 0, b, 0)),
                rows8, rows8, rows8, rows8, wbuf_spec, wbuf_spec,
                pl.BlockSpec((S_ROWS, LANES), lambda b, pt, ix: (b, 0)),
                pl.BlockSpec((None, N_KV, HPG * S_ROWS, LANES), lambda b, pt, ix: (b, 0, 0, 0))]
    return pl.pallas_call(
        functools.partial(_ssel_kernel, n_new=n_new, past_len=past_len),
        grid_spec=pltpu.PrefetchScalarGridSpec(
            num_scalar_prefetch=2, grid=(n_batch,), in_specs=in_specs,
            out_specs=pl.BlockSpec((S_ROWS, NSA_W), lambda b, pt, ix: (b, 0)),
            scratch_shapes=[pltpu.VMEM((2, N_KV, HEAD_DIM, n_new * TOP_N * PAGE_SIZE), F32)] * 2
            + [pltpu.SemaphoreType.DMA((2,))]),
        out_shape=jax.ShapeDtypeStruct((n_batch * S_ROWS, NSA_W), F32),
        compiler_params=pltpu.CompilerParams(dimension_semantics=("arbitrary",),
                                             vmem_limit_bytes=VMEM_LIMIT),
        name="sample_sel",
    )(page_table, idx, kpool, vpool, qp_s, ks_n, vs_n, kw_n, vw_n, kw_buf, vw_buf, gn_s, oc)


N_GATE = 3 * N_HEADS
GATE_BLOCK = W_GN // W_BLOCK


def _w_in_kernel(wt_ref, o_ref):
    blk = wt_ref[...].T
    lane = _iota(blk.shape, 1)
    keep = (pl.program_id(0) != GATE_BLOCK) | (lane < N_GATE)
    o_ref[...] = jnp.where(keep, blk, 0.0).astype(BF16)


def _prep_w_in(w_in_t):
    cols, rows = w_in_t.shape
    assert cols == A_END + N_GATE + R_END and A_END % W_BLOCK == 0 and R_END % W_BLOCK == 0
    assert N_GATE % SUBLANES == 0 and A_END + W_BLOCK <= cols
    n_blocks = GATE_BLOCK + 1

    def first_row(j):
        r = jnp.where(j == GATE_BLOCK, A_END, j * W_BLOCK + jnp.where(j * W_BLOCK >= A_END, N_GATE, 0))
        return pl.multiple_of(r, SUBLANES)

    return pl.pallas_call(
        _w_in_kernel,
        grid=(n_blocks,),
        in_specs=[pl.BlockSpec((pl.Element(W_BLOCK), pl.Element(rows)), lambda j: (first_row(j), 0))],
        out_specs=pl.BlockSpec((rows, W_BLOCK), lambda j: (0, j)),
        out_shape=jax.ShapeDtypeStruct((rows, n_blocks * W_BLOCK), BF16),
        compiler_params=pltpu.CompilerParams(dimension_semantics=("arbitrary",), vmem_limit_bytes=VMEM_LIMIT),
        name="w_in_groups",
    )(w_in_t)


def _prep_cmp(pe, w1, w2):
    w1r = w1.reshape(CMP_R, CMP_STRIDE, HEAD_DIM, HEAD_DIM)
    eye = jnp.eye(N_KV, dtype=w1.dtype)
    w1big = jnp.einsum('rjdh,ge->jgdreh', w1r, eye).reshape(CMP_STRIDE * KV_W, CMP_R * KV_W)
    pe_rows = jnp.broadcast_to(pe.reshape(CMP_R, CMP_STRIDE, 1, HEAD_DIM),
                               (CMP_R, CMP_STRIDE, N_KV, HEAD_DIM)).reshape(CMP_R, CMP_STRIDE * KV_W)
    pe_rows = jnp.pad(pe_rows, ((0, SUBLANES - CMP_R), (0, 0)))
    w2big = jnp.einsum('hd,ge->ghed', w2, eye).reshape(KV_W, KV_W)
    return pe_rows.astype(BF16), w1big.astype(BF16), w2big.astype(BF16)


def _agg_matrix(n_rows, n_blocks, n_cols, col0):
    i = jnp.arange(n_rows)[:, None] * CMP_STRIDE
    jj = jnp.arange(n_cols)[None, :] - col0
    hit = (i < jj * SLC_LEN + SLC_LEN) & (i + CMP_LEN > jj * SLC_LEN) & (jj >= 0) & (jj < n_blocks)
    return hit.astype(BF16)


def kernel(x_prompt, x_sample, cache_k_cmp, cache_v_cmp, cache_k_slc, cache_v_slc, cache_k_win, cache_v_win, state_conv, page_table, ln_g, w_in, pe_k, w1_k, w2_k, pe_v, w1_v, w2_v, dw_k, dw_b, cln_g, cln_b, pw_w, pw_b, w_pa, w_pb, w_o, final_g):
    assert w_in.shape[0] == 1, "single-layer stack"
    n_b, t_len, _ = x_prompt.shape
    n_db, n_new, _ = x_sample.shape
    n_pages = page_table.shape[1]
    past_len = n_pages * PAGE_SIZE
    w_buf = cache_k_win.shape[2]
    assert n_new <= S_ROWS and t_len % TQ == 0 and t_len >= WINDOW + TQ
    assert n_new < CMP_STRIDE, "the new rows must not complete a compression chunk"
    assert t_len // SLC_LEN <= LANES - HEAD_DIM and past_len % PAGE_SIZE == 0

    w_all = _prep_w_in(jnp.transpose(w_in[0]))
    lng = ln_g[0].reshape(1, D_MODEL)
    cmp_w = _prep_cmp(pe_k[0], w1_k[0], w2_k[0]) + _prep_cmp(pe_v[0], w1_v[0], w2_v[0])
    tail_w = (dw_b[0].reshape(1, CONV_C), cln_g[0].reshape(1, CONV_C), cln_b[0].reshape(1, CONV_C),
              pw_w[0].astype(BF16), pw_b[0].reshape(1, CONV_C), w_pa[0].astype(BF16), w_pb[0].astype(BF16),
              w_o[0].astype(BF16), final_g.reshape(1, D_MODEL))
    dw = jnp.pad(dw_k[0], ((0, 1), (0, 0)))

    (qt, kct, vct, kst, vst, kwt, vwt, gnt, sga, u, sgb, sma, smb, k_c, v_c, ksa, vst_g, kwp, vwt_g) = _proj(
        x_prompt.reshape(n_b * t_len, D_MODEL), lng, w_all, n_b, t_len, (0, 1), True)
    seq = lambda a: a.reshape(n_b, t_len, a.shape[-1])
    kcm, vcmt = _cmp_prompt(seq(k_c), seq(v_c), cmp_w)
    n_cmp_rows = t_len // CMP_STRIDE
    aggt_p = _agg_matrix(n_cmp_rows, t_len // SLC_LEN, LANES, 0).T
    o_a = _attn_prompt(qt, gnt, kcm, vcmt, ksa, vst_g, kwp, vwt_g, aggt_p)
    y_prompt = _tail_prompt(x_prompt, o_a, seq(sga), seq(u), seq(sgb), seq(sma), seq(smb), dw, tail_w)

    xs = jnp.pad(x_sample, ((0, 0), (0, S_ROWS - n_new), (0, 0))).reshape(n_db * S_ROWS, D_MODEL)
    (qp_s, kct_s, vct_s, kst_s, vst_s, kwt_s, vwt_s, gn_s, sga_s, u_s, sgb_s, sma_s, smb_s,
     k_s_s, v_s_s, k_w_s, v_w_s) = _proj(xs, lng, w_all, 1, n_db * S_ROWS, (2, 3, 4, 5), False)
    pool = lambda c: jnp.transpose(c[0], (0, 2, 3, 1)).reshape(c.shape[1], KV_W, PAGE_SIZE)
    n_blocks_s = past_len // SLC_LEN + 1
    nb_pad = -(-n_blocks_s // LANES) * LANES
    agg_s = _agg_matrix(past_len // CMP_STRIDE, n_blocks_s, nb_pad, 0)
    agg_s = agg_s.at[past_len // CMP_STRIDE - 1].set(0)
    oc_s, idx_s = _scmp(page_table, pool(cache_k_cmp), pool(cache_v_cmp), qp_s, cmp_w, agg_s, past_len)
    idx_flat = idx_s.reshape(n_db, N_KV, S_ROWS, LANES)[:, :, :n_new, :TOP_N].reshape(n_db, N_KV * n_new * TOP_N)
    win = lambda c: jnp.transpose(c[0], (0, 2, 3, 1)).reshape(n_db, KV_W, w_buf)
    o_a_s = _ssel(page_table, idx_flat, pool(cache_k_slc), pool(cache_v_slc), qp_s, k_s_s, v_s_s, k_w_s,
                  v_w_s, win(cache_k_win), win(cache_v_win), gn_s, oc_s, n_new, past_len)
    u_new = u_s.reshape(n_db, S_ROWS, CONV_C)[:, :n_new]
    up_s = jnp.concatenate([state_conv[0], u_new], axis=1)
    up_pad = jnp.pad(up_s, ((0, 0), (0, UP_ROWS - up_s.shape[1]), (0, 0)))
    dw_shift = jnp.stack([jnp.pad(dw_k[0], ((t, UP_ROWS - CONV_W - t), (0, 0))) for t in range(n_new)])
    y_s = _tail_sample(xs, o_a_s, sga_s, up_pad, sgb_s, sma_s, smb_s, dw_shift, tail_w, n_new)
    y_sample = y_s.reshape(n_db, S_ROWS, D_MODEL)[:, :n_new]

    def p_rows(at):
        return jnp.transpose(at.reshape(n_b, N_KV, HEAD_DIM, at.shape[-1]), (0, 3, 1, 2))[None]

    def s_rows(at):
        a = at.reshape(N_KV, HEAD_DIM, n_db, S_ROWS)[:, :, :, :n_new]
        return jnp.transpose(a, (2, 3, 0, 1))[None]

    win_p = min(WINDOW, t_len)
    p_conv = seq(u)[:, -(CONV_W - 1):][None]
    s_k_win = jnp.concatenate([cache_k_win[0], s_rows(kwt_s)[0]], axis=1)[:, -w_buf:][None]
    s_v_win = jnp.concatenate([cache_v_win[0], s_rows(vwt_s)[0]], axis=1)[:, -w_buf:][None]
    s_conv = up_s[:, -(CONV_W - 1):][None]
    return (y_prompt, y_sample, p_rows(kct), p_rows(vct), p_rows(kst), p_rows(vst),
            p_rows(kwt[:, :, -win_p:]), p_rows(vwt[:, :, -win_p:]), p_conv,
            s_rows(kct_s), s_rows(vct_s), s_rows(kst_s), s_rows(vst_s), s_k_win, s_v_win, s_conv)
```

```python
import functools

import jax
import jax.numpy as jnp
from jax import lax
from jax.experimental import pallas as pl
from jax.experimental.pallas import tpu as pltpu

F32 = jnp.float32
BF16 = jnp.bfloat16

D_MODEL = 1024
N_HEADS = 8
HEAD_DIM = 64
N_KV = 2
HPG = N_HEADS // N_KV
NSA_W = N_HEADS * HEAD_DIM
KV_W = N_KV * HEAD_DIM
CMP_LEN = 32
CMP_STRIDE = 16
CMP_R = CMP_LEN // CMP_STRIDE
SLC_LEN = 64
TOP_N = 16
WINDOW = 512
CONV_C = 512
CONV_W = 31
PAGE_SIZE = 128
EPS = 1e-6
FORCED_SCORE = 1e4
NEG_INF = -1e30
MASK_BIAS = -1e4

LANES = 128
SUBLANES = 8
VMEM_LIMIT = 56 * 1024 * 1024

A_KV = NSA_W
A_END = A_KV + 6 * KV_W
R_GLU = NSA_W
R_GB = R_GLU + 2 * CONV_C
R_MRG = R_GB + CONV_C
R_END = R_MRG + 2 * D_MODEL
W_BLOCK = 2 * LANES
W_REST = A_END
W_GN = W_REST + R_END

TM_PROJ = 256
TQ = 256
TM_TAIL = 256
PAGES_PER_STEP = 64


def _sigmoid(x):
    return jax.nn.sigmoid(x)


def _silu(x):
    return jax.nn.silu(x)


def _dot(a, b):
    return jnp.dot(a, b, preferred_element_type=F32)


def _dot_nt(a, b):
    return lax.dot_general(a, b, (((1,), (1,)), ((), ())), preferred_element_type=F32)


def _iota(shape, dim):
    return lax.broadcasted_iota(jnp.int32, shape, dim)


def _const_spec(shape):
    nd = len(shape)
    return pl.BlockSpec(shape, lambda *_: (0,) * nd)


HIST = 32


def _causal_dw_conv(up_ref, sh_ref, u, dw_ref, first_tile):
    tm = u.shape[0]

    @pl.when(first_tile)
    def _():
        up_ref[0:HIST, :] = jnp.zeros((HIST, CONV_C), F32)

    up_ref[HIST:HIST + tm, :] = u
    first = HIST - (CONV_W - 1)
    n_sh = HIST + tm - SUBLANES
    acc = jnp.zeros((tm, CONV_C), F32)
    for phase in range(SUBLANES):
        tiles = [(first + k) // SUBLANES for k in range(CONV_W) if (first + k) % SUBLANES == phase]
        if tiles and phase:
            sh_ref[phase, 0:n_sh, :] = up_ref[pl.ds(phase, n_sh), :]
        for a in tiles:
            k = a * SUBLANES + phase - first
            rows = up_ref[pl.ds(a * SUBLANES, tm), :] if phase == 0 else sh_ref[phase, pl.ds(a * SUBLANES, tm), :]
            acc = acc + rows * dw_ref[k:k + 1, :]
    up_ref[0:HIST, :] = up_ref[tm:tm + HIST, :]
    return acc


def _proj_kernel(x_ref, lng_ref, w_ref, qp_ref, kct_ref, vct_ref, kst_ref, vst_ref,
                 kwt_ref, vwt_ref, gn_ref, sga_ref, u_ref, sgb_ref, sma_ref, smb_ref, *more_refs,
                 tiles_per_batch, row_ids, attn_ops):
    x = x_ref[...]
    tm = x.shape[0]
    xn = x * lax.rsqrt(jnp.mean(x * x, axis=-1, keepdims=True) + EPS) * lng_ref[...]
    xn = xn.astype(BF16)

    zq = _dot(xn, w_ref[:, :A_KV]) * (HEAD_DIM ** -0.5 * (LOG2_E if attn_ops else 1.0))
    zgn = _dot(xn, w_ref[:, W_GN:W_GN + LANES])
    lane = _iota((tm, LANES), 1)
    for pr in range(N_HEADS // 2):
        two = zq[:, pr * LANES:(pr + 1) * LANES]
        if attn_ops:
            two_t = two.T
            pad = jnp.zeros((LANES - HEAD_DIM, tm), F32)
            for o in range(2):
                qp_ref[2 * pr + o] = jnp.concatenate(
                    [two_t[o * HEAD_DIM:(o + 1) * HEAD_DIM], pad], axis=0).astype(BF16)
        else:
            qp_ref[2 * pr] = jnp.where(lane < HEAD_DIM, two, 0.0).astype(BF16)
            qp_ref[2 * pr + 1] = jnp.where(lane < HEAD_DIM, pltpu.roll(two, HEAD_DIM, axis=1), 0.0).astype(BF16)
    gn_ref[...] = zgn.T if attn_ops else zgn

    zkv = _dot(xn, w_ref[:, A_KV:A_END])
    kv = [zkv[:, i * KV_W:(i + 1) * KV_W] for i in range(6)]
    kvt = [v.T for v in kv]
    for r, v in zip((kct_ref, vct_ref, kst_ref, vst_ref, kwt_ref, vwt_ref), kvt):
        r[...] = v
    for r, i in zip(more_refs[:len(row_ids)], row_ids):
        r[...] = kv[i]

    if attn_ops:
        ksa_ref, vst_g_ref, kwp_ref, vwt_g_ref = more_refs[len(row_ids):]
        t0 = (pl.program_id(0) % tiles_per_batch) * tm
        blk = (t0 + _iota((tm, LANES), 0)) >> 6
        onehot = jnp.where((lane - HEAD_DIM) == blk, 1.0, 0.0)
        ones_row = jnp.where(_iota((V_ROWS - HEAD_DIM, tm), 0) == 0, 1.0, 0.0)
        for g in range(N_KV):
            rows_g = slice(g * HEAD_DIM, (g + 1) * HEAD_DIM)

            def grp(v):
                return v if g == 0 else pltpu.roll(v, HEAD_DIM, axis=1)

            ksa_ref[g] = jnp.where(lane < HEAD_DIM, grp(kv[2]), onehot).astype(BF16)
            kwp_ref[g] = grp(kv[4]).astype(BF16)
            for ref, vt in ((vst_g_ref, kvt[3]), (vwt_g_ref, kvt[5])):
                v_aug = jnp.concatenate([vt[rows_g], ones_row], axis=0).astype(BF16)
                for j in range(tm // KEY_CHUNK):
                    ref[g, j] = v_aug[:, j * KEY_CHUNK:(j + 1) * KEY_CHUNK]

    zglu = _dot(xn, w_ref[:, W_REST + R_GLU:W_REST + R_GB])
    u = zglu[:, :CONV_C] * _sigmoid(zglu[:, CONV_C:])
    u_ref[...] = u
    sga_ref[...] = _silu(_dot(xn, w_ref[:, W_REST:W_REST + R_GLU]))
    sgb_ref[...] = _silu(_dot(xn, w_ref[:, W_REST + R_GB:W_REST + R_MRG]))
    zm = _dot(xn, w_ref[:, W_REST + R_MRG:W_REST + R_END])
    sma_ref[...] = _sigmoid(zm[:, :D_MODEL])
    smb_ref[...] = _sigmoid(zm[:, D_MODEL:])


def _proj(x2d, ln_g, w_all, n_batch, t_len, row_ids, attn_ops):
    m = x2d.shape[0]
    tm = min(TM_PROJ, t_len)
    tpb = t_len // tm
    grid = (m // tm,)

    def row_spec(width):
        return pl.BlockSpec((tm, width), lambda i: (i, 0))

    def bt_spec(lead, width):
        return pl.BlockSpec((None, lead, tm, width), lambda i: (i // tpb, 0, i % tpb, 0))

    t_spec = pl.BlockSpec((None, KV_W, tm), lambda i: (i // tpb, 0, i % tpb))
    if attn_ops:
        out_shape = [jax.ShapeDtypeStruct((n_batch, N_HEADS, LANES, t_len), BF16)]
        out_specs = [pl.BlockSpec((None, N_HEADS, LANES, tm), lambda i: (i // tpb, 0, 0, i % tpb))]
    else:
        out_shape = [jax.ShapeDtypeStruct((n_batch, N_HEADS, t_len, LANES), BF16)]
        out_specs = [bt_spec(N_HEADS, LANES)]
    out_shape += [jax.ShapeDtypeStruct((n_batch, KV_W, t_len), F32)] * 6
    out_specs += [t_spec] * 6
    if attn_ops:
        out_shape.append(jax.ShapeDtypeStruct((n_batch, LANES, t_len), F32))
        out_specs.append(t_spec)
    else:
        out_shape.append(jax.ShapeDtypeStruct((m, LANES), F32))
        out_specs.append(row_spec(LANES))
    for width in (NSA_W, CONV_C, CONV_C, D_MODEL, D_MODEL):
        out_shape.append(jax.ShapeDtypeStruct((m, width), F32))
        out_specs.append(row_spec(width))
    out_shape += [jax.ShapeDtypeStruct((m, KV_W), F32)] * len(row_ids)
    out_specs += [row_spec(KV_W)] * len(row_ids)
    if attn_ops:
        cpt = tm // KEY_CHUNK
        k_shape = jax.ShapeDtypeStruct((n_batch, N_KV, t_len, LANES), BF16)
        vt_shape = jax.ShapeDtypeStruct((n_batch, N_KV, t_len // KEY_CHUNK, V_ROWS, KEY_CHUNK), BF16)
        vt_spec = pl.BlockSpec((None, N_KV, cpt, V_ROWS, KEY_CHUNK), lambda i: (i // tpb, 0, i % tpb, 0, 0))
        out_shape += [k_shape, vt_shape, k_shape, vt_shape]
        out_specs += [bt_spec(N_KV, LANES), vt_spec, bt_spec(N_KV, LANES), vt_spec]

    return pl.pallas_call(
        functools.partial(_proj_kernel, tiles_per_batch=tpb, row_ids=row_ids, attn_ops=attn_ops),
        grid=grid,
        in_specs=[row_spec(D_MODEL), _const_spec((1, D_MODEL)), _const_spec(w_all.shape)],
        out_specs=out_specs,
        out_shape=out_shape,
        compiler_params=pltpu.CompilerParams(dimension_semantics=("arbitrary",),
                                             vmem_limit_bytes=VMEM_LIMIT),
        name="proj",
    )(x2d, ln_g, w_all)


def _compress_from_partials(per, pe_ref, w1_ref, w2_ref):
    n = per.shape[0]
    hp = _dot(pe_ref[...], w1_ref[...])
    h0 = hp[0:1, :KV_W] + hp[1:2, KV_W:]
    h = h0 + per[:, :KV_W] + pltpu.roll(per[:, KV_W:], n - 1, axis=0)
    return _dot(_silu(h).astype(BF16), w2_ref[...])


def _chunk_rows(ref, n_chunks):
    return jnp.concatenate([ref[pl.ds(j, n_chunks, stride=CMP_STRIDE), :] for j in range(CMP_STRIDE)],
                           axis=-1)


def _cmp_prompt_kernel(kc_ref, vc_ref, pek_ref, w1k_ref, w2k_ref, pev_ref, w1v_ref, w2v_ref,
                       ok_ref, ov_ref):
    n_chunks = kc_ref.shape[0] // CMP_STRIDE
    for x_ref, pe_ref, w1_ref, w2_ref, o_ref, transposed in (
            (kc_ref, pek_ref, w1k_ref, w2k_ref, ok_ref, False), (vc_ref, pev_ref, w1v_ref, w2v_ref, ov_ref, True)):
        per = _dot(_chunk_rows(x_ref, n_chunks).astype(BF16), w1_ref[...])
        c = _compress_from_partials(per, pe_ref, w1_ref, w2_ref)
        for g, cg in enumerate((c, pltpu.roll(c, HEAD_DIM, axis=1))):
            o_ref[g] = (cg.T if transposed else cg).astype(BF16)


def _cmp_prompt(k_c, v_c, cmp_w):
    n_batch, t_len, _ = k_c.shape
    n_chunks = t_len // CMP_STRIDE
    seq = pl.BlockSpec((None, t_len, KV_W), lambda b: (b, 0, 0))
    out = pl.BlockSpec((None, N_KV, n_chunks, KV_W), lambda b: (b, 0, 0, 0))
    w_specs = [_const_spec(w.shape) for w in cmp_w]
    return pl.pallas_call(
        _cmp_prompt_kernel,
        grid=(n_batch,),
        in_specs=[seq, seq] + w_specs,
        out_specs=[out, out],
        out_shape=[jax.ShapeDtypeStruct((n_batch, N_KV, n_chunks, KV_W), BF16)] * 2,
        compiler_params=pltpu.CompilerParams(dimension_semantics=("arbitrary",),
                                             vmem_limit_bytes=VMEM_LIMIT),
        name="cmp_prompt",
    )(k_c, v_c, *cmp_w)


def _masked_softmax(s, mask):
    s = jnp.where(mask, s, NEG_INF)
    e = jnp.where(mask, jnp.exp(s - jnp.max(s, axis=-1, keepdims=True)), 0.0)
    return e / jnp.maximum(jnp.sum(e, axis=-1, keepdims=True), 1e-30)


def _softmax_rows(s, mask):
    s = jnp.where(mask, s, NEG_INF)
    e = jnp.where(mask, jnp.exp2(s - jnp.max(s, axis=0, keepdims=True)), 0.0)
    return e / jnp.maximum(jnp.sum(e, axis=0, keepdims=True), 1e-30)


def _block_scores(imp, j, qpos, n_blocks):
    cur = qpos >> 6
    forced = (j == 0) | (j == cur) | (j == cur - 1)
    score = jnp.where(forced, FORCED_SCORE, imp)
    score = jnp.where(j * SLC_LEN <= qpos, score, -1.0)
    return jnp.where((j >= 0) & (j < n_blocks), score, -2.0)


def _pair_heads(o_even, o_odd):
    lane = _iota(o_even.shape, 1)
    return jnp.where(lane < HEAD_DIM, o_even, pltpu.roll(o_odd, HEAD_DIM, axis=1))


KEY_CHUNK = 128
V_ROWS = LANES
LOG2_E = 1.4426950408889634
SEL_ROWS = 32


def _attn_prompt_kernel(qt_ref, gnt_ref, kcm_ref, vcmt_ref, ksa_ref, vst_ref, kwp_ref, vwt_ref, aggt_ref,
                        o_ref, oacc_ref, qsel_ref, m_ref, acc_ref, *, t_len):
    tq = gnt_ref.shape[1]
    n_blocks = t_len // SLC_LEN
    i = pl.program_id(1)
    t0 = i * tq
    chunks_tile = tq // KEY_CHUNK

    gates = _sigmoid(gnt_ref[...])
    qpos_c = t0 + _iota((KV_W, tq), 1)
    cvalid = (_iota((KV_W, tq), 0) * CMP_STRIDE + (CMP_LEN - 1)) <= qpos_c
    jblk = _iota((SEL_ROWS, tq), 0)
    qpos_j = t0 + _iota((SEL_ROWS, tq), 1)
    row_minus_lane = _iota((KEY_CHUNK, tq), 0) - _iota((KEY_CHUNK, tq), 1)

    def gate(hh, branch):
        r = hh * 3 + branch
        return gates[r:r + 1, :]

    flash_ops = ((qsel_ref, ksa_ref, vst_ref, 0, False),
                 (qt_ref, kwp_ref, vwt_ref, jnp.maximum(i - WINDOW // tq, 0), True))

    def step(br, c, hh, diagonal):
        q_ref, k_ref, vt_ref, _, banded = flash_ops[br]
        g, st = hh // HPG, br * N_HEADS + hh
        kc = k_ref[g, pl.ds(pl.multiple_of(c * KEY_CHUNK, KEY_CHUNK), KEY_CHUNK), :]
        s = _dot(kc, q_ref[hh])
        off = c * KEY_CHUNK - t0
        if diagonal:
            s = jnp.where(row_minus_lane + off <= 0, s, NEG_INF)
        elif banded:
            s = jnp.where(row_minus_lane + off >= -WINDOW, s, NEG_INF)
        m = m_ref[st]
        m_new = jnp.maximum(m, jnp.max(s, axis=0, keepdims=True))
        p = jnp.exp2(s - m_new).astype(BF16)
        acc_ref[st] = jnp.exp2(m - m_new) * acc_ref[st] + _dot(vt_ref[g, c], p)
        m_ref[st] = m_new

    def earlier_tiles(br):
        t_lo = flash_ops[br][3]

        def tile_steps(t):
            for cd in range(chunks_tile):
                for hh in range(N_HEADS):
                    step(br, t * chunks_tile + cd, hh, False)

        n_pairs = lax.shift_right_logical(i - t_lo, 1)

        def body(u, carry):
            tile_steps(t_lo + 2 * u)
            tile_steps(t_lo + 2 * u + 1)
            return carry

        lax.fori_loop(0, n_pairs, body, 0)

        @pl.when(t_lo + 2 * n_pairs < i)
        def _():
            tile_steps(i - 1)

    for g in range(N_KV):
        psum = None
        for h in range(HPG):
            hh = g * HPG + h
            p_c = _softmax_rows(_dot(kcm_ref[g], qt_ref[hh]), cvalid)
            oacc_ref[hh] = gate(hh, 0) * _dot(vcmt_ref[g, 0:HEAD_DIM, :], p_c.astype(BF16))
            psum = p_c if psum is None else psum + p_c
        imp = _dot(aggt_ref[...], psum.astype(BF16))[:SEL_ROWS]
        score = _block_scores(imp, jblk, qpos_j, n_blocks)
        rank = jnp.zeros((SEL_ROWS, tq), jnp.int32)
        for jp in range(n_blocks):
            other = score[jp:jp + 1, :]
            ahead = (other > score) | ((other == score) & (jblk > jp))
            rank = rank + jnp.where(ahead, 1, 0)
        allowed = (rank < TOP_N) & (jblk < n_blocks) & (jblk * SLC_LEN <= qpos_j)
        bias = jnp.where(allowed, 0.0, MASK_BIAS).astype(BF16)
        pad = jnp.zeros((LANES - HEAD_DIM - SEL_ROWS, tq), BF16)
        for h in range(HPG):
            qsel_ref[g * HPG + h] = jnp.concatenate([qt_ref[g * HPG + h, 0:HEAD_DIM, :], bias, pad], axis=0)

    m_ref[...] = jnp.full(m_ref.shape, NEG_INF, F32)
    acc_ref[...] = jnp.zeros(acc_ref.shape, F32)
    earlier_tiles(0)
    earlier_tiles(1)
    for cd in range(chunks_tile):
        for hh in range(N_HEADS):
            for br in range(2):
                step(br, i * chunks_tile + cd, hh, True)

    def head_out(hh):
        o = oacc_ref[hh]
        for br in range(2):
            acc = acc_ref[br * N_HEADS + hh]
            o = o + gate(hh, br + 1) * (acc[0:HEAD_DIM, :] / acc[HEAD_DIM:HEAD_DIM + 1, :])
        return o

    for pr in range(N_HEADS // 2):
        o_ref[:, pr * LANES:(pr + 1) * LANES] = jnp.concatenate([head_out(2 * pr), head_out(2 * pr + 1)], axis=0).T


def _attn_prompt(qt, gnt, kcm, vcmt, ksa, vst, kwp, vwt, aggt):
    n_batch, _, _, t_len = qt.shape
    tq = TQ
    n_cmp = kcm.shape[2]
    assert tq % KEY_CHUNK == 0 and WINDOW % KEY_CHUNK == 0 and t_len // SLC_LEN <= SEL_ROWS
    assert n_cmp == KV_W, "one lane tile of compressed blocks"
    k_spec = pl.BlockSpec((None, N_KV, t_len, LANES), lambda b, i: (b, 0, 0, 0))
    vt_spec = pl.BlockSpec((None, N_KV, t_len // KEY_CHUNK, V_ROWS, KEY_CHUNK), lambda b, i: (b, 0, 0, 0, 0))
    cmp_spec = pl.BlockSpec((None, N_KV, n_cmp, KV_W), lambda b, i: (b, 0, 0, 0))
    return pl.pallas_call(
        functools.partial(_attn_prompt_kernel, t_len=t_len),
        grid=(n_batch, t_len // tq),
        in_specs=[pl.BlockSpec((None, N_HEADS, LANES, tq), lambda b, i: (b, 0, 0, i)),
                  pl.BlockSpec((None, LANES, tq), lambda b, i: (b, 0, i)),
                  cmp_spec, cmp_spec, k_spec, vt_spec, k_spec, vt_spec, _const_spec(aggt.shape)],
        out_specs=pl.BlockSpec((None, tq, NSA_W), lambda b, i: (b, i, 0)),
        out_shape=jax.ShapeDtypeStruct((n_batch, t_len, NSA_W), F32),
        scratch_shapes=[pltpu.VMEM((N_HEADS, HEAD_DIM, tq), F32), pltpu.VMEM((N_HEADS, LANES, tq), BF16),
                        pltpu.VMEM((2 * N_HEADS, 1, tq), F32), pltpu.VMEM((2 * N_HEADS, V_ROWS, tq), F32)],
        compiler_params=pltpu.CompilerParams(dimension_semantics=("arbitrary", "arbitrary"),
                                             vmem_limit_bytes=VMEM_LIMIT),
        name="attn_prompt",
    )(qt, gnt, kcm, vcmt, ksa, vst, kwp, vwt, aggt)


def _tail_math(x, o_a, sga, c_pre, sgb, sma, smb, w):
    (dwb_ref, clg_ref, clb_ref, pww_ref, pwb_ref, wpa_ref, wpb_ref, wo_ref, fg_ref) = w
    c = c_pre + dwb_ref[...]
    mu = jnp.mean(c, axis=-1, keepdims=True)
    var = jnp.mean(jnp.square(c - mu), axis=-1, keepdims=True)
    cn = (c - mu) * lax.rsqrt(var + EPS) * clg_ref[...] + clb_ref[...]
    cp = _dot(_silu(cn).astype(BF16), pww_ref[...]) + pwb_ref[...]
    br_a = _dot((o_a * sga).astype(BF16), wpa_ref[...])
    br_b = _dot((cp * sgb).astype(BF16), wpb_ref[...])
    h = sma * br_a + smb * br_b
    xo = x + _dot(h.astype(BF16), wo_ref[...])
    return xo * lax.rsqrt(jnp.mean(xo * xo, axis=-1, keepdims=True) + EPS) * fg_ref[...]


def _tail_prompt_kernel(x_ref, oa_ref, sga_ref, u_ref, sgb_ref, sma_ref, smb_ref, dw_ref, *rest):
    w, (y_ref, up_ref, sh_ref) = rest[:9], rest[9:]
    c_pre = _causal_dw_conv(up_ref, sh_ref, u_ref[...], dw_ref, pl.program_id(1) == 0)
    y_ref[...] = _tail_math(x_ref[...], oa_ref[...], sga_ref[...], c_pre, sgb_ref[...], sma_ref[...],
                            smb_ref[...], w)


def _tail_prompt(x, o_a, sga, u, sgb, sma, smb, dw, tail_w):
    n_batch, t_len, _ = x.shape
    tm = TM_TAIL

    def bt(width):
        return pl.BlockSpec((None, tm, width), lambda b, t: (b, t, 0))

    ins = [x, o_a, sga, u, sgb, sma, smb]
    return pl.pallas_call(
        _tail_prompt_kernel,
        grid=(n_batch, t_len // tm),
        in_specs=[bt(a.shape[-1]) for a in ins] + [_const_spec(dw.shape)] + [_const_spec(a.shape) for a in tail_w],
        out_specs=bt(D_MODEL),
        out_shape=jax.ShapeDtypeStruct((n_batch, t_len, D_MODEL), F32),
        scratch_shapes=[pltpu.VMEM((HIST + tm, CONV_C), F32), pltpu.VMEM((SUBLANES, HIST + tm, CONV_C), F32)],
        compiler_params=pltpu.CompilerParams(dimension_semantics=("arbitrary", "arbitrary"),
                                             vmem_limit_bytes=VMEM_LIMIT),
        name="tail_prompt",
    )(*ins, dw, *tail_w)


S_ROWS = 8
UP_ROWS = 40


def _tail_sample_kernel(x_ref, oa_ref, sga_ref, ups_ref, sgb_ref, sma_ref, smb_ref, dws_ref, *rest,
                        n_new):
    w, (y_ref, c_ref) = rest[:9], rest[9:]
    n_batch = ups_ref.shape[0]
    c_ref[...] = jnp.zeros(c_ref.shape, F32)

    def body(b, carry):
        up = ups_ref[b]
        for t in range(n_new):
            c_ref[pl.ds(b * S_ROWS + t, 1), :] = jnp.sum(up * dws_ref[t], axis=0, keepdims=True)
        return carry

    lax.fori_loop(0, n_batch, body, 0)
    y_ref[...] = _tail_math(x_ref[...], oa_ref[...], sga_ref[...], c_ref[...], sgb_ref[...], sma_ref[...],
                            smb_ref[...], w)


def _tail_sample(x, o_a, sga, up_s, sgb, sma, smb, dw_shift, tail_w, n_new):
    ins = [x, o_a, sga, up_s, sgb, sma, smb, dw_shift] + list(tail_w)
    m = x.shape[0]
    return pl.pallas_call(
        functools.partial(_tail_sample_kernel, n_new=n_new),
        grid=(1,),
        in_specs=[_const_spec(a.shape) for a in ins],
        out_specs=_const_spec((m, D_MODEL)),
        out_shape=jax.ShapeDtypeStruct((m, D_MODEL), F32),
        scratch_shapes=[pltpu.VMEM((m, CONV_C), F32)],
        compiler_params=pltpu.CompilerParams(dimension_semantics=("arbitrary",),
                                             vmem_limit_bytes=VMEM_LIMIT),
        name="tail_sample",
    )(*ins)


def _group_q(q_ref, g):
    q = q_ref[g * HPG:(g + 1) * HPG].reshape(HPG * S_ROWS, LANES).astype(F32)
    return q if g == 0 else pltpu.roll(q, HEAD_DIM, axis=1)


def _group_out(o, g):
    return o if g == 0 else pltpu.roll(o, HEAD_DIM, axis=1)


def _scmp_kernel(pt_ref, kpool_ref, vpool_ref, q_ref, perm_ref, pek_ref, w1k_ref, w2k_ref, pev_ref, w1v_ref,
                 w2v_ref, agg_ref, oc_ref, score_ref, ak_ref, av_ref, kbuf_ref, vbuf_ref, sem_ref,
                 *, n_pages_step, n_steps, past_len):
    p = n_pages_step
    b, pg = pl.program_id(0), pl.program_id(1)
    step = b * n_steps + pg
    n_total = pl.num_programs(0) * n_steps
    chunks_page = PAGE_SIZE // CMP_STRIDE
    rows = chunks_page * p

    def page_copies(step_idx, slot, lookup):
        if lookup:
            bb = step_idx // n_steps
            first = (step_idx - bb * n_steps) * p
        for i in range(p):
            page = pt_ref[bb, first + i] if lookup else 0
            for pool_ref, buf_ref in ((kpool_ref, kbuf_ref), (vpool_ref, vbuf_ref)):
                yield pltpu.make_async_copy(pool_ref.at[page], buf_ref.at[slot, i], sem_ref.at[slot])

    @pl.when(step == 0)
    def _():
        for n, cp in enumerate(page_copies(step, 0, True)):
            cp.start(priority=n % 2)

    @pl.when(step + 1 < n_total)
    def _():
        for n, cp in enumerate(page_copies(step + 1, (step + 1) % 2, True)):
            cp.start(priority=n % 2)

    slot = step % 2
    for cp in page_copies(step, slot, False):
        cp.wait()

    for buf_ref, w1_ref, a_ref in ((kbuf_ref, w1k_ref, ak_ref), (vbuf_ref, w1v_ref, av_ref)):
        blocks = []
        for i in range(p):
            y = _dot_nt(perm_ref[...], buf_ref[slot, i].astype(BF16))
            blocks.append(jnp.concatenate(
                [y[j * chunks_page:(j + 1) * chunks_page] for j in range(CMP_STRIDE)], axis=-1))
        xc = jnp.concatenate(blocks, axis=0).astype(BF16)
        a_ref[pl.ds(pl.multiple_of(pg * rows, rows), rows), :] = _dot(xc, w1_ref[...])

    @pl.when(pg == n_steps - 1)
    def _():
        kc = _compress_from_partials(ak_ref[...], pek_ref, w1k_ref, w2k_ref).astype(BF16)
        vc = _compress_from_partials(av_ref[...], pev_ref, w1v_ref, w2v_ref).astype(BF16)
        n_chunks = kc.shape[0]
        n_blocks = past_len // SLC_LEN + 1
        r32 = HPG * S_ROWS
        qpos32 = past_len + (_iota((r32, n_chunks), 0) & (S_ROWS - 1))
        cvalid = (_iota((r32, n_chunks), 1) * CMP_STRIDE + (CMP_LEN - 1)) <= qpos32
        nb_pad = agg_ref.shape[1]
        jl = _iota((S_ROWS, nb_pad), 1)
        qpos8 = past_len + _iota((S_ROWS, nb_pad), 0)
        for g in range(N_KV):
            qg = _group_q(q_ref, g).astype(BF16)
            p_c = _masked_softmax(_dot_nt(qg, kc), cvalid)
            oc_ref[g] = _group_out(_dot(p_c.astype(BF16), vc), g)
            psum = (p_c[0:S_ROWS] + p_c[S_ROWS:2 * S_ROWS] + p_c[2 * S_ROWS:3 * S_ROWS]
                    + p_c[3 * S_ROWS:4 * S_ROWS])
            imp = _dot(psum.astype(BF16), agg_ref[...])
            score_ref[g] = _block_scores(imp, jl, qpos8, n_blocks)


def _scmp(page_table, kpool, vpool, qp_s, cmp_w, agg_s, past_len):
    n_batch, n_pages = page_table.shape
    p = PAGES_PER_STEP
    n_steps = n_pages // p
    n_chunks = n_pages * (PAGE_SIZE // CMP_STRIDE)

    def cspec(shape):
        nd = len(shape)
        return pl.BlockSpec(shape, lambda b, s, pt: (0,) * nd)

    r_out = jnp.arange(PAGE_SIZE)
    chunks_page = PAGE_SIZE // CMP_STRIDE
    src = (r_out % chunks_page) * CMP_STRIDE + r_out // chunks_page
    perm = (src[:, None] == jnp.arange(PAGE_SIZE)[None, :]).astype(BF16)

    in_specs = ([pl.BlockSpec(memory_space=pl.ANY), pl.BlockSpec(memory_space=pl.ANY),
                 pl.BlockSpec((None, N_HEADS, S_ROWS, LANES), lambda b, s, pt: (0, 0, b, 0))]
                + [cspec(perm.shape)] + [cspec(w.shape) for w in cmp_w] + [cspec(agg_s.shape)])
    page_buf = pltpu.VMEM((2, p, KV_W, PAGE_SIZE), F32)
    nb_pad = agg_s.shape[1]
    out_specs = [pl.BlockSpec((None, N_KV, HPG * S_ROWS, LANES), lambda b, s, pt: (b, 0, 0, 0)),
                 pl.BlockSpec((None, N_KV, S_ROWS, nb_pad), lambda b, s, pt: (b, 0, 0, 0))]
    out_shape = [jax.ShapeDtypeStruct((n_batch, N_KV, HPG * S_ROWS, LANES), F32),
                 jax.ShapeDtypeStruct((n_batch, N_KV, S_ROWS, nb_pad), F32)]
    oc, score = pl.pallas_call(
        functools.partial(_scmp_kernel, n_pages_step=p, n_steps=n_steps, past_len=past_len),
        grid_spec=pltpu.PrefetchScalarGridSpec(
            num_scalar_prefetch=1, grid=(n_batch, n_steps), in_specs=in_specs, out_specs=out_specs,
            scratch_shapes=[pltpu.VMEM((n_chunks, 2 * KV_W), F32)] * 2
            + [page_buf, page_buf, pltpu.SemaphoreType.DMA((2,))]),
        out_shape=out_shape,
        compiler_params=pltpu.CompilerParams(dimension_semantics=("arbitrary", "arbitrary"),
                                             vmem_limit_bytes=VMEM_LIMIT),
        name="sample_cmp",
    )(page_table, kpool, vpool, qp_s, perm, *cmp_w, agg_s)
    return oc, _top_blocks(score.reshape(n_batch * N_KV * S_ROWS, nb_pad))


def _top_blocks_kernel(score_ref, idx_ref):
    score = score_ref[...]
    rows, nb_pad = score.shape
    jlf = _iota((rows, nb_pad), 1).astype(F32)
    lane = _iota((rows, LANES), 1)
    idx = jnp.zeros((rows, LANES), F32)
    for k in range(TOP_N):
        mx = jnp.max(score, axis=-1, keepdims=True)
        am = jnp.min(jnp.where(score == mx, jlf, float(nb_pad)), axis=-1, keepdims=True)
        idx = jnp.where(lane == k, am, idx)
        score = jnp.where(jlf == am, -3.0, score)
    idx_ref[...] = idx.astype(jnp.int32)


def _top_blocks(score):
    rows = score.shape[0]
    return pl.pallas_call(
        _top_blocks_kernel,
        grid=(1,),
        in_specs=[_const_spec(score.shape)],
        out_specs=_const_spec((rows, LANES)),
        out_shape=jax.ShapeDtypeStruct((rows, LANES), jnp.int32),
        compiler_params=pltpu.CompilerParams(dimension_semantics=("arbitrary",), vmem_limit_bytes=VMEM_LIMIT),
        name="sample_topk",
    )(score)


def _ssel_kernel(pt_ref, idx_ref, kpool_ref, vpool_ref, q_ref, ksn_ref, vsn_ref, kwn_ref, vwn_ref,
                 kwb_ref, vwb_ref, gn_ref, oc_ref, o_ref, kbuf_ref, vbuf_ref, sem_ref,
                 *, n_new, past_len):
    b = pl.program_id(0)
    n_pool_blocks = past_len // SLC_LEN
    blocks_page = PAGE_SIZE // SLC_LEN

    page_shift = blocks_page.bit_length() - 1
    assert blocks_page == 1 << page_shift

    def block_copies(bb, ring, lookup):
        for g in range(N_KV):
            for qi in range(n_new):
                for k in range(TOP_N):
                    page = 0
                    if lookup:
                        j = jnp.minimum(idx_ref[bb, (g * n_new + qi) * TOP_N + k], n_pool_blocks - 1)
                        page = pt_ref[bb, lax.shift_right_logical(j, page_shift)]
                    for pool_ref, buf_ref in ((kpool_ref, kbuf_ref), (vpool_ref, vbuf_ref)):
                        yield pltpu.make_async_copy(
                            pool_ref.at[page, pl.ds(g * HEAD_DIM, HEAD_DIM), :],
                            buf_ref.at[ring, g, :, pl.ds((qi * TOP_N + k) * PAGE_SIZE, PAGE_SIZE)],
                            sem_ref.at[ring])

    @pl.when(b == 0)
    def _():
        for n, cp in enumerate(block_copies(b, 0, True)):
            cp.start(priority=n % 2)

    @pl.when(b + 1 < pl.num_programs(0))
    def _():
        for n, cp in enumerate(block_copies(b + 1, (b + 1) % 2, True)):
            cp.start(priority=n % 2)

    ring = b % 2

    r32 = HPG * S_ROWS
    n_sel = n_new * TOP_N * PAGE_SIZE
    new_rows = LANES
    newcol = _iota((r32, new_rows), 1)
    qrow_new = _iota((r32, new_rows), 0) & (S_ROWS - 1)

    def pad_new(ref):
        return jnp.concatenate([ref[...], jnp.zeros((new_rows - S_ROWS, KV_W), F32)], axis=0).astype(BF16)

    ks_new, vs_new, kw_new, vw_new = pad_new(ksn_ref), pad_new(vsn_ref), pad_new(kwn_ref), pad_new(vwn_ref)

    w_buf = kwb_ref.shape[1]
    wpos = jnp.concatenate([past_len - w_buf + _iota((r32, w_buf), 1), past_len + newcol], axis=1)
    qpos_w = past_len + (_iota((r32, w_buf + new_rows), 0) & (S_ROWS - 1))
    wmask = (wpos <= qpos_w) & (wpos >= qpos_w - WINDOW)
    wmask = wmask & jnp.concatenate([jnp.full((r32, w_buf), True), newcol < n_new], axis=1)
    kw_t = kwb_ref[...].astype(BF16)
    vw_t = vwb_ref[...].astype(BF16)
    gates = _sigmoid(gn_ref[...])

    o_w = []
    q_g = []
    for g in range(N_KV):
        qg = _group_q(q_ref, g).astype(BF16)
        q_g.append(qg)
        s_w = jnp.concatenate([_dot(qg, kw_t), _dot_nt(qg, kw_new)], axis=1)
        p_w = _masked_softmax(s_w, wmask).astype(BF16)
        o_w.append(_group_out(_dot_nt(p_w[:, :w_buf], vw_t) + _dot(p_w[:, w_buf:], vw_new), g))

    for cp in block_copies(b, ring, False):
        cp.wait()

    pad64 = jnp.zeros((r32, LANES - HEAD_DIM), F32)
    row_qi = _iota((r32, PAGE_SIZE), 0) & (S_ROWS - 1)
    lane_part = lax.shift_right_logical(_iota((r32, PAGE_SIZE), 1), SLC_LEN.bit_length() - 1)
    for g in range(N_KV):
        q64 = q_ref[g * HPG:(g + 1) * HPG].reshape(r32, LANES)[:, :HEAD_DIM]
        pieces = []
        new_limit = jnp.zeros((r32, new_rows), jnp.int32)
        for qi in range(n_new):
            n_hit = jnp.int32(0)
            for k in range(TOP_N):
                j = idx_ref[b, (g * n_new + qi) * TOP_N + k]
                part = jnp.where(j < n_pool_blocks, j & (blocks_page - 1), -1)
                pieces.append((row_qi == qi) & (lane_part == part))
                n_hit = n_hit + jnp.where(j == n_pool_blocks, 1, 0)
            new_limit = jnp.where(qrow_new == qi, jnp.where(n_hit > 0, n_new, 0), new_limit)
        nmask = (newcol <= qrow_new) & (newcol < new_limit)
        s_s = jnp.concatenate([_dot(q64, kbuf_ref[ring, g].astype(BF16)), _dot_nt(q_g[g], ks_new)], axis=1)
        p_s = _masked_softmax(s_s, jnp.concatenate(pieces + [nmask], axis=1)).astype(BF16)
        o_pool = _dot_nt(p_s[:, :n_sel], vbuf_ref[ring, g].astype(BF16))
        o_s = jnp.concatenate([o_pool, pad64], axis=1) + _group_out(_dot(p_s[:, n_sel:], vs_new), g)
        o_c = oc_ref[g]
        heads = []
        for h in range(HPG):
            col = (g * HPG + h) * 3
            rs = slice(h * S_ROWS, (h + 1) * S_ROWS)
            heads.append(gates[:, col:col + 1] * o_c[rs] + gates[:, col + 1:col + 2] * o_s[rs]
                         + gates[:, col + 2:col + 3] * o_w[g][rs])
        for pr in range(HPG // 2):
            c0 = (g * HPG + 2 * pr) * HEAD_DIM
            o_ref[:, c0:c0 + LANES] = _pair_heads(heads[2 * pr], heads[2 * pr + 1])


def _ssel(page_table, idx, kpool, vpool, qp_s, ks_n, vs_n, kw_n, vw_n, kw_buf, vw_buf, gn_s, oc,
          n_new, past_len):
    n_batch = page_table.shape[0]
    w_buf = kw_buf.shape[2]
    rows8 = pl.BlockSpec((S_ROWS, KV_W), lambda b, pt, ix: (b, 0))
    wbuf_spec = pl.BlockSpec((None, KV_W, w_buf), lambda b, pt, ix: (b, 0, 0))
    in_specs = [pl.BlockSpec(memory_space=pl.ANY), pl.BlockSpec(memory_space=pl.ANY),
                pl.BlockSpec((None, N_HEADS, S_ROWS, LANES), lambda b, pt, ix: (0, 0, b, 0)),
                rows8, rows8, rows8, rows8, wbuf_spec, wbuf_spec,
                pl.BlockSpec((S_ROWS, LANES), lambda b, pt, ix: (b, 0)),
                pl.BlockSpec((None, N_KV, HPG * S_ROWS, LANES), lambda b, pt, ix: (b, 0, 0, 0))]
    return pl.pallas_call(
        functools.partial(_ssel_kernel, n_new=n_new, past_len=past_len),
        grid_spec=pltpu.PrefetchScalarGridSpec(
            num_scalar_prefetch=2, grid=(n_batch,), in_specs=in_specs,
            out_specs=pl.BlockSpec((S_ROWS, NSA_W), lambda b, pt, ix: (b, 0)),
            scratch_shapes=[pltpu.VMEM((2, N_KV, HEAD_DIM, n_new * TOP_N * PAGE_SIZE), F32)] * 2
            + [pltpu.SemaphoreType.DMA((2,))]),
        out_shape=jax.ShapeDtypeStruct((n_batch * S_ROWS, NSA_W), F32),
        compiler_params=pltpu.CompilerParams(dimension_semantics=("arbitrary",),
                                             vmem_limit_bytes=VMEM_LIMIT),
        name="sample_sel",
    )(page_table, idx, kpool, vpool, qp_s, ks_n, vs_n, kw_n, vw_n, kw_buf, vw_buf, gn_s, oc)


N_GATE = 3 * N_HEADS
GATE_BLOCK = W_GN // W_BLOCK


def _w_in_kernel(wt_ref, o_ref):
    blk = wt_ref[...].T
    lane = _iota(blk.shape, 1)
    keep = (pl.program_id(0) != GATE_BLOCK) | (lane < N_GATE)
    o_ref[...] = jnp.where(keep, blk, 0.0).astype(BF16)


def _prep_w_in(w_in_t):
    cols, rows = w_in_t.shape
    assert cols == A_END + N_GATE + R_END and A_END % W_BLOCK == 0 and R_END % W_BLOCK == 0
    assert N_GATE % SUBLANES == 0 and A_END + W_BLOCK <= cols
    n_blocks = GATE_BLOCK + 1

    def first_row(j):
        r = jnp.where(j == GATE_BLOCK, A_END, j * W_BLOCK + jnp.where(j * W_BLOCK >= A_END, N_GATE, 0))
        return pl.multiple_of(r, SUBLANES)

    return pl.pallas_call(
        _w_in_kernel,
        grid=(n_blocks,),
        in_specs=[pl.BlockSpec((pl.Element(W_BLOCK), pl.Element(rows)), lambda j: (first_row(j), 0))],
        out_specs=pl.BlockSpec((rows, W_BLOCK), lambda j: (0, j)),
        out_shape=jax.ShapeDtypeStruct((rows, n_blocks * W_BLOCK), BF16),
        compiler_params=pltpu.CompilerParams(dimension_semantics=("arbitrary",), vmem_limit_bytes=VMEM_LIMIT),
        name="w_in_groups",
    )(w_in_t)


def _prep_cmp(pe, w1, w2):
    w1r = w1.reshape(CMP_R, CMP_STRIDE, HEAD_DIM, HEAD_DIM)
    eye = jnp.eye(N_KV, dtype=w1.dtype)
    w1big = jnp.einsum('rjdh,ge->jgdreh', w1r, eye).reshape(CMP_STRIDE * KV_W, CMP_R * KV_W)
    pe_rows = jnp.broadcast_to(pe.reshape(CMP_R, CMP_STRIDE, 1, HEAD_DIM),
                               (CMP_R, CMP_STRIDE, N_KV, HEAD_DIM)).reshape(CMP_R, CMP_STRIDE * KV_W)
    pe_rows = jnp.pad(pe_rows, ((0, SUBLANES - CMP_R), (0, 0)))
    w2big = jnp.einsum('hd,ge->ghed', w2, eye).reshape(KV_W, KV_W)
    return pe_rows.astype(BF16), w1big.astype(BF16), w2big.astype(BF16)


def _agg_matrix(n_rows, n_blocks, n_cols, col0):
    i = jnp.arange(n_rows)[:, None] * CMP_STRIDE
    jj = jnp.arange(n_cols)[None, :] - col0
    hit = (i < jj * SLC_LEN + SLC_LEN) & (i + CMP_LEN > jj * SLC_LEN) & (jj >= 0) & (jj < n_blocks)
    return hit.astype(BF16)


def kernel(x_prompt, x_sample, cache_k_cmp, cache_v_cmp, cache_k_slc, cache_v_slc, cache_k_win, cache_v_win, state_conv, page_table, ln_g, w_in, pe_k, w1_k, w2_k, pe_v, w1_v, w2_v, dw_k, dw_b, cln_g, cln_b, pw_w, pw_b, w_pa, w_pb, w_o, final_g):
    assert w_in.shape[0] == 1, "single-layer stack"
    n_b, t_len, _ = x_prompt.shape
    n_db, n_new, _ = x_sample.shape
    n_pages = page_table.shape[1]
    past_len = n_pages * PAGE_SIZE
    w_buf = cache_k_win.shape[2]
    assert n_new <= S_ROWS and t_len % TQ == 0 and t_len >= WINDOW + TQ
    assert n_new < CMP_STRIDE, "the new rows must not complete a compression chunk"
    assert t_len // SLC_LEN <= LANES - HEAD_DIM and past_len % PAGE_SIZE == 0

    w_all = _prep_w_in(jnp.transpose(w_in[0]))
    lng = ln_g[0].reshape(1, D_MODEL)
    cmp_w = _prep_cmp(pe_k[0], w1_k[0], w2_k[0]) + _prep_cmp(pe_v[0], w1_v[0], w2_v[0])
    tail_w = (dw_b[0].reshape(1, CONV_C), cln_g[0].reshape(1, CONV_C), cln_b[0].reshape(1, CONV_C),
              pw_w[0].astype(BF16), pw_b[0].reshape(1, CONV_C), w_pa[0].astype(BF16), w_pb[0].astype(BF16),
              w_o[0].astype(BF16), final_g.reshape(1, D_MODEL))
    dw = jnp.pad(dw_k[0], ((0, 1), (0, 0)))

    (qt, kct, vct, kst, vst, kwt, vwt, gnt, sga, u, sgb, sma, smb, k_c, v_c, ksa, vst_g, kwp, vwt_g) = _proj(
        x_prompt.reshape(n_b * t_len, D_MODEL), lng, w_all, n_b, t_len, (0, 1), True)
    seq = lambda a: a.reshape(n_b, t_len, a.shape[-1])
    kcm, vcmt = _cmp_prompt(seq(k_c), seq(v_c), cmp_w)
    n_cmp_rows = t_len // CMP_STRIDE
    aggt_p = _agg_matrix(n_cmp_rows, t_len // SLC_LEN, LANES, 0).T
    o_a = _attn_prompt(qt, gnt, kcm, vcmt, ksa, vst_g, kwp, vwt_g, aggt_p)
    y_prompt = _tail_prompt(x_prompt, o_a, seq(sga), seq(u), seq(sgb), seq(sma), seq(smb), dw, tail_w)

    xs = jnp.pad(x_sample, ((0, 0), (0, S_ROWS - n_new), (0, 0))).reshape(n_db * S_ROWS, D_MODEL)
    (qp_s, kct_s, vct_s, kst_s, vst_s, kwt_s, vwt_s, gn_s, sga_s, u_s, sgb_s, sma_s, smb_s,
     k_s_s, v_s_s, k_w_s, v_w_s) = _proj(xs, lng, w_all, 1, n_db * S_ROWS, (2, 3, 4, 5), False)
    pool = lambda c: jnp.transpose(c[0], (0, 2, 3, 1)).reshape(c.shape[1], KV_W, PAGE_SIZE)
    n_blocks_s = past_len // SLC_LEN + 1
    nb_pad = -(-n_blocks_s // LANES) * LANES
    agg_s = _agg_matrix(past_len // CMP_STRIDE, n_blocks_s, nb_pad, 0)
    agg_s = agg_s.at[past_len // CMP_STRIDE - 1].set(0)
    oc_s, idx_s = _scmp(page_table, pool(cache_k_cmp), pool(cache_v_cmp), qp_s, cmp_w, agg_s, past_len)
    idx_flat = idx_s.reshape(n_db, N_KV, S_ROWS, LANES)[:, :, :n_new, :TOP_N].reshape(n_db, N_KV * n_new * TOP_N)
    win = lambda c: jnp.transpose(c[0], (0, 2, 3, 1)).reshape(n_db, KV_W, w_buf)
    o_a_s = _ssel(page_table, idx_flat, pool(cache_k_slc), pool(cache_v_slc), qp_s, k_s_s, v_s_s, k_w_s,
                  v_w_s, win(cache_k_win), win(cache_v_win), gn_s, oc_s, n_new, past_len)
    u_new = u_s.reshape(n_db, S_ROWS, CONV_C)[:, :n_new]
    up_s = jnp.concatenate([state_conv[0], u_new], axis=1)
    up_pad = jnp.pad(up_s, ((0, 0), (0, UP_ROWS - up_s.shape[1]), (0, 0)))
    dw_shift = jnp.stack([jnp.pad(dw_k[0], ((t, UP_ROWS - CONV_W - t), (0, 0))) for t in range(n_new)])
    y_s = _tail_sample(xs, o_a_s, sga_s, up_pad, sgb_s, sma_s, smb_s, dw_shift, tail_w, n_new)
    y_sample = y_s.reshape(n_db, S_ROWS, D_MODEL)[:, :n_new]

    def p_rows(at):
        return jnp.transpose(at.reshape(n_b, N_KV, HEAD_DIM, at.shape[-1]), (0, 3, 1, 2))[None]

    def s_rows(at):
        a = at.reshape(N_KV, HEAD_DIM, n_db, S_ROWS)[:, :, :, :n_new]
        return jnp.transpose(a, (2, 3, 0, 1))[None]

    win_p = min(WINDOW, t_len)
    p_conv = seq(u)[:, -(CONV_W - 1):][None]
    s_k_win = jnp.concatenate([cache_k_win[0], s_rows(kwt_s)[0]], axis=1)[:, -w_buf:][None]
    s_v_win = jnp.concatenate([cache_v_win[0], s_rows(vwt_s)[0]], axis=1)[:, -w_buf:][None]
    s_conv = up_s[:, -(CONV_W - 1):][None]
    return (y_prompt, y_sample, p_rows(kct), p_rows(vct), p_rows(kst), p_rows(vst),
            p_rows(kwt[:, :, -win_p:]), p_rows(vwt[:, :, -win_p:]), p_conv,
            s_rows(kct_s), s_rows(vct_s), s_rows(kst_s), s_rows(vst_s), s_k_win, s_v_win, s_conv)
```
